```python
import math
import jax
import jax.numpy as jnp
from jax import lax
import numpy as np

D_MODEL = 2048
BATCH = 4
SEQ = 4096
DEPTH = 1
DEC_BATCH = 32
DEC_SEQ = 4
PAST_LEN = 16384
PAGE_SIZE = 128

M_HEADS = 4
M_DQK = 128
M_DV = 256
M_CONV = 4
M_CHUNK = 64
N_HEADS = 16
N_KV = 4
HEAD_DIM = 64
CMP_BLK = 32
SEL_BLK = 64
N_SELECT = 16
WINDOW = 512
CMP_HIDDEN = 128
NSA_QBLK = 64
FORCE_SCORE = 1e4
N_GROUPS = 4
EXPERTS_PER_GROUP = 8
N_EXPERTS = N_GROUPS * EXPERTS_PER_GROUP
EXPERT_HIDDEN = 512
TOP_K_FINE = 2
MOE_BLK = 128

EPS = 1e-6
POOL_FACTOR = 1.25
M_QK = M_HEADS * M_DQK
M_VW = M_HEADS * M_DV
N_QW = N_HEADS * HEAD_DIM
N_KVW = 6 * N_KV * HEAD_DIM
SPLITS = (M_QK, M_QK, M_VW, M_HEADS, M_HEADS, M_VW, N_QW, N_KVW, 3 * N_HEADS, D_MODEL, D_MODEL)
D_IN = sum(SPLITS)

kernel_name = 'hybrid_mlstm_nsa_hmoe_step'


def rms_norm(x, g):
    xf = x.astype(jnp.float32)
    y = xf * lax.rsqrt(jnp.mean(xf * xf, axis=-1, keepdims=True) + EPS)
    return (y * g.astype(jnp.float32)).astype(x.dtype)


def masked_softmax(s, mask):
    s = jnp.where(mask, s.astype(jnp.float32), -jnp.inf)
    m = jnp.max(s, axis=-1, keepdims=True)
    m = jnp.where(jnp.isfinite(m), m, 0.0)
    e = jnp.where(mask, jnp.exp(s - m), 0.0)
    return e / jnp.maximum(jnp.sum(e, axis=-1, keepdims=True), 1e-30)


def short_conv(u, prev, w):
    T = u.shape[1]
    full = jnp.concatenate([prev.astype(u.dtype), u], axis=1)
    out = full[:, 0:T] * w[0]
    for j in range(1, M_CONV):
        out = out + full[:, j:j + T] * w[j]
    return out, full[:, T:]


def mlstm_chunkwise(q, k, v, i_pre, logf, C0, n0, m0):
    B, T = q.shape[:2]
    L = M_CHUNK if T % M_CHUNK == 0 else T
    nc = T // L

    def chunks(a):
        a = a.reshape((B, nc, L) + a.shape[2:])
        return jnp.moveaxis(jnp.moveaxis(a, 3, 2), 1, 0)

    causal = jnp.tril(jnp.ones((L, L), bool))

    def step(carry, xs):
        C, n, m = carry
        qc, kc, vc, ic, fc = xs
        b = jnp.cumsum(fc, axis=-1)
        logD = jnp.where(causal, b[..., :, None] - b[..., None, :] + ic[..., None, :], -jnp.inf)
        inter = b + m[..., None]
        m_t = jnp.maximum(inter, jnp.max(logD, axis=-1))
        a = jnp.exp(inter - m_t)
        S = jnp.einsum('bhtd,bhsd->bhts', qc, kc) * jnp.exp(logD - m_t[..., None])
        num = a[..., None] * jnp.einsum('bhtd,bhde->bhte', qc, C) + jnp.einsum('bhts,bhse->bhte', S, vc)
        den = a * jnp.einsum('bhtd,bhd->bht', qc, n) + jnp.sum(S, axis=-1)
        h = num / jnp.maximum(jnp.abs(den), jnp.exp(-m_t))[..., None]
        m_new = m_t[..., -1]
        w = jnp.exp(b[..., -1:] - b + ic - m_new[..., None])
        aL = jnp.exp(b[..., -1] + m - m_new)
        C_new = aL[..., None, None] * C + jnp.einsum('bhs,bhsd,bhse->bhde', w, kc, vc)
        n_new = aL[..., None] * n + jnp.einsum('bhs,bhsd->bhd', w, kc)
        return (C_new, n_new, m_new), h

    (C, n, m), h = lax.scan(step, (C0, n0, m0), (chunks(q), chunks(k), chunks(v), chunks(i_pre), chunks(logf)))
    h = jnp.moveaxis(jnp.moveaxis(h, 0, 1), 2, 3).reshape(B, T, M_HEADS, M_DV)
    return h, C, n, m


def compress(rows, pe, w1, w2):
    B, Lk = rows.shape[:2]
    nc = Lk // CMP_BLK
    blk = rows[:, :nc * CMP_BLK].reshape(B, nc, CMP_BLK, N_KV, HEAD_DIM) + pe[:, None, :]
    hid = jax.nn.silu(jnp.einsum('bclgd,ldf->bcgf', blk, w1))
    return jnp.einsum('bcgf,fd->bcgd', hid, w2)


def nsa_attention(q, gates, kv_all, kw_full, q0, g_kc, g_ks, g_kw, pe_ck, w_ck1, w_ck2, pe_cv, w_cv1, w_cv2):
    B, T = q.shape[:2]
    Lk = kv_all.shape[1]
    hpg = N_HEADS // N_KV
    scale = HEAD_DIM ** -0.5
    kc = rms_norm(compress(kv_all[:, :, 0], pe_ck, w_ck1, w_ck2), g_kc)
    vc = compress(kv_all[:, :, 1], pe_cv, w_cv1, w_cv2)
    nc = kc.shape[1]
    n_sel = -(-Lk // SEL_BLK)
    pad = n_sel * SEL_BLK - Lk
    ks = jnp.pad(rms_norm(kv_all[:, :, 2], g_ks), ((0, 0), (0, pad), (0, 0), (0, 0)))
    ks = ks.reshape(B, n_sel, SEL_BLK, N_KV, HEAD_DIM).transpose(0, 3, 1, 2, 4)
    vs = jnp.pad(kv_all[:, :, 3], ((0, 0), (0, pad), (0, 0), (0, 0)))
    vs = vs.reshape(B, n_sel, SEL_BLK, N_KV, HEAD_DIM).transpose(0, 3, 1, 2, 4)
    kw = rms_norm(kw_full[:, :, 0], g_kw)
    vw = kw_full[:, :, 1]
    top = min(N_SELECT, n_sel)
    ratio = SEL_BLK // CMP_BLK
    qb = NSA_QBLK if T % NSA_QBLK == 0 else T
    nqb = T // qb
    qg = q.reshape(B, nqb, qb, N_KV, hpg, HEAD_DIM).transpose(1, 0, 3, 4, 2, 5)
    gg = jnp.moveaxis(gates.reshape(B, nqb, qb, N_HEADS, 3), 1, 0)
    cmp_end = (jnp.arange(nc) + 1) * CMP_BLK - 1
    blk_ids = jnp.arange(n_sel)
    b_ix = jnp.arange(B)[:, None, None, None]
    g_ix = jnp.arange(N_KV)[None, :, None, None]

    def block(args):
        qi, gi, j = args
        s0 = j * qb
        t = q0 + s0 + jnp.arange(qb)
        sc = jnp.einsum('bghqd,bcgd->bghqc', qi, kc) * scale
        p_c = masked_softmax(sc, cmp_end[None, :] <= t[:, None])
        o_c = jnp.einsum('bghqc,bcgd->bghqd', p_c.astype(vc.dtype), vc)
        imp = jnp.sum(p_c, axis=2)
        imp = jnp.pad(imp, ((0, 0), (0, 0), (0, 0), (0, n_sel * ratio - nc)))
        imp = imp.reshape(B, N_KV, qb, n_sel, ratio).sum(-1)
        cur = t // SEL_BLK
        valid = blk_ids[None, :] <= cur[:, None]
        forced = (blk_ids[None, :] == 0) | (blk_ids[None, :] == cur[:, None]) | (blk_ids[None, :] == cur[:, None] - 1)
        imp = jnp.where(valid, jnp.where(forced, FORCE_SCORE, imp), -jnp.inf)
        top_v, top_i = lax.top_k(imp, top)
        k_sel = ks[b_ix, g_ix, top_i]
        v_sel = vs[b_ix, g_ix, top_i]
        pos = top_i[..., None] * SEL_BLK + jnp.arange(SEL_BLK)
        m_s = (pos <= t[:, None, None]) & jnp.isfinite(top_v)[..., None]
        ss = (jnp.einsum('bghqd,bgqnkd->bghqnk', qi, k_sel) * scale).reshape(B, N_KV, hpg, qb, top * SEL_BLK)
        p_s = masked_softmax(ss, m_s.reshape(B, N_KV, 1, qb, top * SEL_BLK))
        o_s = jnp.einsum('bghqnk,bgqnkd->bghqd', p_s.reshape(B, N_KV, hpg, qb, top, SEL_BLK).astype(v_sel.dtype), v_sel)
        kwi = lax.dynamic_slice_in_dim(kw, s0, qb + WINDOW, axis=1)
        vwi = lax.dynamic_slice_in_dim(vw, s0, qb + WINDOW, axis=1)
        pw = q0 - WINDOW + s0 + jnp.arange(qb + WINDOW)
        m_w = (pw[None, :] <= t[:, None]) & (pw[None, :] > t[:, None] - WINDOW) & (pw[None, :] >= 0)
        sw = jnp.einsum('bghqd,bkgd->bghqk', qi, kwi) * scale
        p_w = masked_softmax(sw, m_w)
        o_w = jnp.einsum('bghqk,bkgd->bghqd', p_w.astype(vwi.dtype), vwi)
        o = jnp.stack([o_c, o_s, o_w], axis=-1)
        gi_r = gi.reshape(B, qb, N_KV, hpg, 3).transpose(0, 2, 3, 1, 4)
        return jnp.einsum('bghqdc,bghqc->bqghd', o, gi_r.astype(o.dtype))

    out = lax.map(block, (qg, gg, jnp.arange(nqb)))
    return jnp.moveaxis(out, 0, 1).reshape(B, T, N_QW)


def hier_moe(x, w_group, b_group, w_expert, b_expert, w_e1, w_e3, w_e2):
    N, D = x.shape
    f32 = jnp.float32
    pg = jax.nn.softmax((x @ w_group + b_group).astype(f32), axis=-1)
    grp = jnp.argmax(pg, axis=-1)
    p_grp = jnp.max(pg, axis=-1)
    le = (x @ w_expert + b_expert).astype(f32).reshape(N, N_GROUPS, EXPERTS_PER_GROUP)
    pe = jax.nn.softmax(le[jnp.arange(N), grp], axis=-1)
    top_p, top_i = lax.top_k(pe, TOP_K_FINE)
    wts = p_grp[:, None] * top_p / jnp.sum(top_p, axis=-1, keepdims=True)
    eid = (grp[:, None] * EXPERTS_PER_GROUP + top_i).reshape(-1)
    tok = jnp.repeat(jnp.arange(N), TOP_K_FINE)
    wt = wts.reshape(-1)
    M = N * TOP_K_FINE
    order = jnp.argsort(eid)
    e_s, tok_s, wt_s = eid[order], tok[order], wt[order]
    counts = jnp.zeros((N_EXPERTS,), jnp.int32).at[eid].add(1)
    padded = (counts + MOE_BLK - 1) // MOE_BLK * MOE_BLK
    p_end = jnp.cumsum(padded)
    p_start = p_end - padded
    c_start = jnp.cumsum(counts) - counts
    dest = p_start[e_s] + jnp.arange(M) - c_start[e_s]
    n_blk = (M + N_EXPERTS * (MOE_BLK - 1) + MOE_BLK - 1) // MOE_BLK
    rows = jnp.zeros((n_blk * MOE_BLK, D), x.dtype).at[dest].set(x[tok_s])
    blk_e = jnp.minimum(jnp.searchsorted(p_end, jnp.arange(n_blk) * MOE_BLK, side='right'), N_EXPERTS - 1)

    def expert_block(args):
        xb, e = args
        return (jax.nn.silu(xb @ w_e1[e]) * (xb @ w_e3[e])) @ w_e2[e]

    out = lax.map(expert_block, (rows.reshape(n_blk, MOE_BLK, D), blk_e)).reshape(-1, D)
    return jax.ops.segment_sum(out[dest] * wt_s[:, None].astype(x.dtype), tok_s, num_segments=N)


def hybrid_layer(x, conv_prev, C0, n0, m0, kv_past, win_past, g_mix, w_in, w_conv, b_i, b_f, g_mh, g_q, g_kc, g_ks, g_kw,
                 pe_ck, w_ck1, w_ck2, pe_cv, w_cv1, w_cv2, w_proj_m, w_proj_n, w_out, g_ffn, w_group, b_group,
                 w_expert, b_expert, w_e1, w_e3, w_e2):
    B, T, D = x.shape
    P = kv_past.shape[1]
    WB = win_past.shape[1]
    f32 = jnp.float32
    hx = rms_norm(x, g_mix)
    split_at = np.cumsum(SPLITS)[:-1].tolist()
    mq, mk, mv, mi, mf, mo, nq, nkv, ng, gate_m, gate_n = jnp.split(hx @ w_in, split_at, axis=-1)
    qk, conv_state = short_conv(jnp.concatenate([mq, mk], axis=-1), conv_prev, w_conv)
    qk = jax.nn.silu(qk)
    q_m = qk[..., :M_QK].reshape(B, T, M_HEADS, M_DQK).astype(f32)
    k_m = qk[..., M_QK:].reshape(B, T, M_HEADS, M_DQK).astype(f32) * (M_DQK ** -0.5)
    v_m = mv.reshape(B, T, M_HEADS, M_DV).astype(f32)
    i_pre = (mi + b_i).astype(f32)
    logf = jax.nn.log_sigmoid((mf + b_f).astype(f32))
    h_m, C1, n1, m1 = mlstm_chunkwise(q_m, k_m, v_m, i_pre, logf, C0.astype(f32), n0.astype(f32), m0.astype(f32))
    h_m = rms_norm(h_m, g_mh).reshape(B, T, M_VW).astype(x.dtype) * jax.nn.sigmoid(mo)
    qn = rms_norm(nq.reshape(B, T, N_HEADS, HEAD_DIM), g_q)
    kv_new = nkv.reshape(B, T, 6, N_KV, HEAD_DIM)
    kv_rows = kv_new[:, :, :4]
    kv_all = jnp.concatenate([kv_past.astype(x.dtype), kv_rows], axis=1)
    win_all = jnp.concatenate([win_past.astype(x.dtype), kv_new[:, :, 4:]], axis=1)
    kw_full = jnp.pad(win_all, ((0, 0), (WINDOW - WB, 0), (0, 0), (0, 0), (0, 0)))
    gates_n = jax.nn.sigmoid(ng).reshape(B, T, N_HEADS, 3)
    h_n = nsa_attention(qn, gates_n, kv_all, kw_full, P, g_kc, g_ks, g_kw, pe_ck, w_ck1, w_ck2, pe_cv, w_cv1, w_cv2)
    win_state = win_all[:, -min(WINDOW, WB + T):]
    u = jax.nn.sigmoid(gate_m) * (h_m @ w_proj_m) + jax.nn.sigmoid(gate_n) * (h_n.astype(x.dtype) @ w_proj_n)
    x1 = x + u @ w_out
    h2 = rms_norm(x1, g_ffn).reshape(B * T, D)
    y = x1 + hier_moe(h2, w_group, b_group, w_expert, b_expert, w_e1, w_e3, w_e2).reshape(B, T, D)
    return y, kv_rows, win_state, C1, n1, m1, conv_state


def setup_inputs(seed: int = 0) -> dict:
    key = jax.random.key(seed)
    keys = iter(jax.random.split(key, 48))

    def nrm(shape, scale):
        return jax.random.normal(next(keys), shape, jnp.float32) * scale

    n_pages = PAST_LEN // PAGE_SIZE
    n_phys = int(math.ceil(POOL_FACTOR * DEC_BATCH * n_pages))
    wb = min(WINDOW, PAST_LEN)
    x_prompt = nrm((BATCH, SEQ, D_MODEL), 1.0)
    x_sample = nrm((DEC_BATCH, DEC_SEQ, D_MODEL), 1.0)
    cache_kv = nrm((n_phys, PAGE_SIZE, 4, N_KV, HEAD_DIM), 1.0)
    page_table = jax.random.permutation(next(keys), n_phys)[:DEC_BATCH * n_pages].reshape(DEC_BATCH, n_pages).astype(jnp.int32)
    return {
        'x_prompt': x_prompt,
        'x_sample': x_sample,
        'cache_kv': cache_kv,
        'page_table': page_table,
        'state_win': nrm((DEC_BATCH, wb, 2, N_KV, HEAD_DIM), 1.0),
        'state_C': nrm((DEC_BATCH, M_HEADS, M_DQK, M_DV), M_DQK ** -0.5),
        'state_n': nrm((DEC_BATCH, M_HEADS, M_DQK), M_DQK ** -0.5),
        'state_m': nrm((DEC_BATCH, M_HEADS), 0.5),
        'state_conv': nrm((DEC_BATCH, M_CONV - 1, 2 * M_QK), 1.0),
        'g_mix': 1.0 + nrm((D_MODEL,), 0.05),
        'w_in': nrm((D_MODEL, D_IN), D_MODEL ** -0.5),
        'w_conv': nrm((M_CONV, 2 * M_QK), M_CONV ** -0.5),
        'b_i': nrm((M_HEADS,), 0.5),
        'b_f': 3.0 + nrm((M_HEADS,), 0.5),
        'g_mh': 1.0 + nrm((M_HEADS, M_DV), 0.05),
        'g_q': 1.0 + nrm((HEAD_DIM,), 0.05),
        'g_kc': 1.0 + nrm((HEAD_DIM,), 0.05),
        'g_ks': 1.0 + nrm((HEAD_DIM,), 0.05),
        'g_kw': 1.0 + nrm((HEAD_DIM,), 0.05),
        'pe_ck': nrm((CMP_BLK, HEAD_DIM), 0.1),
        'w_ck1': nrm((CMP_BLK, HEAD_DIM, CMP_HIDDEN), (CMP_BLK * HEAD_DIM) ** -0.5),
        'w_ck2': nrm((CMP_HIDDEN, HEAD_DIM), CMP_HIDDEN ** -0.5),
        'pe_cv': nrm((CMP_BLK, HEAD_DIM), 0.1),
        'w_cv1': nrm((CMP_BLK, HEAD_DIM, CMP_HIDDEN), (CMP_BLK * HEAD_DIM) ** -0.5),
        'w_cv2': nrm((CMP_HIDDEN, HEAD_DIM), CMP_HIDDEN ** -0.5),
        'w_proj_m': nrm((M_VW, D_MODEL), M_VW ** -0.5),
        'w_proj_n': nrm((N_QW, D_MODEL), N_QW ** -0.5),
        'w_out': nrm((D_MODEL, D_MODEL), D_MODEL ** -0.5),
        'g_ffn': 1.0 + nrm((D_MODEL,), 0.05),
        'w_group': nrm((D_MODEL, N_GROUPS), D_MODEL ** -0.5),
        'b_group': nrm((N_GROUPS,), 0.01),
        'w_expert': nrm((D_MODEL, N_EXPERTS), D_MODEL ** -0.5),
        'b_expert': nrm((N_EXPERTS,), 0.01),
        'w_e1': nrm((N_EXPERTS, D_MODEL, EXPERT_HIDDEN), D_MODEL ** -0.5),
        'w_e3': nrm((N_EXPERTS, D_MODEL, EXPERT_HIDDEN), D_MODEL ** -0.5),
        'w_e2': nrm((N_EXPERTS, EXPERT_HIDDEN, D_MODEL), EXPERT_HIDDEN ** -0.5),
    }


def reference(x_prompt, x_sample, cache_kv, page_table, state_win, state_C, state_n, state_m, state_conv,
              g_mix, w_in, w_conv, b_i, b_f, g_mh, g_q, g_kc, g_ks, g_kw, pe_ck, w_ck1, w_ck2, pe_cv, w_cv1, w_cv2,
              w_proj_m, w_proj_n, w_out, g_ffn, w_group, b_group, w_expert, b_expert, w_e1, w_e3, w_e2):
    B = x_prompt.shape[0]
    DB = x_sample.shape[0]
    n_pages = page_table.shape[1]
    dt = x_prompt.dtype
    f32 = jnp.float32
    weights = (g_mix, w_in, w_conv, b_i, b_f, g_mh, g_q, g_kc, g_ks, g_kw, pe_ck, w_ck1, w_ck2, pe_cv, w_cv1, w_cv2,
               w_proj_m, w_proj_n, w_out, g_ffn, w_group, b_group, w_expert, b_expert, w_e1, w_e3, w_e2)
    y_p, y_s = x_prompt, x_sample
    for _ in range(DEPTH):
        y_p, kv_p, win_p, C_p, n_p, m_p, conv_p = hybrid_layer(
            y_p, jnp.zeros((B, M_CONV - 1, 2 * M_QK), dt), jnp.zeros((B, M_HEADS, M_DQK, M_DV), f32),
            jnp.zeros((B, M_HEADS, M_DQK), f32), jnp.zeros((B, M_HEADS), f32),
            jnp.zeros((B, 0, 4, N_KV, HEAD_DIM), dt), jnp.zeros((B, 0, 2, N_KV, HEAD_DIM), dt), *weights)
        kv_past = cache_kv[page_table].reshape(DB, n_pages * PAGE_SIZE, 4, N_KV, HEAD_DIM)
        y_s, kv_s, win_s, C_s, n_s, m_s, conv_s = hybrid_layer(
            y_s, state_conv, state_C, state_n, state_m, kv_past, state_win, *weights)
    return (y_p, y_s, kv_p, kv_s, win_p, win_s, C_p, C_s, n_p, n_s, m_p, m_s, conv_p, conv_s)
```

```python
import functools
import math

import jax
import jax.numpy as jnp
import numpy as np
from jax import lax
from jax.experimental import pallas as pl
from jax.experimental.pallas import tpu as pltpu

D_MODEL = 2048
M_HEADS = 4
M_DQK = 128
M_DV = 256
M_CONV = 4
M_CHUNK = 64
N_HEADS = 16
N_KV = 4
HEAD_DIM = 64
CMP_BLK = 32
SEL_BLK = 64
N_SELECT = 16
WINDOW = 512
NSA_QBLK = 64
FORCE_SCORE = 1e4
N_GROUPS = 4
EXPERTS_PER_GROUP = 8
N_EXPERTS = N_GROUPS * EXPERTS_PER_GROUP
TOP_K_FINE = 2
MOE_BLK = 128
EPS = 1e-6
M_QK = M_HEADS * M_DQK
M_VW = M_HEADS * M_DV
N_QW = N_HEADS * HEAD_DIM
N_KVW = 6 * N_KV * HEAD_DIM
SPLITS = (M_QK, M_QK, M_VW, M_HEADS, M_HEADS, M_VW, N_QW, N_KVW, 3 * N_HEADS, D_MODEL, D_MODEL)
D_IN = sum(SPLITS)

F32 = jnp.float32
BF16 = jnp.bfloat16
VMEM_LIMIT = 48 * 1024 * 1024


def _norm_proj_kernel(x_ref, g_ref, w_ref, o_ref):
    x = x_ref[...]
    r = lax.rsqrt(jnp.mean(x * x, axis=-1, keepdims=True) + EPS)
    hx = (x * r * g_ref[...]).astype(BF16)
    o_ref[...] = jnp.dot(hx, w_ref[...], preferred_element_type=F32)


def norm_proj(x, g, w_bf16, *, tm=512, tn=512):
    n, d = x.shape
    k = w_bf16.shape[1]
    return pl.pallas_call(
        _norm_proj_kernel,
        grid=(n // tm, k // tn),
        in_specs=[
            pl.BlockSpec((tm, d), lambda i, j: (i, 0)),
            pl.BlockSpec((1, d), lambda i, j: (0, 0)),
            pl.BlockSpec((d, tn), lambda i, j: (0, j)),
        ],
        out_specs=pl.BlockSpec((tm, tn), lambda i, j: (i, j)),
        out_shape=jax.ShapeDtypeStruct((n, k), F32),
        compiler_params=pltpu.CompilerParams(
            dimension_semantics=("parallel", "arbitrary"), vmem_limit_bytes=VMEM_LIMIT),
        name="norm_proj",
    )(x, g.reshape(1, d), w_bf16)


def rms_norm(x, g):
    xf = x.astype(F32)
    y = xf * lax.rsqrt(jnp.mean(xf * xf, axis=-1, keepdims=True) + EPS)
    return (y * g.astype(F32)).astype(x.dtype)


def masked_softmax(s, mask):
    s = jnp.where(mask, s.astype(F32), -jnp.inf)
    m = jnp.max(s, axis=-1, keepdims=True)
    m = jnp.where(jnp.isfinite(m), m, 0.0)
    e = jnp.where(mask, jnp.exp(s - m), 0.0)
    return e / jnp.maximum(jnp.sum(e, axis=-1, keepdims=True), 1e-30)


def short_conv(u, prev, w):
    T = u.shape[1]
    full = jnp.concatenate([prev.astype(u.dtype), u], axis=1)
    out = full[:, 0:T] * w[0]
    for j in range(1, M_CONV):
        out = out + full[:, j:j + T] * w[j]
    return out, full[:, T:]


def mlstm_chunkwise(q, k, v, i_pre, logf, C0, n0, m0):
    B, T = q.shape[:2]
    L = M_CHUNK if T % M_CHUNK == 0 else T
    nc = T // L

    def chunks(a):
        a = a.reshape((B, nc, L) + a.shape[2:])
        return jnp.moveaxis(jnp.moveaxis(a, 3, 2), 1, 0)

    causal = jnp.tril(jnp.ones((L, L), bool))

    def step(carry, xs):
        C, n, m = carry
        qc, kc, vc, ic, fc = xs
        b = jnp.cumsum(fc, axis=-1)
        logD = jnp.where(causal, b[..., :, None] - b[..., None, :] + ic[..., None, :], -jnp.inf)
        inter = b + m[..., None]
        m_t = jnp.maximum(inter, jnp.max(logD, axis=-1))
        a = jnp.exp(inter - m_t)
        S = jnp.einsum('bhtd,bhsd->bhts', qc, kc) * jnp.exp(logD - m_t[..., None])
        num = a[..., None] * jnp.einsum('bhtd,bhde->bhte', qc, C) + jnp.einsum('bhts,bhse->bhte', S, vc)
        den = a * jnp.einsum('bhtd,bhd->bht', qc, n) + jnp.sum(S, axis=-1)
        h = num / jnp.maximum(jnp.abs(den), jnp.exp(-m_t))[..., None]
        m_new = m_t[..., -1]
        w = jnp.exp(b[..., -1:] - b + ic - m_new[..., None])
        aL = jnp.exp(b[..., -1] + m - m_new)
        C_new = aL[..., None, None] * C + jnp.einsum('bhs,bhsd,bhse->bhde', w, kc, vc)
        n_new = aL[..., None] * n + jnp.einsum('bhs,bhsd->bhd', w, kc)
        return (C_new, n_new, m_new), h

    (C, n, m), h = lax.scan(step, (C0, n0, m0), (chunks(q), chunks(k), chunks(v), chunks(i_pre), chunks(logf)))
    h = jnp.moveaxis(jnp.moveaxis(h, 0, 1), 2, 3).reshape(B, T, M_HEADS, M_DV)
    return h, C, n, m


def compress(rows, pe, w1, w2):
    B, Lk = rows.shape[:2]
    nc = Lk // CMP_BLK
    blk = rows[:, :nc * CMP_BLK].reshape(B, nc, CMP_BLK, N_KV, HEAD_DIM) + pe[:, None, :]
    hid = jax.nn.silu(jnp.einsum('bclgd,ldf->bcgf', blk, w1))
    return jnp.einsum('bcgf,fd->bcgd', hid, w2)


def nsa_attention(q, gates, kv_all, kw_full, q0, g_kc, g_ks, g_kw, pe_ck, w_ck1, w_ck2, pe_cv, w_cv1, w_cv2):
    B, T = q.shape[:2]
    Lk = kv_all.shape[1]
    hpg = N_HEADS // N_KV
    scale = HEAD_DIM ** -0.5
    kc = rms_norm(compress(kv_all[:, :, 0], pe_ck, w_ck1, w_ck2), g_kc)
    vc = compress(kv_all[:, :, 1], pe_cv, w_cv1, w_cv2)
    nc = kc.shape[1]
    n_sel = -(-Lk // SEL_BLK)
    pad = n_sel * SEL_BLK - Lk
    ks = jnp.pad(rms_norm(kv_all[:, :, 2], g_ks), ((0, 0), (0, pad), (0, 0), (0, 0)))
    ks = ks.reshape(B, n_sel, SEL_BLK, N_KV, HEAD_DIM).transpose(0, 3, 1, 2, 4)
    vs = jnp.pad(kv_all[:, :, 3], ((0, 0), (0, pad), (0, 0), (0, 0)))
    vs = vs.reshape(B, n_sel, SEL_BLK, N_KV, HEAD_DIM).transpose(0, 3, 1, 2, 4)
    kw = rms_norm(kw_full[:, :, 0], g_kw)
    vw = kw_full[:, :, 1]
    top = min(N_SELECT, n_sel)
    ratio = SEL_BLK // CMP_BLK
    qb = NSA_QBLK if T % NSA_QBLK == 0 else T
    nqb = T // qb
    qg = q.reshape(B, nqb, qb, N_KV, hpg, HEAD_DIM).transpose(1, 0, 3, 4, 2, 5)
    gg = jnp.moveaxis(gates.reshape(B, nqb, qb, N_HEADS, 3), 1, 0)
    cmp_end = (jnp.arange(nc) + 1) * CMP_BLK - 1
    blk_ids = jnp.arange(n_sel)
    b_ix = jnp.arange(B)[:, None, None, None]
    g_ix = jnp.arange(N_KV)[None, :, None, None]

    def block(args):
        qi, gi, j = args
        s0 = j * qb
        t = q0 + s0 + jnp.arange(qb)
        sc = jnp.einsum('bghqd,bcgd->bghqc', qi, kc) * scale
        p_c = masked_softmax(sc, cmp_end[None, :] <= t[:, None])
        o_c = jnp.einsum('bghqc,bcgd->bghqd', p_c.astype(vc.dtype), vc)
        imp = jnp.sum(p_c, axis=2)
        imp = jnp.pad(imp, ((0, 0), (0, 0), (0, 0), (0, n_sel * ratio - nc)))
        imp = imp.reshape(B, N_KV, qb, n_sel, ratio).sum(-1)
        cur = t // SEL_BLK
        valid = blk_ids[None, :] <= cur[:, None]
        forced = (blk_ids[None, :] == 0) | (blk_ids[None, :] == cur[:, None]) | (blk_ids[None, :] == cur[:, None] - 1)
        imp = jnp.where(valid, jnp.where(forced, FORCE_SCORE, imp), -jnp.inf)
        top_v, top_i = lax.top_k(imp, top)
        k_sel = ks[b_ix, g_ix, top_i]
        v_sel = vs[b_ix, g_ix, top_i]
        pos = top_i[..., None] * SEL_BLK + jnp.arange(SEL_BLK)
        m_s = (pos <= t[:, None, None]) & jnp.isfinite(top_v)[..., None]
        ss = (jnp.einsum('bghqd,bgqnkd->bghqnk', qi, k_sel) * scale).reshape(B, N_KV, hpg, qb, top * SEL_BLK)
        p_s = masked_softmax(ss, m_s.reshape(B, N_KV, 1, qb, top * SEL_BLK))
        o_s = jnp.einsum('bghqnk,bgqnkd->bghqd', p_s.reshape(B, N_KV, hpg, qb, top, SEL_BLK).astype(v_sel.dtype), v_sel)
        kwi = lax.dynamic_slice_in_dim(kw, s0, qb + WINDOW, axis=1)
        vwi = lax.dynamic_slice_in_dim(vw, s0, qb + WINDOW, axis=1)
        pw = q0 - WINDOW + s0 + jnp.arange(qb + WINDOW)
        m_w = (pw[None, :] <= t[:, None]) & (pw[None, :] > t[:, None] - WINDOW) & (pw[None, :] >= 0)
        sw = jnp.einsum('bghqd,bkgd->bghqk', qi, kwi) * scale
        p_w = masked_softmax(sw, m_w)
        o_w = jnp.einsum('bghqk,bkgd->bghqd', p_w.astype(vwi.dtype), vwi)
        o = jnp.stack([o_c, o_s, o_w], axis=-1)
        gi_r = gi.reshape(B, qb, N_KV, hpg, 3).transpose(0, 2, 3, 1, 4)
        return jnp.einsum('bghqdc,bghqc->bqghd', o, gi_r.astype(o.dtype))

    out = lax.map(block, (qg, gg, jnp.arange(nqb)))
    return jnp.moveaxis(out, 0, 1).reshape(B, T, N_QW)


def hier_moe(x, w_group, b_group, w_expert, b_expert, w_e1, w_e3, w_e2):
    N, D = x.shape
    pg = jax.nn.softmax((x @ w_group + b_group).astype(F32), axis=-1)
    grp = jnp.argmax(pg, axis=-1)
    p_grp = jnp.max(pg, axis=-1)
    le = (x @ w_expert + b_expert).astype(F32).reshape(N, N_GROUPS, EXPERTS_PER_GROUP)
    pe = jax.nn.softmax(le[jnp.arange(N), grp], axis=-1)
    top_p, top_i = lax.top_k(pe, TOP_K_FINE)
    wts = p_grp[:, None] * top_p / jnp.sum(top_p, axis=-1, keepdims=True)
    eid = (grp[:, None] * EXPERTS_PER_GROUP + top_i).reshape(-1)
    tok = jnp.repeat(jnp.arange(N), TOP_K_FINE)
    wt = wts.reshape(-1)
    M = N * TOP_K_FINE
    order = jnp.argsort(eid)
    e_s, tok_s, wt_s = eid[order], tok[order], wt[order]
    counts = jnp.zeros((N_EXPERTS,), jnp.int32).at[eid].add(1)
    padded = (counts + MOE_BLK - 1) // MOE_BLK * MOE_BLK
    p_end = jnp.cumsum(padded)
    p_start = p_end - padded
    c_start = jnp.cumsum(counts) - counts
    dest = p_start[e_s] + jnp.arange(M) - c_start[e_s]
    n_blk = (M + N_EXPERTS * (MOE_BLK - 1) + MOE_BLK - 1) // MOE_BLK
    rows = jnp.zeros((n_blk * MOE_BLK, D), x.dtype).at[dest].set(x[tok_s])
    blk_e = jnp.minimum(jnp.searchsorted(p_end, jnp.arange(n_blk) * MOE_BLK, side='right'), N_EXPERTS - 1)

    def expert_block(args):
        xb, e = args
        return (jax.nn.silu(xb @ w_e1[e]) * (xb @ w_e3[e])) @ w_e2[e]

    out = lax.map(expert_block, (rows.reshape(n_blk, MOE_BLK, D), blk_e)).reshape(-1, D)
    return jax.ops.segment_sum(out[dest] * wt_s[:, None].astype(x.dtype), tok_s, num_segments=N)


def hybrid_layer(x, conv_prev, C0, n0, m0, kv_past, win_past, g_mix, w_in, w_conv, b_i, b_f, g_mh, g_q, g_kc, g_ks, g_kw,
                 pe_ck, w_ck1, w_ck2, pe_cv, w_cv1, w_cv2, w_proj_m, w_proj_n, w_out, g_ffn, w_group, b_group,
                 w_expert, b_expert, w_e1, w_e3, w_e2, *, use_pallas_proj):
    B, T, D = x.shape
    P = kv_past.shape[1]
    WB = win_past.shape[1]
    split_at = np.cumsum(SPLITS)[:-1].tolist()
    if use_pallas_proj:
        d_pad = -(-D_IN // 512) * 512
        w_pad = jnp.pad(w_in, ((0, 0), (0, d_pad - D_IN))).astype(BF16)
        proj = norm_proj(x.reshape(B * T, D), g_mix, w_pad)[:, :D_IN].reshape(B, T, D_IN)
    else:
        proj = rms_norm(x, g_mix) @ w_in
    mq, mk, mv, mi, mf, mo, nq, nkv, ng, gate_m, gate_n = jnp.split(proj, split_at, axis=-1)
    qk, conv_state = short_conv(jnp.concatenate([mq, mk], axis=-1), conv_prev, w_conv)
    qk = jax.nn.silu(qk)
    q_m = qk[..., :M_QK].reshape(B, T, M_HEADS, M_DQK).astype(F32)
    k_m = qk[..., M_QK:].reshape(B, T, M_HEADS, M_DQK).astype(F32) * (M_DQK ** -0.5)
    v_m = mv.reshape(B, T, M_HEADS, M_DV).astype(F32)
    i_pre = (mi + b_i).astype(F32)
    logf = jax.nn.log_sigmoid((mf + b_f).astype(F32))
    h_m, C1, n1, m1 = mlstm_chunkwise(q_m, k_m, v_m, i_pre, logf, C0.astype(F32), n0.astype(F32), m0.astype(F32))
    h_m = rms_norm(h_m, g_mh).reshape(B, T, M_VW).astype(x.dtype) * jax.nn.sigmoid(mo)
    qn = rms_norm(nq.reshape(B, T, N_HEADS, HEAD_DIM), g_q)
    kv_new = nkv.reshape(B, T, 6, N_KV, HEAD_DIM)
    kv_rows = kv_new[:, :, :4]
    kv_all = jnp.concatenate([kv_past.astype(x.dtype), kv_rows], axis=1)
    win_all = jnp.concatenate([win_past.astype(x.dtype), kv_new[:, :, 4:]], axis=1)
    kw_full = jnp.pad(win_all, ((0, 0), (WINDOW - WB, 0), (0, 0), (0, 0), (0, 0)))
    gates_n = jax.nn.sigmoid(ng).reshape(B, T, N_HEADS, 3)
    h_n = nsa_attention(qn, gates_n, kv_all, kw_full, P, g_kc, g_ks, g_kw, pe_ck, w_ck1, w_ck2, pe_cv, w_cv1, w_cv2)
    win_state = win_all[:, -min(WINDOW, WB + T):]
    u = jax.nn.sigmoid(gate_m) * (h_m @ w_proj_m) + jax.nn.sigmoid(gate_n) * (h_n.astype(x.dtype) @ w_proj_n)
    x1 = x + u @ w_out
    h2 = rms_norm(x1, g_ffn).reshape(B * T, D)
    y = x1 + hier_moe(h2, w_group, b_group, w_expert, b_expert, w_e1, w_e3, w_e2).reshape(B, T, D)
    return y, kv_rows, win_state, C1, n1, m1, conv_state


def kernel(x_prompt, x_sample, cache_kv, page_table, state_win, state_C, state_n, state_m, state_conv, g_mix, w_in, w_conv, b_i, b_f, g_mh, g_q, g_kc, g_ks, g_kw, pe_ck, w_ck1, w_ck2, pe_cv, w_cv1, w_cv2, w_proj_m, w_proj_n, w_out, g_ffn, w_group, b_group, w_expert, b_expert, w_e1, w_e3, w_e2):
    B = x_prompt.shape[0]
    DB = x_sample.shape[0]
    n_pages = page_table.shape[1]
    dt = x_prompt.dtype
    weights = (g_mix, w_in, w_conv, b_i, b_f, g_mh, g_q, g_kc, g_ks, g_kw, pe_ck, w_ck1, w_ck2, pe_cv, w_cv1, w_cv2,
               w_proj_m, w_proj_n, w_out, g_ffn, w_group, b_group, w_expert, b_expert, w_e1, w_e3, w_e2)
    y_p, kv_p, win_p, C_p, n_p, m_p, conv_p = hybrid_layer(
        x_prompt, jnp.zeros((B, M_CONV - 1, 2 * M_QK), dt), jnp.zeros((B, M_HEADS, M_DQK, M_DV), F32),
        jnp.zeros((B, M_HEADS, M_DQK), F32), jnp.zeros((B, M_HEADS), F32),
        jnp.zeros((B, 0, 4, N_KV, HEAD_DIM), dt), jnp.zeros((B, 0, 2, N_KV, HEAD_DIM), dt), *weights,
        use_pallas_proj=True)
    with jax.default_matmul_precision("highest"):
        kv_past = cache_kv[page_table].reshape(DB, n_pages * cache_kv.shape[1], 4, N_KV, HEAD_DIM)
        y_s, kv_s, win_s, C_s, n_s, m_s, conv_s = hybrid_layer(
            x_sample, state_conv, state_C, state_n, state_m, kv_past, state_win, *weights,
            use_pallas_proj=False)
    return (y_p, y_s, kv_p, kv_s, win_p, win_s, C_p, C_s, n_p, n_s, m_p, m_s, conv_p, conv_s)
```

```python
import functools
import math

import jax
import jax.numpy as jnp
import numpy as np
from jax import lax
from jax.experimental import pallas as pl
from jax.experimental.pallas import tpu as pltpu

D_MODEL = 2048
M_HEADS = 4
M_DQK = 128
M_DV = 256
M_CONV = 4
M_CHUNK = 64
N_HEADS = 16
N_KV = 4
HEAD_DIM = 64
CMP_BLK = 32
SEL_BLK = 64
N_SELECT = 16
WINDOW = 512
NSA_QBLK = 64
FORCE_SCORE = 1e4
N_GROUPS = 4
EXPERTS_PER_GROUP = 8
N_EXPERTS = N_GROUPS * EXPERTS_PER_GROUP
TOP_K_FINE = 2
MOE_BLK = 128
EPS = 1e-6
M_QK = M_HEADS * M_DQK
M_VW = M_HEADS * M_DV
N_QW = N_HEADS * HEAD_DIM
N_KVW = 6 * N_KV * HEAD_DIM
SPLITS = (M_QK, M_QK, M_VW, M_HEADS, M_HEADS, M_VW, N_QW, N_KVW, 3 * N_HEADS, D_MODEL, D_MODEL)
D_IN = sum(SPLITS)

F32 = jnp.float32
BF16 = jnp.bfloat16
VMEM_LIMIT = 48 * 1024 * 1024


def _mxu(a, b):
    return jnp.dot(a, b, preferred_element_type=F32, precision=lax.Precision.DEFAULT)


def _norm_proj_kernel(x_ref, g_ref, w_ref, o_ref, hx_ref):
    @pl.when(pl.program_id(1) == 0)
    def _():
        x = x_ref[...]
        r = lax.rsqrt(jnp.mean(x * x, axis=-1, keepdims=True) + EPS)
        hx_ref[...] = (x * r * g_ref[...]).astype(BF16)

    o_ref[...] = _mxu(hx_ref[...], w_ref[...])


def norm_proj(x, g, w_bf16, *, tm=1024, tn=896):
    n, d = x.shape
    k = w_bf16.shape[1]
    return pl.pallas_call(
        _norm_proj_kernel,
        grid=(n // tm, k // tn),
        in_specs=[
            pl.BlockSpec((tm, d), lambda i, j: (i, 0)),
            pl.BlockSpec((1, d), lambda i, j: (0, 0)),
            pl.BlockSpec((d, tn), lambda i, j: (0, j)),
        ],
        out_specs=pl.BlockSpec((tm, tn), lambda i, j: (i, j)),
        out_shape=jax.ShapeDtypeStruct((n, k), F32),
        scratch_shapes=[pltpu.VMEM((tm, d), BF16)],
        compiler_params=pltpu.CompilerParams(
            dimension_semantics=("parallel", "arbitrary"), vmem_limit_bytes=VMEM_LIMIT),
        name="norm_proj",
    )(x, g.reshape(1, d), w_bf16)


def _rms_rows_kernel(x_ref, g_ref, o_ref):
    x = x_ref[...]
    r = lax.rsqrt(jnp.mean(x * x, axis=-1, keepdims=True) + EPS)
    o_ref[...] = (x * r * g_ref[...]).astype(o_ref.dtype)


def rms_rows(x, g, out_dtype, *, tr=2048):
    rows, c = x.shape
    tr = min(tr, rows)
    return pl.pallas_call(
        _rms_rows_kernel,
        grid=(rows // tr,),
        in_specs=[pl.BlockSpec((tr, c), lambda i: (i, 0)), pl.BlockSpec((1, c), lambda i: (0, 0))],
        out_specs=pl.BlockSpec((tr, c), lambda i: (i, 0)),
        out_shape=jax.ShapeDtypeStruct((rows, c), out_dtype),
        compiler_params=pltpu.CompilerParams(dimension_semantics=("parallel",)),
        name="rms_rows",
    )(x, g.reshape(1, c))


def _split_bf16(x):
    hi = lax.bitcast_convert_type(
        lax.bitcast_convert_type(x, jnp.int32) & jnp.int32(-65536), F32)
    return hi.astype(BF16), (x - hi).astype(BF16)


def _dot_split(a, b_hi, b_lo):
    a_hi, a_lo = _split_bf16(a)
    return _mxu(a_hi, b_hi) + (_mxu(a_lo, b_hi) + _mxu(a_hi, b_lo))


def _compress_kernel(x_ref, pe_ref, w1_ref, w1l_ref, w2_ref, w2l_ref, g_ref, o_ref, *, normalize, precise):
    x = x_ref[...] + pe_ref[...]
    if precise:
        h = _dot_split(x, w1_ref[...], w1l_ref[...])
    else:
        h = _mxu(x.astype(BF16), w1_ref[...])
    h = h * jax.nn.sigmoid(h)
    if precise:
        y = _dot_split(h, w2_ref[...], w2l_ref[...])
    else:
        y = _mxu(h.astype(BF16), w2_ref[...])
    if normalize:
        y = y * lax.rsqrt(jnp.mean(y * y, axis=-1, keepdims=True) + EPS) * g_ref[...]
    o_ref[...] = y.astype(o_ref.dtype)


def compress_rows(x, pe, w1, w2, g, *, normalize, precise=False, tr=512):
    rows, k = x.shape
    f = w1.shape[-1]
    d = w2.shape[1]
    tr = min(tr, rows)
    w1_hi, w1_lo = _split_bf16(w1.reshape(k, f))
    w2_hi, w2_lo = _split_bf16(w2)
    return pl.pallas_call(
        functools.partial(_compress_kernel, normalize=normalize, precise=precise),
        grid=(rows // tr,),
        in_specs=[
            pl.BlockSpec((tr, k), lambda i: (i, 0)),
            pl.BlockSpec((1, k), lambda i: (0, 0)),
            pl.BlockSpec((k, f), lambda i: (0, 0)),
            pl.BlockSpec((k, f), lambda i: (0, 0)),
            pl.BlockSpec((f, d), lambda i: (0, 0)),
            pl.BlockSpec((f, d), lambda i: (0, 0)),
            pl.BlockSpec((1, d), lambda i: (0, 0)),
        ],
        out_specs=pl.BlockSpec((tr, d), lambda i: (i, 0)),
        out_shape=jax.ShapeDtypeStruct((rows, d), F32 if precise else BF16),
        compiler_params=pltpu.CompilerParams(dimension_semantics=("parallel",)),
        name="nsa_compress",
    )(x, pe.reshape(1, k), w1_hi, w1_lo, w2_hi, w2_lo, g.reshape(1, d))


NSA_TQ = 128
NSA_TK = 256
HPG = N_HEADS // N_KV
NSA_Q = HPG * NSA_TQ
NEG_BIG = -1e30


def _nsa_prompt_kernel(qT_ref, gq_ref, gate_ref, kc_ref, vcT_ref, ks_ref, vsT_ref, kw_ref, vwT_ref,
                       o_ref, sc_ref, sel_ref, imp_ref, *, n_cmp, n_sel):
    j = pl.program_id(2)
    s0 = j * NSA_TQ
    q = qT_ref[0, 0, 0]
    r = lax.rsqrt(jnp.mean(q * q, axis=0, keepdims=True) + EPS)
    qn = (q * r * gq_ref[...] * (HEAD_DIM ** -0.5)).astype(BF16)
    lane = lax.broadcasted_iota(jnp.int32, (1, NSA_Q), 1)
    t_row = s0 + (lane & (NSA_TQ - 1))

    s = _mxu(kc_ref[0, 0], qn)
    c_end = lax.broadcasted_iota(jnp.int32, (n_cmp, NSA_Q), 0) * CMP_BLK + (CMP_BLK - 1)
    mask_c = c_end <= t_row
    s = jnp.where(mask_c, s, -jnp.inf)
    m = jnp.max(s, axis=0, keepdims=True)
    m = jnp.where(m > -jnp.inf, m, 0.0)
    e = jnp.where(mask_c, jnp.exp(s - m), 0.0)
    p_c = e / jnp.maximum(jnp.sum(e, axis=0, keepdims=True), 1e-30)
    o_c = _mxu(vcT_ref[0, 0], p_c.astype(BF16))

    imp_ref[...] = (p_c[:, 0:NSA_TQ] + p_c[:, NSA_TQ:2 * NSA_TQ]
                    + p_c[:, 2 * NSA_TQ:3 * NSA_TQ] + p_c[:, 3 * NSA_TQ:4 * NSA_TQ])
    imp = imp_ref[pl.ds(0, n_sel, stride=2), :] + imp_ref[pl.ds(1, n_sel, stride=2), :]
    n_iota = lax.broadcasted_iota(jnp.int32, (n_sel, NSA_TQ), 0)
    cur = (s0 + lax.broadcasted_iota(jnp.int32, (n_sel, NSA_TQ), 1)) // SEL_BLK
    valid = n_iota <= cur
    forced = (n_iota == 0) | (n_iota == cur) | (n_iota == cur - 1)
    score = jnp.where(valid, jnp.where(forced, FORCE_SCORE, imp), -jnp.inf)
    sc_ref[...] = score

    def rank_body(n2, rank):
        row = jnp.broadcast_to(sc_ref[pl.ds(n2, 1), :], (n_sel, NSA_TQ))
        beats = (row > score) | ((row == score) & (n2 < n_iota))
        return rank + jnp.where(beats, 1.0, 0.0)

    rank = lax.fori_loop(0, n_sel, rank_body, jnp.zeros((n_sel, NSA_TQ), F32))
    sel_ref[...] = jnp.where(valid & (rank < N_SELECT), 1.0, 0.0)

    key_iota = lax.broadcasted_iota(jnp.int32, (NSA_TK, NSA_Q), 0)

    def attend(k_ref, vT_ref, c_lo, c_hi, mask_fn):
        def body(c, carry):
            m_i, l_i, acc = carry
            k = k_ref[0, 0, pl.ds(pl.multiple_of(c * NSA_TK, NSA_TK), NSA_TK), :]
            sk = _mxu(k, qn)
            mask = mask_fn(c)
            sk = jnp.where(mask, sk, NEG_BIG)
            m_new = jnp.maximum(m_i, jnp.max(sk, axis=0, keepdims=True))
            alpha = jnp.exp(m_i - m_new)
            p = jnp.where(mask, jnp.exp(sk - m_new), 0.0)
            l_new = alpha * l_i + jnp.sum(p, axis=0, keepdims=True)
            acc = alpha * acc + _mxu(vT_ref[0, 0, c], p.astype(BF16))
            return m_new, l_new, acc

        init = (jnp.full((1, NSA_Q), NEG_BIG, F32), jnp.zeros((1, NSA_Q), F32),
                jnp.zeros((HEAD_DIM, NSA_Q), F32))
        _, l_f, acc_f = lax.fori_loop(c_lo, c_hi, body, init)
        return acc_f / jnp.maximum(l_f, 1e-30)

    def sel_mask(c):
        pos = c * NSA_TK + key_iota
        per_blk = [jnp.broadcast_to(sel_ref[pl.ds(c * (NSA_TK // SEL_BLK) + i, 1), :], (SEL_BLK, NSA_TQ))
                   for i in range(NSA_TK // SEL_BLK)]
        sm = jnp.concatenate(per_blk, axis=0)
        sm = jnp.concatenate([sm] * HPG, axis=1)
        return (sm > 0.5) & (pos <= t_row)

    def win_mask(c):
        pos = c * NSA_TK + key_iota
        return (pos <= t_row) & (pos > t_row - WINDOW)

    c_hi = (s0 + NSA_TQ + NSA_TK - 1) // NSA_TK
    o_s = attend(ks_ref, vsT_ref, 0, c_hi, sel_mask)
    o_w = attend(kw_ref, vwT_ref, jnp.maximum(s0 - (WINDOW - 1), 0) // NSA_TK, c_hi, win_mask)

    g = jax.nn.sigmoid(gate_ref[0, 0, 0])
    o_ref[0, 0, 0] = g[0:1] * o_c + g[1:2] * o_s + g[2:3] * o_w


def nsa_prompt(nq, ng, kv_new, g_q, g_kc, g_ks, g_kw, pe_ck, w_ck1, w_ck2, pe_cv, w_cv1, w_cv2):
    B, T = nq.shape[:2]
    G = N_KV
    nqb = T // NSA_TQ
    n_cmp = T // CMP_BLK
    n_sel = T // SEL_BLK
    nch = T // NSA_TK
    qT = nq.reshape(B, nqb, NSA_TQ, G, HPG, HEAD_DIM).transpose(0, 3, 1, 5, 4, 2).reshape(B, G, nqb, HEAD_DIM, NSA_Q)
    gT = ng.reshape(B, nqb, NSA_TQ, G, HPG, 3).transpose(0, 3, 1, 5, 4, 2).reshape(B, G, nqb, 3, NSA_Q)
    kv_g = kv_new.transpose(2, 0, 3, 1, 4)
    xk = kv_g[0].reshape(B * G * n_cmp, CMP_BLK * HEAD_DIM)
    xv = kv_g[1].reshape(B * G * n_cmp, CMP_BLK * HEAD_DIM)
    kc = compress_rows(xk, pe_ck, w_ck1, w_ck2, g_kc, normalize=True).reshape(B, G, n_cmp, HEAD_DIM)
    vc = compress_rows(xv, pe_cv, w_cv1, w_cv2, g_kc, normalize=False).reshape(B, G, n_cmp, HEAD_DIM)
    vcT = vc.transpose(0, 1, 3, 2)
    ks = rms_rows(kv_g[2].reshape(B * G * T, HEAD_DIM), g_ks, BF16).reshape(B, G, T, HEAD_DIM)
    kw = rms_rows(kv_g[4].reshape(B * G * T, HEAD_DIM), g_kw, BF16).reshape(B, G, T, HEAD_DIM)
    vsT = kv_g[3].astype(BF16).reshape(B, G, nch, NSA_TK, HEAD_DIM).transpose(0, 1, 2, 4, 3)
    vwT = kv_g[5].astype(BF16).reshape(B, G, nch, NSA_TK, HEAD_DIM).transpose(0, 1, 2, 4, 3)

    bg = lambda b, g, j: (b, g, 0, 0)
    bg5 = lambda b, g, j: (b, g, 0, 0, 0)
    bgj = lambda b, g, j: (b, g, j, 0, 0)
    oT = pl.pallas_call(
        functools.partial(_nsa_prompt_kernel, n_cmp=n_cmp, n_sel=n_sel),
        grid=(B, G, nqb),
        in_specs=[
            pl.BlockSpec((1, 1, 1, HEAD_DIM, NSA_Q), bgj),
            pl.BlockSpec((HEAD_DIM, 1), lambda b, g, j: (0, 0)),
            pl.BlockSpec((1, 1, 1, 3, NSA_Q), bgj),
            pl.BlockSpec((1, 1, n_cmp, HEAD_DIM), bg),
            pl.BlockSpec((1, 1, HEAD_DIM, n_cmp), bg),
            pl.BlockSpec((1, 1, T, HEAD_DIM), bg),
            pl.BlockSpec((1, 1, nch, HEAD_DIM, NSA_TK), bg5),
            pl.BlockSpec((1, 1, T, HEAD_DIM), bg),
            pl.BlockSpec((1, 1, nch, HEAD_DIM, NSA_TK), bg5),
        ],
        out_specs=pl.BlockSpec((1, 1, 1, HEAD_DIM, NSA_Q), bgj),
        out_shape=jax.ShapeDtypeStruct((B, G, nqb, HEAD_DIM, NSA_Q), F32),
        scratch_shapes=[pltpu.VMEM((n_sel, NSA_TQ), F32), pltpu.VMEM((n_sel, NSA_TQ), F32),
                        pltpu.VMEM((n_cmp, NSA_TQ), F32)],
        compiler_params=pltpu.CompilerParams(
            dimension_semantics=("parallel", "parallel", "arbitrary"), vmem_limit_bytes=VMEM_LIMIT),
        name="nsa_prompt",
    )(qT, g_q.reshape(HEAD_DIM, 1), gT, kc, vcT, ks, vsT, kw, vwT)
    return oT.reshape(B, G, nqb, HEAD_DIM, HPG, NSA_TQ).transpose(0, 2, 5, 1, 4, 3).reshape(B, T, N_QW)


def rms_norm(x, g):
    xf = x.astype(F32)
    y = xf * lax.rsqrt(jnp.mean(xf * xf, axis=-1, keepdims=True) + EPS)
    return (y * g.astype(F32)).astype(x.dtype)


def masked_softmax(s, mask):
    s = jnp.where(mask, s.astype(F32), -jnp.inf)
    m = jnp.max(s, axis=-1, keepdims=True)
    m = jnp.where(jnp.isfinite(m), m, 0.0)
    e = jnp.where(mask, jnp.exp(s - m), 0.0)
    return e / jnp.maximum(jnp.sum(e, axis=-1, keepdims=True), 1e-30)


def short_conv(u, prev, w):
    T = u.shape[1]
    full = jnp.concatenate([prev.astype(u.dtype), u], axis=1)
    out = full[:, 0:T] * w[0]
    for j in range(1, M_CONV):
        out = out + full[:, j:j + T] * w[j]
    return out, full[:, T:]


def mlstm_chunkwise(q, k, v, i_pre, logf, C0, n0, m0):
    B, T = q.shape[:2]
    L = M_CHUNK if T % M_CHUNK == 0 else T
    nc = T // L

    def chunks(a):
        a = a.reshape((B, nc, L) + a.shape[2:])
        return jnp.moveaxis(jnp.moveaxis(a, 3, 2), 1, 0)

    causal = jnp.tril(jnp.ones((L, L), bool))

    def step(carry, xs):
        C, n, m = carry
        qc, kc, vc, ic, fc = xs
        b = jnp.cumsum(fc, axis=-1)
        logD = jnp.where(causal, b[..., :, None] - b[..., None, :] + ic[..., None, :], -jnp.inf)
        inter = b + m[..., None]
        m_t = jnp.maximum(inter, jnp.max(logD, axis=-1))
        a = jnp.exp(inter - m_t)
        S = jnp.einsum('bhtd,bhsd->bhts', qc, kc) * jnp.exp(logD - m_t[..., None])
        num = a[..., None] * jnp.einsum('bhtd,bhde->bhte', qc, C) + jnp.einsum('bhts,bhse->bhte', S, vc)
        den = a * jnp.einsum('bhtd,bhd->bht', qc, n) + jnp.sum(S, axis=-1)
        h = num / jnp.maximum(jnp.abs(den), jnp.exp(-m_t))[..., None]
        m_new = m_t[..., -1]
        w = jnp.exp(b[..., -1:] - b + ic - m_new[..., None])
        aL = jnp.exp(b[..., -1] + m - m_new)
        C_new = aL[..., None, None] * C + jnp.einsum('bhs,bhsd,bhse->bhde', w, kc, vc)
        n_new = aL[..., None] * n + jnp.einsum('bhs,bhsd->bhd', w, kc)
        return (C_new, n_new, m_new), h

    (C, n, m), h = lax.scan(step, (C0, n0, m0), (chunks(q), chunks(k), chunks(v), chunks(i_pre), chunks(logf)))
    h = jnp.moveaxis(jnp.moveaxis(h, 0, 1), 2, 3).reshape(B, T, M_HEADS, M_DV)
    return h, C, n, m


def compress(rows, pe, w1, w2):
    B, Lk = rows.shape[:2]
    nc = Lk // CMP_BLK
    blk = rows[:, :nc * CMP_BLK].reshape(B, nc, CMP_BLK, N_KV, HEAD_DIM) + pe[:, None, :]
    hid = jax.nn.silu(jnp.einsum('bclgd,ldf->bcgf', blk, w1))
    return jnp.einsum('bcgf,fd->bcgd', hid, w2)


def nsa_attention(q, gates, kv_all, kw_full, q0, g_kc, g_ks, g_kw, pe_ck, w_ck1, w_ck2, pe_cv, w_cv1, w_cv2):
    B, T = q.shape[:2]
    Lk = kv_all.shape[1]
    hpg = N_HEADS // N_KV
    scale = HEAD_DIM ** -0.5
    kc = rms_norm(compress(kv_all[:, :, 0], pe_ck, w_ck1, w_ck2), g_kc)
    vc = compress(kv_all[:, :, 1], pe_cv, w_cv1, w_cv2)
    nc = kc.shape[1]
    n_sel = -(-Lk // SEL_BLK)
    pad = n_sel * SEL_BLK - Lk
    ks = jnp.pad(rms_norm(kv_all[:, :, 2], g_ks), ((0, 0), (0, pad), (0, 0), (0, 0)))
    ks = ks.reshape(B, n_sel, SEL_BLK, N_KV, HEAD_DIM).transpose(0, 3, 1, 2, 4)
    vs = jnp.pad(kv_all[:, :, 3], ((0, 0), (0, pad), (0, 0), (0, 0)))
    vs = vs.reshape(B, n_sel, SEL_BLK, N_KV, HEAD_DIM).transpose(0, 3, 1, 2, 4)
    kw = rms_norm(kw_full[:, :, 0], g_kw)
    vw = kw_full[:, :, 1]
    top = min(N_SELECT, n_sel)
    ratio = SEL_BLK // CMP_BLK
    qb = NSA_QBLK if T % NSA_QBLK == 0 else T
    nqb = T // qb
    qg = q.reshape(B, nqb, qb, N_KV, hpg, HEAD_DIM).transpose(1, 0, 3, 4, 2, 5)
    gg = jnp.moveaxis(gates.reshape(B, nqb, qb, N_HEADS, 3), 1, 0)
    cmp_end = (jnp.arange(nc) + 1) * CMP_BLK - 1
    blk_ids = jnp.arange(n_sel)
    b_ix = jnp.arange(B)[:, None, None, None]
    g_ix = jnp.arange(N_KV)[None, :, None, None]

    def block(args):
        qi, gi, j = args
        s0 = j * qb
        t = q0 + s0 + jnp.arange(qb)
        sc = jnp.einsum('bghqd,bcgd->bghqc', qi, kc) * scale
        p_c = masked_softmax(sc, cmp_end[None, :] <= t[:, None])
        o_c = jnp.einsum('bghqc,bcgd->bghqd', p_c.astype(vc.dtype), vc)
        imp = jnp.sum(p_c, axis=2)
        imp = jnp.pad(imp, ((0, 0), (0, 0), (0, 0), (0, n_sel * ratio - nc)))
        imp = imp.reshape(B, N_KV, qb, n_sel, ratio).sum(-1)
        cur = t // SEL_BLK
        valid = blk_ids[None, :] <= cur[:, None]
        forced = (blk_ids[None, :] == 0) | (blk_ids[None, :] == cur[:, None]) | (blk_ids[None, :] == cur[:, None] - 1)
        imp = jnp.where(valid, jnp.where(forced, FORCE_SCORE, imp), -jnp.inf)
        top_v, top_i = lax.top_k(imp, top)
        k_sel = ks[b_ix, g_ix, top_i]
        v_sel = vs[b_ix, g_ix, top_i]
        pos = top_i[..., None] * SEL_BLK + jnp.arange(SEL_BLK)
        m_s = (pos <= t[:, None, None]) & jnp.isfinite(top_v)[..., None]
        ss = (jnp.einsum('bghqd,bgqnkd->bghqnk', qi, k_sel) * scale).reshape(B, N_KV, hpg, qb, top * SEL_BLK)
        p_s = masked_softmax(ss, m_s.reshape(B, N_KV, 1, qb, top * SEL_BLK))
        o_s = jnp.einsum('bghqnk,bgqnkd->bghqd', p_s.reshape(B, N_KV, hpg, qb, top, SEL_BLK).astype(v_sel.dtype), v_sel)
        kwi = lax.dynamic_slice_in_dim(kw, s0, qb + WINDOW, axis=1)
        vwi = lax.dynamic_slice_in_dim(vw, s0, qb + WINDOW, axis=1)
        pw = q0 - WINDOW + s0 + jnp.arange(qb + WINDOW)
        m_w = (pw[None, :] <= t[:, None]) & (pw[None, :] > t[:, None] - WINDOW) & (pw[None, :] >= 0)
        sw = jnp.einsum('bghqd,bkgd->bghqk', qi, kwi) * scale
        p_w = masked_softmax(sw, m_w)
        o_w = jnp.einsum('bghqk,bkgd->bghqd', p_w.astype(vwi.dtype), vwi)
        o = jnp.stack([o_c, o_s, o_w], axis=-1)
        gi_r = gi.reshape(B, qb, N_KV, hpg, 3).transpose(0, 2, 3, 1, 4)
        return jnp.einsum('bghqdc,bghqc->bqghd', o, gi_r.astype(o.dtype))

    out = lax.map(block, (qg, gg, jnp.arange(nqb)))
    return jnp.moveaxis(out, 0, 1).reshape(B, T, N_QW)


def nsa_sample(q, gates, cache_kv, page_table, kv_rows, win_all, g_kc, g_ks, g_kw,
               pe_ck, w_ck1, w_ck2, pe_cv, w_cv1, w_cv2):
    B, T = q.shape[:2]
    n_pages = page_table.shape[1]
    page = cache_kv.shape[1]
    P = n_pages * page
    G, hd = N_KV, HEAD_DIM
    assert T < CMP_BLK and T <= SEL_BLK and P % SEL_BLK == 0 and page % SEL_BLK == 0
    assert win_all.shape[1] == WINDOW + T
    nc = P // CMP_BLK
    cpp = page // CMP_BLK
    n_last = P // SEL_BLK
    n_sel = n_last + 1
    top = min(N_SELECT, n_sel)
    ratio = SEL_BLK // CMP_BLK
    scale = hd ** -0.5
    t = P + jnp.arange(T)
    qg = q.reshape(B, T, G, HPG, hd)

    flat_pages = page_table.reshape(-1)
    cmp_pages = jax.vmap(lambda p: lax.dynamic_slice(cache_kv, (p, 0, 0, 0, 0), (1, page, 2, G, hd))[0])(flat_pages)
    cmp_pages = cmp_pages.reshape(B, n_pages, cpp, CMP_BLK, 2, G, hd).transpose(4, 0, 5, 1, 2, 3, 6)
    cmp_pages = cmp_pages.reshape(2, B * G * nc, CMP_BLK * hd)
    kc = compress_rows(cmp_pages[0], pe_ck, w_ck1, w_ck2, g_kc, normalize=True, precise=True).reshape(B, G, nc, hd)
    vc = compress_rows(cmp_pages[1], pe_cv, w_cv1, w_cv2, g_kc, normalize=False, precise=True).reshape(B, G, nc, hd)
    cmp_end = (jnp.arange(nc) + 1) * CMP_BLK - 1
    sc = jnp.einsum('btghd,bgcd->bghtc', qg, kc) * scale
    p_c = masked_softmax(sc, cmp_end[None, :] <= t[:, None])
    o_c = jnp.einsum('bghtc,bgcd->bghtd', p_c, vc)

    imp = jnp.sum(p_c, axis=2)
    imp = jnp.pad(imp, ((0, 0), (0, 0), (0, 0), (0, n_sel * ratio - nc)))
    imp = imp.reshape(B, G, T, n_sel, ratio).sum(-1)
    blk_ids = jnp.arange(n_sel)
    cur = t // SEL_BLK
    valid = blk_ids[None, :] <= cur[:, None]
    forced = (blk_ids[None, :] == 0) | (blk_ids[None, :] == cur[:, None]) | (blk_ids[None, :] == cur[:, None] - 1)
    imp = jnp.where(valid, jnp.where(forced, FORCE_SCORE, imp), -jnp.inf)
    top_v, top_i = lax.top_k(imp, top)

    blk_c = jnp.minimum(top_i, n_last - 1)
    b_ix = jnp.arange(B)[:, None, None, None]
    phys = page_table[b_ix, blk_c // (page // SEL_BLK)]
    tok0 = (blk_c % (page // SEL_BLK)) * SEL_BLK
    g_ix = jnp.broadcast_to(jnp.arange(G)[None, :, None, None], top_i.shape)

    def take_block(p, r0, g):
        return lax.dynamic_slice(cache_kv, (p, r0, 2, g, 0), (1, SEL_BLK, 2, 1, hd))[0, :, :, 0]

    blocks = jax.vmap(take_block)(phys.reshape(-1), tok0.reshape(-1), g_ix.reshape(-1))
    blocks = blocks.reshape(B, G, T, top, SEL_BLK, 2, hd)
    new_blk = jnp.pad(kv_rows[:, :, 2:4], ((0, 0), (0, SEL_BLK - T), (0, 0), (0, 0), (0, 0)))
    new_blk = new_blk.transpose(0, 3, 1, 2, 4)[:, :, None, None]
    blocks = jnp.where((top_i == n_last)[..., None, None, None], new_blk, blocks)
    k_sel = rms_norm(blocks[..., 0, :], g_ks)
    v_sel = blocks[..., 1, :]
    pos = top_i[..., None] * SEL_BLK + jnp.arange(SEL_BLK)
    m_s = (pos <= t[:, None, None]) & jnp.isfinite(top_v)[..., None]
    ss = (jnp.einsum('btghd,bgtnkd->bghtnk', qg, k_sel) * scale).reshape(B, G, HPG, T, top * SEL_BLK)
    p_s = masked_softmax(ss, m_s.reshape(B, G, 1, T, top * SEL_BLK))
    o_s = jnp.einsum('bghtnk,bgtnkd->bghtd', p_s.reshape(B, G, HPG, T, top, SEL_BLK), v_sel)

    kw = rms_norm(win_all[:, :, 0], g_kw)
    vw = win_all[:, :, 1]
    pw = P - WINDOW + jnp.arange(T + WINDOW)
    m_w = (pw[None, :] <= t[:, None]) & (pw[None, :] > t[:, None] - WINDOW) & (pw[None, :] >= 0)
    sw = jnp.einsum('btghd,bkgd->bghtk', qg, kw) * scale
    p_w = masked_softmax(sw, m_w)
    o_w = jnp.einsum('bghtk,bkgd->bghtd', p_w, vw)

    o = jnp.stack([o_c, o_s, o_w], axis=-1)
    return jnp.einsum('bghtdc,btghc->btghd', o, gates.reshape(B, T, G, HPG, 3)).reshape(B, T, N_QW)


def moe_route(x, w_group, b_group, w_expert, b_expert):
    N = x.shape[0]
    pg = jax.nn.softmax((x @ w_group + b_group).astype(F32), axis=-1)
    grp = jnp.argmax(pg, axis=-1)
    p_grp = jnp.max(pg, axis=-1)
    le = (x @ w_expert + b_expert).astype(F32).reshape(N, N_GROUPS, EXPERTS_PER_GROUP)
    pe = jax.nn.softmax(le[jnp.arange(N), grp], axis=-1)
    top_p, top_i = lax.top_k(pe, TOP_K_FINE)
    wts = p_grp[:, None] * top_p / jnp.sum(top_p, axis=-1, keepdims=True)
    eid = grp[:, None] * EXPERTS_PER_GROUP + top_i
    return eid.astype(jnp.int32), wts


MOE_TM = 256


def _moe_ffn_kernel(be_ref, x_ref, w1_ref, w3_ref, w2_ref, wt_ref, o_ref):
    del be_ref
    x = x_ref[...]
    a = _mxu(x, w1_ref[0])
    b = _mxu(x, w3_ref[0])
    h = (a * jax.nn.sigmoid(a) * b).astype(BF16)
    o_ref[...] = _mxu(h, w2_ref[0]) * wt_ref[...]


def moe_ffn(rows, row_wt, blk_e, w1, w3, w2):
    m_pad, d = rows.shape
    n_blk = m_pad // MOE_TM
    f = w1.shape[2]
    return pl.pallas_call(
        _moe_ffn_kernel,
        grid_spec=pltpu.PrefetchScalarGridSpec(
            num_scalar_prefetch=1,
            grid=(n_blk,),
            in_specs=[
                pl.BlockSpec((MOE_TM, d), lambda i, be: (i, 0)),
                pl.BlockSpec((1, d, f), lambda i, be: (be[i], 0, 0)),
                pl.BlockSpec((1, d, f), lambda i, be: (be[i], 0, 0)),
                pl.BlockSpec((1, f, d), lambda i, be: (be[i], 0, 0)),
                pl.BlockSpec((MOE_TM, 1), lambda i, be: (i, 0)),
            ],
            out_specs=pl.BlockSpec((MOE_TM, d), lambda i, be: (i, 0)),
        ),
        out_shape=jax.ShapeDtypeStruct((m_pad, d), F32),
        compiler_params=pltpu.CompilerParams(
            dimension_semantics=("arbitrary",), vmem_limit_bytes=VMEM_LIMIT),
        name="moe_ffn",
    )(blk_e, rows, w1, w3, w2, row_wt)


def hier_moe_apply(x, eid, wts, w_e1, w_e3, w_e2):
    N, D = x.shape
    M = N * TOP_K_FINE
    eid_f = eid.reshape(-1)
    order = jnp.argsort(eid_f)
    e_s = eid_f[order]
    counts = jnp.zeros((N_EXPERTS,), jnp.int32).at[eid_f].add(1)
    padded = (counts + MOE_TM - 1) // MOE_TM * MOE_TM
    p_end = jnp.cumsum(padded)
    p_start = p_end - padded
    c_start = jnp.cumsum(counts) - counts
    dest = p_start[e_s] + jnp.arange(M, dtype=jnp.int32) - c_start[e_s]
    n_blk = (M + N_EXPERTS * (MOE_TM - 1) + MOE_TM - 1) // MOE_TM
    m_pad = n_blk * MOE_TM
    src_tok = jnp.zeros((m_pad,), jnp.int32).at[dest].set((order // TOP_K_FINE).astype(jnp.int32))
    row_wt = jnp.zeros((m_pad,), F32).at[dest].set(wts.reshape(-1)[order])
    pos = jnp.zeros((M,), jnp.int32).at[order].set(dest)
    blk_e = jnp.minimum(jnp.searchsorted(p_end, jnp.arange(n_blk, dtype=jnp.int32) * MOE_TM, side='right'),
                        N_EXPERTS - 1).astype(jnp.int32)
    rows = x.astype(BF16)[src_tok]
    out = moe_ffn(rows, row_wt.reshape(m_pad, 1), blk_e,
                  w_e1.astype(BF16), w_e3.astype(BF16), w_e2.astype(BF16))
    pos = pos.reshape(N, TOP_K_FINE)
    return out[pos[:, 0]] + out[pos[:, 1]]


def hybrid_layer(x, conv_prev, C0, n0, m0, paged, win_past, g_mix, w_in, w_conv, b_i, b_f, g_mh, g_q, g_kc, g_ks, g_kw,
                 pe_ck, w_ck1, w_ck2, pe_cv, w_cv1, w_cv2, w_proj_m, w_proj_n, w_out, g_ffn, w_group, b_group,
                 w_expert, b_expert, w_e1, w_e3, w_e2, *, use_pallas_proj):
    B, T, D = x.shape
    WB = win_past.shape[1]
    split_at = np.cumsum(SPLITS)[:-1].tolist()
    if use_pallas_proj:
        d_pad = -(-D_IN // 896) * 896
        w_pad = jnp.pad(w_in, ((0, 0), (0, d_pad - D_IN))).astype(BF16)
        proj = norm_proj(x.reshape(B * T, D), g_mix, w_pad)[:, :D_IN].reshape(B, T, D_IN)
    else:
        proj = rms_norm(x, g_mix) @ w_in
    mq, mk, mv, mi, mf, mo, nq, nkv, ng, gate_m, gate_n = jnp.split(proj, split_at, axis=-1)
    qk, conv_state = short_conv(jnp.concatenate([mq, mk], axis=-1), conv_prev, w_conv)
    qk = jax.nn.silu(qk)
    q_m = qk[..., :M_QK].reshape(B, T, M_HEADS, M_DQK).astype(F32)
    k_m = qk[..., M_QK:].reshape(B, T, M_HEADS, M_DQK).astype(F32) * (M_DQK ** -0.5)
    v_m = mv.reshape(B, T, M_HEADS, M_DV).astype(F32)
    i_pre = (mi + b_i).astype(F32)
    logf = jax.nn.log_sigmoid((mf + b_f).astype(F32))
    h_m, C1, n1, m1 = mlstm_chunkwise(q_m, k_m, v_m, i_pre, logf, C0.astype(F32), n0.astype(F32), m0.astype(F32))
    h_m = rms_norm(h_m, g_mh).reshape(B, T, M_VW).astype(x.dtype) * jax.nn.sigmoid(mo)
    qn = rms_norm(nq.reshape(B, T, N_HEADS, HEAD_DIM), g_q)
    kv_new = nkv.reshape(B, T, 6, N_KV, HEAD_DIM)
    kv_rows = kv_new[:, :, :4]
    win_all = jnp.concatenate([win_past.astype(x.dtype), kv_new[:, :, 4:]], axis=1)
    if paged is None:
        h_n = nsa_prompt(nq, ng, kv_new, g_q, g_kc, g_ks, g_kw, pe_ck, w_ck1, w_ck2, pe_cv, w_cv1, w_cv2)
    else:
        gates_n = jax.nn.sigmoid(ng).reshape(B, T, N_HEADS, 3)
        h_n = nsa_sample(qn, gates_n, paged[0], paged[1], kv_rows, win_all, g_kc, g_ks, g_kw,
                         pe_ck, w_ck1, w_ck2, pe_cv, w_cv1, w_cv2)
    win_state = win_all[:, -min(WINDOW, WB + T):]
    u = jax.nn.sigmoid(gate_m) * (h_m @ w_proj_m) + jax.nn.sigmoid(gate_n) * (h_n.astype(x.dtype) @ w_proj_n)
    x1 = x + u @ w_out
    h2 = rms_norm(x1, g_ffn).reshape(B * T, D)
    eid, wts = moe_route(h2, w_group, b_group, w_expert, b_expert)
    return x1, h2, eid, wts, kv_rows, win_state, C1, n1, m1, conv_state


def kernel(x_prompt, x_sample, cache_kv, page_table, state_win, state_C, state_n, state_m, state_conv, g_mix, w_in, w_conv, b_i, b_f, g_mh, g_q, g_kc, g_ks, g_kw, pe_ck, w_ck1, w_ck2, pe_cv, w_cv1, w_cv2, w_proj_m, w_proj_n, w_out, g_ffn, w_group, b_group, w_expert, b_expert, w_e1, w_e3, w_e2):
    B = x_prompt.shape[0]
    DB = x_sample.shape[0]
    n_pages = page_table.shape[1]
    dt = x_prompt.dtype
    weights = (g_mix, w_in, w_conv, b_i, b_f, g_mh, g_q, g_kc, g_ks, g_kw, pe_ck, w_ck1, w_ck2, pe_cv, w_cv1, w_cv2,
               w_proj_m, w_proj_n, w_out, g_ffn, w_group, b_group, w_expert, b_expert, w_e1, w_e3, w_e2)
    x1_p, h2_p, eid_p, wts_p, kv_p, win_p, C_p, n_p, m_p, conv_p = hybrid_layer(
        x_prompt, jnp.zeros((B, M_CONV - 1, 2 * M_QK), dt), jnp.zeros((B, M_HEADS, M_DQK, M_DV), F32),
        jnp.zeros((B, M_HEADS, M_DQK), F32), jnp.zeros((B, M_HEADS), F32),
        None, jnp.zeros((B, 0, 2, N_KV, HEAD_DIM), dt), *weights, use_pallas_proj=True)
    with jax.default_matmul_precision("highest"):
        x1_s, h2_s, eid_s, wts_s, kv_s, win_s, C_s, n_s, m_s, conv_s = hybrid_layer(
            x_sample, state_conv, state_C, state_n, state_m, (cache_kv, page_table), state_win, *weights,
            use_pallas_proj=False)
    n_p_tok = h2_p.shape[0]
    moe = hier_moe_apply(jnp.concatenate([h2_p, h2_s], axis=0), jnp.concatenate([eid_p, eid_s], axis=0),
                         jnp.concatenate([wts_p, wts_s], axis=0), w_e1, w_e3, w_e2)
    y_p = x1_p + moe[:n_p_tok].reshape(x1_p.shape)
    y_s = x1_s + moe[n_p_tok:].reshape(x1_s.shape)
    return (y_p, y_s, kv_p, kv_s, win_p, win_s, C_p, C_s, n_p, n_s, m_p, m_s, conv_p, conv_s)
```

```python
import functools
import math

import jax
import jax.numpy as jnp
import numpy as np
from jax import lax
from jax.experimental import pallas as pl
from jax.experimental.pallas import tpu as pltpu

D_MODEL = 2048
M_HEADS = 4
M_DQK = 128
M_DV = 256
M_CONV = 4
M_CHUNK = 64
N_HEADS = 16
N_KV = 4
HEAD_DIM = 64
CMP_BLK = 32
SEL_BLK = 64
N_SELECT = 16
WINDOW = 512
NSA_QBLK = 64
FORCE_SCORE = 1e4
N_GROUPS = 4
EXPERTS_PER_GROUP = 8
N_EXPERTS = N_GROUPS * EXPERTS_PER_GROUP
TOP_K_FINE = 2
MOE_BLK = 128
EPS = 1e-6
M_QK = M_HEADS * M_DQK
M_VW = M_HEADS * M_DV
N_QW = N_HEADS * HEAD_DIM
N_KVW = 6 * N_KV * HEAD_DIM
SPLITS = (M_QK, M_QK, M_VW, M_HEADS, M_HEADS, M_VW, N_QW, N_KVW, 3 * N_HEADS, D_MODEL, D_MODEL)
D_IN = sum(SPLITS)

F32 = jnp.float32
BF16 = jnp.bfloat16
VMEM_LIMIT = 48 * 1024 * 1024


def _mxu(a, b):
    return jnp.dot(a, b, preferred_element_type=F32, precision=lax.Precision.DEFAULT)


def _norm_proj_kernel(x_ref, g_ref, w_ref, o_ref, hx_ref):
    @pl.when(pl.program_id(1) == 0)
    def _():
        x = x_ref[...]
        r = lax.rsqrt(jnp.mean(x * x, axis=-1, keepdims=True) + EPS)
        hx_ref[...] = (x * r * g_ref[...]).astype(BF16)

    o_ref[...] = _mxu(hx_ref[...], w_ref[...])


def norm_proj(x, g, w_bf16, *, tm=1024, tn=896):
    n, d = x.shape
    k = w_bf16.shape[1]
    return pl.pallas_call(
        _norm_proj_kernel,
        grid=(n // tm, k // tn),
        in_specs=[
            pl.BlockSpec((tm, d), lambda i, j: (i, 0)),
            pl.BlockSpec((1, d), lambda i, j: (0, 0)),
            pl.BlockSpec((d, tn), lambda i, j: (0, j)),
        ],
        out_specs=pl.BlockSpec((tm, tn), lambda i, j: (i, j)),
        out_shape=jax.ShapeDtypeStruct((n, k), F32),
        scratch_shapes=[pltpu.VMEM((tm, d), BF16)],
        compiler_params=pltpu.CompilerParams(
            dimension_semantics=("parallel", "arbitrary"), vmem_limit_bytes=VMEM_LIMIT),
        name="norm_proj",
    )(x, g.reshape(1, d), w_bf16)


def _rms_rows_kernel(x_ref, g_ref, o_ref):
    x = x_ref[...]
    r = lax.rsqrt(jnp.mean(x * x, axis=-1, keepdims=True) + EPS)
    o_ref[...] = (x * r * g_ref[...]).astype(o_ref.dtype)


def rms_rows(x, g, out_dtype, *, tr=2048):
    rows, c = x.shape
    tr = min(tr, rows)
    return pl.pallas_call(
        _rms_rows_kernel,
        grid=(rows // tr,),
        in_specs=[pl.BlockSpec((tr, c), lambda i: (i, 0)), pl.BlockSpec((1, c), lambda i: (0, 0))],
        out_specs=pl.BlockSpec((tr, c), lambda i: (i, 0)),
        out_shape=jax.ShapeDtypeStruct((rows, c), out_dtype),
        compiler_params=pltpu.CompilerParams(dimension_semantics=("parallel",)),
        name="rms_rows",
    )(x, g.reshape(1, c))


def _split_bf16(x):
    hi = lax.bitcast_convert_type(
        lax.bitcast_convert_type(x, jnp.int32) & jnp.int32(-65536), F32)
    return hi.astype(BF16), (x - hi).astype(BF16)


def _dot_split(a, b_hi, b_lo):
    a_hi, a_lo = _split_bf16(a)
    return _mxu(a_hi, b_hi) + (_mxu(a_lo, b_hi) + _mxu(a_hi, b_lo))


def _compress_kernel(x_ref, pe_ref, w1_ref, w1l_ref, w2_ref, w2l_ref, g_ref, o_ref, *, normalize, precise):
    x = x_ref[...] + pe_ref[...]
    if precise:
        h = _dot_split(x, w1_ref[...], w1l_ref[...])
    else:
        h = _mxu(x.astype(BF16), w1_ref[...])
    h = h * jax.nn.sigmoid(h)
    if precise:
        y = _dot_split(h, w2_ref[...], w2l_ref[...])
    else:
        y = _mxu(h.astype(BF16), w2_ref[...])
    if normalize:
        y = y * lax.rsqrt(jnp.mean(y * y, axis=-1, keepdims=True) + EPS) * g_ref[...]
    o_ref[...] = y.astype(o_ref.dtype)


def compress_rows(x, pe, w1, w2, g, *, normalize, precise=False, tr=512):
    rows, k = x.shape
    f = w1.shape[-1]
    d = w2.shape[1]
    tr = min(tr, rows)
    w1_hi, w1_lo = _split_bf16(w1.reshape(k, f))
    w2_hi, w2_lo = _split_bf16(w2)
    return pl.pallas_call(
        functools.partial(_compress_kernel, normalize=normalize, precise=precise),
        grid=(rows // tr,),
        in_specs=[
            pl.BlockSpec((tr, k), lambda i: (i, 0)),
            pl.BlockSpec((1, k), lambda i: (0, 0)),
            pl.BlockSpec((k, f), lambda i: (0, 0)),
            pl.BlockSpec((k, f), lambda i: (0, 0)),
            pl.BlockSpec((f, d), lambda i: (0, 0)),
            pl.BlockSpec((f, d), lambda i: (0, 0)),
            pl.BlockSpec((1, d), lambda i: (0, 0)),
        ],
        out_specs=pl.BlockSpec((tr, d), lambda i: (i, 0)),
        out_shape=jax.ShapeDtypeStruct((rows, d), F32 if precise else BF16),
        compiler_params=pltpu.CompilerParams(dimension_semantics=("parallel",)),
        name="nsa_compress",
    )(x, pe.reshape(1, k), w1_hi, w1_lo, w2_hi, w2_lo, g.reshape(1, d))


CMP_TILE_ROWS = 128


def _compress_paged_kernel(x_ref, pe_ref, w_hi_ref, w_lo_ref, w2_hi_ref, w2_lo_ref, g_ref, o_ref, *,
                           normalize, cpp, hidden):
    rows = x_ref.shape[0]
    hd = x_ref.shape[1]
    acc = jnp.zeros((rows, cpp * hidden), F32)
    for dp in range(hd // 2):
        a = x_ref[:, 2 * dp, :] + pe_ref[pl.ds(2 * dp, 1), :]
        b = x_ref[:, 2 * dp + 1, :] + pe_ref[pl.ds(2 * dp + 1, 1), :]
        acc = acc + _dot_split(jnp.concatenate([a, b], axis=1), w_hi_ref[dp], w_lo_ref[dp])
    hid = acc * jax.nn.sigmoid(acc)
    for c in range(cpp):
        y = _dot_split(hid[:, c * hidden:(c + 1) * hidden], w2_hi_ref[...], w2_lo_ref[...])
        if normalize:
            y = y * lax.rsqrt(jnp.mean(y * y, axis=-1, keepdims=True) + EPS) * g_ref[...]
        o_ref[:, c * hd:(c + 1) * hd] = y


def compress_paged(x_tiles, pe, w1, w2, g, *, normalize):
    r, hd, page = x_tiles.shape
    cpp = page // CMP_BLK
    hidden = w1.shape[-1]
    eye = jnp.eye(cpp, dtype=F32)
    w_big = jnp.einsum('cC,ldf->dclCf', eye, w1).reshape(hd // 2, 2 * page, cpp * hidden)
    w_hi, w_lo = _split_bf16(w_big)
    w2_hi, w2_lo = _split_bf16(w2)
    pe_t = jnp.tile(pe.T, (1, cpp))
    const3 = lambda i: (0, 0, 0)
    const2 = lambda i: (0, 0)
    return pl.pallas_call(
        functools.partial(_compress_paged_kernel, normalize=normalize, cpp=cpp, hidden=hidden),
        grid=(r // CMP_TILE_ROWS,),
        in_specs=[
            pl.BlockSpec((CMP_TILE_ROWS, hd, page), lambda i: (i, 0, 0)),
            pl.BlockSpec((hd, page), const2),
            pl.BlockSpec((hd // 2, 2 * page, cpp * hidden), const3, pipeline_mode=pl.Buffered(1)),
            pl.BlockSpec((hd // 2, 2 * page, cpp * hidden), const3, pipeline_mode=pl.Buffered(1)),
            pl.BlockSpec((hidden, hd), const2),
            pl.BlockSpec((hidden, hd), const2),
            pl.BlockSpec((1, hd), const2),
        ],
        out_specs=pl.BlockSpec((CMP_TILE_ROWS, cpp * hd), lambda i: (i, 0)),
        out_shape=jax.ShapeDtypeStruct((r, cpp * hd), F32),
        compiler_params=pltpu.CompilerParams(dimension_semantics=("parallel",), vmem_limit_bytes=VMEM_LIMIT),
        name="nsa_compress_paged",
    )(x_tiles, pe_t, w_hi, w_lo, w2_hi, w2_lo, g.reshape(1, hd))


NSA_TQ = 128
NSA_TK = 256
HPG = N_HEADS // N_KV
NSA_Q = HPG * NSA_TQ
NEG_BIG = -1e30


def _nsa_prompt_kernel(qT_ref, gq_ref, gate_ref, kc_ref, vcT_ref, ks_ref, vsT_ref, kw_ref, vwT_ref,
                       o_ref, sc_ref, sel_ref, imp_ref, *, n_cmp, n_sel):
    j = pl.program_id(2)
    s0 = j * NSA_TQ
    q = qT_ref[0, 0, 0]
    r = lax.rsqrt(jnp.mean(q * q, axis=0, keepdims=True) + EPS)
    qn = (q * r * gq_ref[...] * (HEAD_DIM ** -0.5)).astype(BF16)
    lane = lax.broadcasted_iota(jnp.int32, (1, NSA_Q), 1)
    t_row = s0 + (lane & (NSA_TQ - 1))

    s = _mxu(kc_ref[0, 0], qn)
    c_end = lax.broadcasted_iota(jnp.int32, (n_cmp, NSA_Q), 0) * CMP_BLK + (CMP_BLK - 1)
    mask_c = c_end <= t_row
    s = jnp.where(mask_c, s, -jnp.inf)
    m = jnp.max(s, axis=0, keepdims=True)
    m = jnp.where(m > -jnp.inf, m, 0.0)
    e = jnp.where(mask_c, jnp.exp(s - m), 0.0)
    p_c = e / jnp.maximum(jnp.sum(e, axis=0, keepdims=True), 1e-30)
    o_c = _mxu(vcT_ref[0, 0], p_c.astype(BF16))

    imp_ref[...] = (p_c[:, 0:NSA_TQ] + p_c[:, NSA_TQ:2 * NSA_TQ]
                    + p_c[:, 2 * NSA_TQ:3 * NSA_TQ] + p_c[:, 3 * NSA_TQ:4 * NSA_TQ])
    imp = imp_ref[pl.ds(0, n_sel, stride=2), :] + imp_ref[pl.ds(1, n_sel, stride=2), :]
    n_iota = lax.broadcasted_iota(jnp.int32, (n_sel, NSA_TQ), 0)
    cur = (s0 + lax.broadcasted_iota(jnp.int32, (n_sel, NSA_TQ), 1)) // SEL_BLK
    valid = n_iota <= cur
    forced = (n_iota == 0) | (n_iota == cur) | (n_iota == cur - 1)
    score = jnp.where(valid, jnp.where(forced, FORCE_SCORE, imp), -jnp.inf)
    sc_ref[...] = score

    def rank_body(n2, rank):
        row = jnp.broadcast_to(sc_ref[pl.ds(n2, 1), :], (n_sel, NSA_TQ))
        beats = (row > score) | ((row == score) & (n2 < n_iota))
        return rank + jnp.where(beats, 1.0, 0.0)

    rank = lax.fori_loop(0, n_sel, rank_body, jnp.zeros((n_sel, NSA_TQ), F32))
    sel_ref[...] = jnp.where(valid & (rank < N_SELECT), 1.0, 0.0)

    key_iota = lax.broadcasted_iota(jnp.int32, (NSA_TK, NSA_Q), 0)

    def attend(k_ref, vT_ref, c_lo, c_hi, mask_fn):
        def body(c, carry):
            m_i, l_i, acc = carry
            k = k_ref[0, 0, pl.ds(pl.multiple_of(c * NSA_TK, NSA_TK), NSA_TK), :]
            sk = _mxu(k, qn)
            mask = mask_fn(c)
            sk = jnp.where(mask, sk, NEG_BIG)
            m_new = jnp.maximum(m_i, jnp.max(sk, axis=0, keepdims=True))
            alpha = jnp.exp(m_i - m_new)
            p = jnp.where(mask, jnp.exp(sk - m_new), 0.0)
            l_new = alpha * l_i + jnp.sum(p, axis=0, keepdims=True)
            acc = alpha * acc + _mxu(vT_ref[0, 0, c], p.astype(BF16))
            return m_new, l_new, acc

        init = (jnp.full((1, NSA_Q), NEG_BIG, F32), jnp.zeros((1, NSA_Q), F32),
                jnp.zeros((HEAD_DIM, NSA_Q), F32))
        _, l_f, acc_f = lax.fori_loop(c_lo, c_hi, body, init)
        return acc_f / jnp.maximum(l_f, 1e-30)

    def sel_mask(c):
        pos = c * NSA_TK + key_iota
        per_blk = [jnp.broadcast_to(sel_ref[pl.ds(c * (NSA_TK // SEL_BLK) + i, 1), :], (SEL_BLK, NSA_TQ))
                   for i in range(NSA_TK // SEL_BLK)]
        sm = jnp.concatenate(per_blk, axis=0)
        sm = jnp.concatenate([sm] * HPG, axis=1)
        return (sm > 0.5) & (pos <= t_row)

    def win_mask(c):
        pos = c * NSA_TK + key_iota
        return (pos <= t_row) & (pos > t_row - WINDOW)

    c_hi = (s0 + NSA_TQ + NSA_TK - 1) // NSA_TK
    o_s = attend(ks_ref, vsT_ref, 0, c_hi, sel_mask)
    o_w = attend(kw_ref, vwT_ref, jnp.maximum(s0 - (WINDOW - 1), 0) // NSA_TK, c_hi, win_mask)

    g = jax.nn.sigmoid(gate_ref[0, 0, 0])
    o_ref[0, 0, 0] = g[0:1] * o_c + g[1:2] * o_s + g[2:3] * o_w


def nsa_prompt(nq, ng, kv_new, g_q, g_kc, g_ks, g_kw, pe_ck, w_ck1, w_ck2, pe_cv, w_cv1, w_cv2):
    B, T = nq.shape[:2]
    G = N_KV
    nqb = T // NSA_TQ
    n_cmp = T // CMP_BLK
    n_sel = T // SEL_BLK
    nch = T // NSA_TK
    qT = nq.reshape(B, nqb, NSA_TQ, G, HPG, HEAD_DIM).transpose(0, 3, 1, 5, 4, 2).reshape(B, G, nqb, HEAD_DIM, NSA_Q)
    gT = ng.reshape(B, nqb, NSA_TQ, G, HPG, 3).transpose(0, 3, 1, 5, 4, 2).reshape(B, G, nqb, 3, NSA_Q)
    kv_g = kv_new.transpose(2, 0, 3, 1, 4)
    xk = kv_g[0].reshape(B * G * n_cmp, CMP_BLK * HEAD_DIM)
    xv = kv_g[1].reshape(B * G * n_cmp, CMP_BLK * HEAD_DIM)
    kc = compress_rows(xk, pe_ck, w_ck1, w_ck2, g_kc, normalize=True).reshape(B, G, n_cmp, HEAD_DIM)
    vc = compress_rows(xv, pe_cv, w_cv1, w_cv2, g_kc, normalize=False).reshape(B, G, n_cmp, HEAD_DIM)
    vcT = vc.transpose(0, 1, 3, 2)
    ks = rms_rows(kv_g[2].reshape(B * G * T, HEAD_DIM), g_ks, BF16).reshape(B, G, T, HEAD_DIM)
    kw = rms_rows(kv_g[4].reshape(B * G * T, HEAD_DIM), g_kw, BF16).reshape(B, G, T, HEAD_DIM)
    vsT = kv_g[3].astype(BF16).reshape(B, G, nch, NSA_TK, HEAD_DIM).transpose(0, 1, 2, 4, 3)
    vwT = kv_g[5].astype(BF16).reshape(B, G, nch, NSA_TK, HEAD_DIM).transpose(0, 1, 2, 4, 3)

    bg = lambda b, g, j: (b, g, 0, 0)
    bg5 = lambda b, g, j: (b, g, 0, 0, 0)
    bgj = lambda b, g, j: (b, g, j, 0, 0)
    oT = pl.pallas_call(
        functools.partial(_nsa_prompt_kernel, n_cmp=n_cmp, n_sel=n_sel),
        grid=(B, G, nqb),
        in_specs=[
            pl.BlockSpec((1, 1, 1, HEAD_DIM, NSA_Q), bgj),
            pl.BlockSpec((HEAD_DIM, 1), lambda b, g, j: (0, 0)),
            pl.BlockSpec((1, 1, 1, 3, NSA_Q), bgj),
            pl.BlockSpec((1, 1, n_cmp, HEAD_DIM), bg),
            pl.BlockSpec((1, 1, HEAD_DIM, n_cmp), bg),
            pl.BlockSpec((1, 1, T, HEAD_DIM), bg),
            pl.BlockSpec((1, 1, nch, HEAD_DIM, NSA_TK), bg5),
            pl.BlockSpec((1, 1, T, HEAD_DIM), bg),
            pl.BlockSpec((1, 1, nch, HEAD_DIM, NSA_TK), bg5),
        ],
        out_specs=pl.BlockSpec((1, 1, 1, HEAD_DIM, NSA_Q), bgj),
        out_shape=jax.ShapeDtypeStruct((B, G, nqb, HEAD_DIM, NSA_Q), F32),
        scratch_shapes=[pltpu.VMEM((n_sel, NSA_TQ), F32), pltpu.VMEM((n_sel, NSA_TQ), F32),
                        pltpu.VMEM((n_cmp, NSA_TQ), F32)],
        compiler_params=pltpu.CompilerParams(
            dimension_semantics=("parallel", "parallel", "arbitrary"), vmem_limit_bytes=VMEM_LIMIT),
        name="nsa_prompt",
    )(qT, g_q.reshape(HEAD_DIM, 1), gT, kc, vcT, ks, vsT, kw, vwT)
    return oT.reshape(B, G, nqb, HEAD_DIM, HPG, NSA_TQ).transpose(0, 2, 5, 1, 4, 3).reshape(B, T, N_QW)


def rms_norm(x, g):
    xf = x.astype(F32)
    y = xf * lax.rsqrt(jnp.mean(xf * xf, axis=-1, keepdims=True) + EPS)
    return (y * g.astype(F32)).astype(x.dtype)


def masked_softmax(s, mask):
    s = jnp.where(mask, s.astype(F32), -jnp.inf)
    m = jnp.max(s, axis=-1, keepdims=True)
    m = jnp.where(jnp.isfinite(m), m, 0.0)
    e = jnp.where(mask, jnp.exp(s - m), 0.0)
    return e / jnp.maximum(jnp.sum(e, axis=-1, keepdims=True), 1e-30)


def short_conv(u, prev, w):
    T = u.shape[1]
    full = jnp.concatenate([prev.astype(u.dtype), u], axis=1)
    out = full[:, 0:T] * w[0]
    for j in range(1, M_CONV):
        out = out + full[:, j:j + T] * w[j]
    return out, full[:, T:]


def mlstm_chunkwise(q, k, v, i_pre, logf, C0, n0, m0):
    B, T = q.shape[:2]
    L = M_CHUNK if T % M_CHUNK == 0 else T
    nc = T // L

    def chunks(a):
        a = a.reshape((B, nc, L) + a.shape[2:])
        return jnp.moveaxis(jnp.moveaxis(a, 3, 2), 1, 0)

    causal = jnp.tril(jnp.ones((L, L), bool))

    def step(carry, xs):
        C, n, m = carry
        qc, kc, vc, ic, fc = xs
        b = jnp.cumsum(fc, axis=-1)
        logD = jnp.where(causal, b[..., :, None] - b[..., None, :] + ic[..., None, :], -jnp.inf)
        inter = b + m[..., None]
        m_t = jnp.maximum(inter, jnp.max(logD, axis=-1))
        a = jnp.exp(inter - m_t)
        S = jnp.einsum('bhtd,bhsd->bhts', qc, kc) * jnp.exp(logD - m_t[..., None])
        num = a[..., None] * jnp.einsum('bhtd,bhde->bhte', qc, C) + jnp.einsum('bhts,bhse->bhte', S, vc)
        den = a * jnp.einsum('bhtd,bhd->bht', qc, n) + jnp.sum(S, axis=-1)
        h = num / jnp.maximum(jnp.abs(den), jnp.exp(-m_t))[..., None]
        m_new = m_t[..., -1]
        w = jnp.exp(b[..., -1:] - b + ic - m_new[..., None])
        aL = jnp.exp(b[..., -1] + m - m_new)
        C_new = aL[..., None, None] * C + jnp.einsum('bhs,bhsd,bhse->bhde', w, kc, vc)
        n_new = aL[..., None] * n + jnp.einsum('bhs,bhsd->bhd', w, kc)
        return (C_new, n_new, m_new), h

    (C, n, m), h = lax.scan(step, (C0, n0, m0), (chunks(q), chunks(k), chunks(v), chunks(i_pre), chunks(logf)))
    h = jnp.moveaxis(jnp.moveaxis(h, 0, 1), 2, 3).reshape(B, T, M_HEADS, M_DV)
    return h, C, n, m


def compress(rows, pe, w1, w2):
    B, Lk = rows.shape[:2]
    nc = Lk // CMP_BLK
    blk = rows[:, :nc * CMP_BLK].reshape(B, nc, CMP_BLK, N_KV, HEAD_DIM) + pe[:, None, :]
    hid = jax.nn.silu(jnp.einsum('bclgd,ldf->bcgf', blk, w1))
    return jnp.einsum('bcgf,fd->bcgd', hid, w2)


def nsa_attention(q, gates, kv_all, kw_full, q0, g_kc, g_ks, g_kw, pe_ck, w_ck1, w_ck2, pe_cv, w_cv1, w_cv2):
    B, T = q.shape[:2]
    Lk = kv_all.shape[1]
    hpg = N_HEADS // N_KV
    scale = HEAD_DIM ** -0.5
    kc = rms_norm(compress(kv_all[:, :, 0], pe_ck, w_ck1, w_ck2), g_kc)
    vc = compress(kv_all[:, :, 1], pe_cv, w_cv1, w_cv2)
    nc = kc.shape[1]
    n_sel = -(-Lk // SEL_BLK)
    pad = n_sel * SEL_BLK - Lk
    ks = jnp.pad(rms_norm(kv_all[:, :, 2], g_ks), ((0, 0), (0, pad), (0, 0), (0, 0)))
    ks = ks.reshape(B, n_sel, SEL_BLK, N_KV, HEAD_DIM).transpose(0, 3, 1, 2, 4)
    vs = jnp.pad(kv_all[:, :, 3], ((0, 0), (0, pad), (0, 0), (0, 0)))
    vs = vs.reshape(B, n_sel, SEL_BLK, N_KV, HEAD_DIM).transpose(0, 3, 1, 2, 4)
    kw = rms_norm(kw_full[:, :, 0], g_kw)
    vw = kw_full[:, :, 1]
    top = min(N_SELECT, n_sel)
    ratio = SEL_BLK // CMP_BLK
    qb = NSA_QBLK if T % NSA_QBLK == 0 else T
    nqb = T // qb
    qg = q.reshape(B, nqb, qb, N_KV, hpg, HEAD_DIM).transpose(1, 0, 3, 4, 2, 5)
    gg = jnp.moveaxis(gates.reshape(B, nqb, qb, N_HEADS, 3), 1, 0)
    cmp_end = (jnp.arange(nc) + 1) * CMP_BLK - 1
    blk_ids = jnp.arange(n_sel)
    b_ix = jnp.arange(B)[:, None, None, None]
    g_ix = jnp.arange(N_KV)[None, :, None, None]

    def block(args):
        qi, gi, j = args
        s0 = j * qb
        t = q0 + s0 + jnp.arange(qb)
        sc = jnp.einsum('bghqd,bcgd->bghqc', qi, kc) * scale
        p_c = masked_softmax(sc, cmp_end[None, :] <= t[:, None])
        o_c = jnp.einsum('bghqc,bcgd->bghqd', p_c.astype(vc.dtype), vc)
        imp = jnp.sum(p_c, axis=2)
        imp = jnp.pad(imp, ((0, 0), (0, 0), (0, 0), (0, n_sel * ratio - nc)))
        imp = imp.reshape(B, N_KV, qb, n_sel, ratio).sum(-1)
        cur = t // SEL_BLK
        valid = blk_ids[None, :] <= cur[:, None]
        forced = (blk_ids[None, :] == 0) | (blk_ids[None, :] == cur[:, None]) | (blk_ids[None, :] == cur[:, None] - 1)
        imp = jnp.where(valid, jnp.where(forced, FORCE_SCORE, imp), -jnp.inf)
        top_v, top_i = lax.top_k(imp, top)
        k_sel = ks[b_ix, g_ix, top_i]
        v_sel = vs[b_ix, g_ix, top_i]
        pos = top_i[..., None] * SEL_BLK + jnp.arange(SEL_BLK)
        m_s = (pos <= t[:, None, None]) & jnp.isfinite(top_v)[..., None]
        ss = (jnp.einsum('bghqd,bgqnkd->bghqnk', qi, k_sel) * scale).reshape(B, N_KV, hpg, qb, top * SEL_BLK)
        p_s = masked_softmax(ss, m_s.reshape(B, N_KV, 1, qb, top * SEL_BLK))
        o_s = jnp.einsum('bghqnk,bgqnkd->bghqd', p_s.reshape(B, N_KV, hpg, qb, top, SEL_BLK).astype(v_sel.dtype), v_sel)
        kwi = lax.dynamic_slice_in_dim(kw, s0, qb + WINDOW, axis=1)
        vwi = lax.dynamic_slice_in_dim(vw, s0, qb + WINDOW, axis=1)
        pw = q0 - WINDOW + s0 + jnp.arange(qb + WINDOW)
        m_w = (pw[None, :] <= t[:, None]) & (pw[None, :] > t[:, None] - WINDOW) & (pw[None, :] >= 0)
        sw = jnp.einsum('bghqd,bkgd->bghqk', qi, kwi) * scale
        p_w = masked_softmax(sw, m_w)
        o_w = jnp.einsum('bghqk,bkgd->bghqd', p_w.astype(vwi.dtype), vwi)
        o = jnp.stack([o_c, o_s, o_w], axis=-1)
        gi_r = gi.reshape(B, qb, N_KV, hpg, 3).transpose(0, 2, 3, 1, 4)
        return jnp.einsum('bghqdc,bghqc->bqghd', o, gi_r.astype(o.dtype))

    out = lax.map(block, (qg, gg, jnp.arange(nqb)))
    return jnp.moveaxis(out, 0, 1).reshape(B, T, N_QW)


def nsa_sample(q, gates, cache_kv, page_table, kv_rows, win_all, g_kc, g_ks, g_kw,
               pe_ck, w_ck1, w_ck2, pe_cv, w_cv1, w_cv2):
    B, T = q.shape[:2]
    n_pages = page_table.shape[1]
    page = cache_kv.shape[1]
    P = n_pages * page
    G, hd = N_KV, HEAD_DIM
    assert T < CMP_BLK and T <= SEL_BLK and P % SEL_BLK == 0 and page % SEL_BLK == 0
    assert win_all.shape[1] == WINDOW + T
    nc = P // CMP_BLK
    cpp = page // CMP_BLK
    n_last = P // SEL_BLK
    n_sel = n_last + 1
    top = min(N_SELECT, n_sel)
    ratio = SEL_BLK // CMP_BLK
    scale = hd ** -0.5
    t = P + jnp.arange(T)
    qg = q.reshape(B, T, G, HPG, hd)

    cache_t = cache_kv.transpose(0, 2, 3, 4, 1)
    flat_pages = page_table.reshape(-1)

    def cmp_tiles(kind):
        tiles = jax.vmap(lambda p: lax.dynamic_slice(cache_t, (p, kind, 0, 0, 0), (1, 1, G, hd, page))[0, 0])(flat_pages)
        return tiles.reshape(B * n_pages * G, hd, page)

    def cmp_rows(y):
        return y.reshape(B, n_pages, G, cpp, hd).transpose(0, 2, 1, 3, 4).reshape(B, G, nc, hd)

    kc = cmp_rows(compress_paged(cmp_tiles(0), pe_ck, w_ck1, w_ck2, g_kc, normalize=True))
    vc = cmp_rows(compress_paged(cmp_tiles(1), pe_cv, w_cv1, w_cv2, g_kc, normalize=False))
    cmp_end = (jnp.arange(nc) + 1) * CMP_BLK - 1
    sc = jnp.einsum('btghd,bgcd->bghtc', qg, kc) * scale
    p_c = masked_softmax(sc, cmp_end[None, :] <= t[:, None])
    o_c = jnp.einsum('bghtc,bgcd->bghtd', p_c, vc)

    imp = jnp.sum(p_c, axis=2)
    imp = jnp.pad(imp, ((0, 0), (0, 0), (0, 0), (0, n_sel * ratio - nc)))
    imp = imp.reshape(B, G, T, n_sel, ratio).sum(-1)
    blk_ids = jnp.arange(n_sel)
    cur = t // SEL_BLK
    valid = blk_ids[None, :] <= cur[:, None]
    forced = (blk_ids[None, :] == 0) | (blk_ids[None, :] == cur[:, None]) | (blk_ids[None, :] == cur[:, None] - 1)
    imp = jnp.where(valid, jnp.where(forced, FORCE_SCORE, imp), -jnp.inf)
    top_v, top_i = lax.top_k(imp, top)

    bpp = page // SEL_BLK
    blk_c = jnp.minimum(top_i, n_last - 1)
    b_ix = jnp.arange(B)[:, None, None, None]
    phys = page_table[b_ix, blk_c // bpp]
    g_ix = jnp.broadcast_to(jnp.arange(G)[None, :, None, None], top_i.shape)

    def take_tile(p, g):
        return lax.dynamic_slice(cache_t, (p, 2, g, 0, 0), (1, 2, 1, hd, page))[0, :, 0]

    tiles = jax.vmap(take_tile)(phys.reshape(-1), g_ix.reshape(-1)).reshape(B, G, T, top, 2, hd, page)
    new_tile = jnp.pad(kv_rows[:, :, 2:4], ((0, 0), (0, page - T), (0, 0), (0, 0), (0, 0)))
    new_tile = new_tile.transpose(0, 3, 2, 4, 1)[:, :, None, None]
    is_new = top_i == n_last
    tiles = jnp.where(is_new[..., None, None, None], new_tile, tiles)
    k_t = tiles[..., 0, :, :]
    v_t = tiles[..., 1, :, :]
    k_t = k_t * lax.rsqrt(jnp.mean(k_t * k_t, axis=-2, keepdims=True) + EPS) * g_ks[:, None]
    lane = jnp.arange(page)
    tile0 = jnp.where(is_new, n_last, blk_c // bpp * bpp)
    pos = tile0[..., None] * SEL_BLK + lane
    m_s = ((pos // SEL_BLK == top_i[..., None]) & (pos <= t[:, None, None])
           & jnp.isfinite(top_v)[..., None])
    ss = (jnp.einsum('btghd,bgtndp->bghtnp', qg, k_t) * scale).reshape(B, G, HPG, T, top * page)
    p_s = masked_softmax(ss, m_s.reshape(B, G, 1, T, top * page))
    o_s = jnp.einsum('bghtnp,bgtndp->bghtd', p_s.reshape(B, G, HPG, T, top, page), v_t)

    kw = rms_norm(win_all[:, :, 0], g_kw)
    vw = win_all[:, :, 1]
    pw = P - WINDOW + jnp.arange(T + WINDOW)
    m_w = (pw[None, :] <= t[:, None]) & (pw[None, :] > t[:, None] - WINDOW) & (pw[None, :] >= 0)
    sw = jnp.einsum('btghd,bkgd->bghtk', qg, kw) * scale
    p_w = masked_softmax(sw, m_w)
    o_w = jnp.einsum('bghtk,bkgd->bghtd', p_w, vw)

    o = jnp.stack([o_c, o_s, o_w], axis=-1)
    return jnp.einsum('bghtdc,btghc->btghd', o, gates.reshape(B, T, G, HPG, 3)).reshape(B, T, N_QW)


def moe_route(x, w_group, b_group, w_expert, b_expert):
    N = x.shape[0]
    pg = jax.nn.softmax((x @ w_group + b_group).astype(F32), axis=-1)
    grp = jnp.argmax(pg, axis=-1)
    p_grp = jnp.max(pg, axis=-1)
    le = (x @ w_expert + b_expert).astype(F32).reshape(N, N_GROUPS, EXPERTS_PER_GROUP)
    pe = jax.nn.softmax(le[jnp.arange(N), grp], axis=-1)
    top_p, top_i = lax.top_k(pe, TOP_K_FINE)
    wts = p_grp[:, None] * top_p / jnp.sum(top_p, axis=-1, keepdims=True)
    eid = grp[:, None] * EXPERTS_PER_GROUP + top_i
    return eid.astype(jnp.int32), wts


MOE_TM = 256


def _moe_ffn_kernel(be_ref, x_ref, w1_ref, w3_ref, w2_ref, wt_ref, o_ref):
    del be_ref
    x = x_ref[...]
    a = _mxu(x, w1_ref[0])
    b = _mxu(x, w3_ref[0])
    h = (a * jax.nn.sigmoid(a) * b).astype(BF16)
    o_ref[...] = _mxu(h, w2_ref[0]) * wt_ref[...]


def moe_ffn(rows, row_wt, blk_e, w1, w3, w2):
    m_pad, d = rows.shape
    n_blk = m_pad // MOE_TM
    f = w1.shape[2]
    return pl.pallas_call(
        _moe_ffn_kernel,
        grid_spec=pltpu.PrefetchScalarGridSpec(
            num_scalar_prefetch=1,
            grid=(n_blk,),
            in_specs=[
                pl.BlockSpec((MOE_TM, d), lambda i, be: (i, 0)),
                pl.BlockSpec((1, d, f), lambda i, be: (be[i], 0, 0)),
                pl.BlockSpec((1, d, f), lambda i, be: (be[i], 0, 0)),
                pl.BlockSpec((1, f, d), lambda i, be: (be[i], 0, 0)),
                pl.BlockSpec((MOE_TM, 1), lambda i, be: (i, 0)),
            ],
            out_specs=pl.BlockSpec((MOE_TM, d), lambda i, be: (i, 0)),
        ),
        out_shape=jax.ShapeDtypeStruct((m_pad, d), F32),
        compiler_params=pltpu.CompilerParams(
            dimension_semantics=("arbitrary",), vmem_limit_bytes=VMEM_LIMIT),
        name="moe_ffn",
    )(blk_e, rows, w1, w3, w2, row_wt)


def hier_moe_apply(x, eid, wts, w_e1, w_e3, w_e2):
    N, D = x.shape
    M = N * TOP_K_FINE
    eid_f = eid.reshape(-1)
    order = jnp.argsort(eid_f)
    e_s = eid_f[order]
    counts = jnp.zeros((N_EXPERTS,), jnp.int32).at[eid_f].add(1)
    padded = (counts + MOE_TM - 1) // MOE_TM * MOE_TM
    p_end = jnp.cumsum(padded)
    p_start = p_end - padded
    c_start = jnp.cumsum(counts) - counts
    dest = p_start[e_s] + jnp.arange(M, dtype=jnp.int32) - c_start[e_s]
    n_blk = (M + N_EXPERTS * (MOE_TM - 1) + MOE_TM - 1) // MOE_TM
    m_pad = n_blk * MOE_TM
    src_tok = jnp.zeros((m_pad,), jnp.int32).at[dest].set((order // TOP_K_FINE).astype(jnp.int32))
    row_wt = jnp.zeros((m_pad,), F32).at[dest].set(wts.reshape(-1)[order])
    pos = jnp.zeros((M,), jnp.int32).at[order].set(dest)
    blk_e = jnp.minimum(jnp.searchsorted(p_end, jnp.arange(n_blk, dtype=jnp.int32) * MOE_TM, side='right'),
                        N_EXPERTS - 1).astype(jnp.int32)
    rows = x.astype(BF16)[src_tok]
    out = moe_ffn(rows, row_wt.reshape(m_pad, 1), blk_e,
                  w_e1.astype(BF16), w_e3.astype(BF16), w_e2.astype(BF16))
    pos = pos.reshape(N, TOP_K_FINE)
    return out[pos[:, 0]] + out[pos[:, 1]]


def hybrid_layer(x, conv_prev, C0, n0, m0, paged, win_past, g_mix, w_in, w_conv, b_i, b_f, g_mh, g_q, g_kc, g_ks, g_kw,
                 pe_ck, w_ck1, w_ck2, pe_cv, w_cv1, w_cv2, w_proj_m, w_proj_n, w_out, g_ffn, w_group, b_group,
                 w_expert, b_expert, w_e1, w_e3, w_e2, *, use_pallas_proj):
    B, T, D = x.shape
    WB = win_past.shape[1]
    split_at = np.cumsum(SPLITS)[:-1].tolist()
    if use_pallas_proj:
        d_pad = -(-D_IN // 896) * 896
        w_pad = jnp.pad(w_in, ((0, 0), (0, d_pad - D_IN))).astype(BF16)
        proj = norm_proj(x.reshape(B * T, D), g_mix, w_pad)[:, :D_IN].reshape(B, T, D_IN)
    else:
        proj = rms_norm(x, g_mix) @ w_in
    mq, mk, mv, mi, mf, mo, nq, nkv, ng, gate_m, gate_n = jnp.split(proj, split_at, axis=-1)
    qk, conv_state = short_conv(jnp.concatenate([mq, mk], axis=-1), conv_prev, w_conv)
    qk = jax.nn.silu(qk)
    q_m = qk[..., :M_QK].reshape(B, T, M_HEADS, M_DQK).astype(F32)
    k_m = qk[..., M_QK:].reshape(B, T, M_HEADS, M_DQK).astype(F32) * (M_DQK ** -0.5)
    v_m = mv.reshape(B, T, M_HEADS, M_DV).astype(F32)
    i_pre = (mi + b_i).astype(F32)
    logf = jax.nn.log_sigmoid((mf + b_f).astype(F32))
    h_m, C1, n1, m1 = mlstm_chunkwise(q_m, k_m, v_m, i_pre, logf, C0.astype(F32), n0.astype(F32), m0.astype(F32))
    h_m = rms_norm(h_m, g_mh).reshape(B, T, M_VW).astype(x.dtype) * jax.nn.sigmoid(mo)
    qn = rms_norm(nq.reshape(B, T, N_HEADS, HEAD_DIM), g_q)
    kv_new = nkv.reshape(B, T, 6, N_KV, HEAD_DIM)
    kv_rows = kv_new[:, :, :4]
    win_all = jnp.concatenate([win_past.astype(x.dtype), kv_new[:, :, 4:]], axis=1)
    if paged is None:
        h_n = nsa_prompt(nq, ng, kv_new, g_q, g_kc, g_ks, g_kw, pe_ck, w_ck1, w_ck2, pe_cv, w_cv1, w_cv2)
    else:
        gates_n = jax.nn.sigmoid(ng).reshape(B, T, N_HEADS, 3)
        h_n = nsa_sample(qn, gates_n, paged[0], paged[1], kv_rows, win_all, g_kc, g_ks, g_kw,
                         pe_ck, w_ck1, w_ck2, pe_cv, w_cv1, w_cv2)
    win_state = win_all[:, -min(WINDOW, WB + T):]
    u = jax.nn.sigmoid(gate_m) * (h_m @ w_proj_m) + jax.nn.sigmoid(gate_n) * (h_n.astype(x.dtype) @ w_proj_n)
    x1 = x + u @ w_out
    h2 = rms_norm(x1, g_ffn).reshape(B * T, D)
    eid, wts = moe_route(h2, w_group, b_group, w_expert, b_expert)
    return x1, h2, eid, wts, kv_rows, win_state, C1, n1, m1, conv_state


def kernel(x_prompt, x_sample, cache_kv, page_table, state_win, state_C, state_n, state_m, state_conv, g_mix, w_in, w_conv, b_i, b_f, g_mh, g_q, g_kc, g_ks, g_kw, pe_ck, w_ck1, w_ck2, pe_cv, w_cv1, w_cv2, w_proj_m, w_proj_n, w_out, g_ffn, w_group, b_group, w_expert, b_expert, w_e1, w_e3, w_e2):
    B = x_prompt.shape[0]
    DB = x_sample.shape[0]
    n_pages = page_table.shape[1]
    dt = x_prompt.dtype
    weights = (g_mix, w_in, w_conv, b_i, b_f, g_mh, g_q, g_kc, g_ks, g_kw, pe_ck, w_ck1, w_ck2, pe_cv, w_cv1, w_cv2,
               w_proj_m, w_proj_n, w_out, g_ffn, w_group, b_group, w_expert, b_expert, w_e1, w_e3, w_e2)
    x1_p, h2_p, eid_p, wts_p, kv_p, win_p, C_p, n_p, m_p, conv_p = hybrid_layer(
        x_prompt, jnp.zeros((B, M_CONV - 1, 2 * M_QK), dt), jnp.zeros((B, M_HEADS, M_DQK, M_DV), F32),
        jnp.zeros((B, M_HEADS, M_DQK), F32), jnp.zeros((B, M_HEADS), F32),
        None, jnp.zeros((B, 0, 2, N_KV, HEAD_DIM), dt), *weights, use_pallas_proj=True)
    with jax.default_matmul_precision("highest"):
        x1_s, h2_s, eid_s, wts_s, kv_s, win_s, C_s, n_s, m_s, conv_s = hybrid_layer(
            x_sample, state_conv, state_C, state_n, state_m, (cache_kv, page_table), state_win, *weights,
            use_pallas_proj=False)
    n_p_tok = h2_p.shape[0]
    moe = hier_moe_apply(jnp.concatenate([h2_p, h2_s], axis=0), jnp.concatenate([eid_p, eid_s], axis=0),
                         jnp.concatenate([wts_p, wts_s], axis=0), w_e1, w_e3, w_e2)
    y_p = x1_p + moe[:n_p_tok].reshape(x1_p.shape)
    y_s = x1_s + moe[n_p_tok:].reshape(x1_s.shape)
    return (y_p, y_s, kv_p, kv_s, win_p, win_s, C_p, C_s, n_p, n_s, m_p, m_s, conv_p, conv_s)
```

```python
import functools
import math

import jax
import jax.numpy as jnp
import numpy as np
from jax import lax
from jax.experimental import pallas as pl
from jax.experimental.pallas import tpu as pltpu

D_MODEL = 2048
M_HEADS = 4
M_DQK = 128
M_DV = 256
M_CONV = 4
M_CHUNK = 64
N_HEADS = 16
N_KV = 4
HEAD_DIM = 64
CMP_BLK = 32
SEL_BLK = 64
N_SELECT = 16
WINDOW = 512
NSA_QBLK = 64
FORCE_SCORE = 1e4
N_GROUPS = 4
EXPERTS_PER_GROUP = 8
N_EXPERTS = N_GROUPS * EXPERTS_PER_GROUP
TOP_K_FINE = 2
MOE_BLK = 128
EPS = 1e-6
M_QK = M_HEADS * M_DQK
M_VW = M_HEADS * M_DV
N_QW = N_HEADS * HEAD_DIM
N_KVW = 6 * N_KV * HEAD_DIM
SPLITS = (M_QK, M_QK, M_VW, M_HEADS, M_HEADS, M_VW, N_QW, N_KVW, 3 * N_HEADS, D_MODEL, D_MODEL)
D_IN = sum(SPLITS)

F32 = jnp.float32
BF16 = jnp.bfloat16
VMEM_LIMIT = 48 * 1024 * 1024


def _mxu(a, b):
    return jnp.dot(a, b, preferred_element_type=F32, precision=lax.Precision.DEFAULT)


def _norm_proj_kernel(x_ref, g_ref, w_ref, o_ref, hx_ref):
    @pl.when(pl.program_id(1) == 0)
    def _():
        x = x_ref[...]
        r = lax.rsqrt(jnp.mean(x * x, axis=-1, keepdims=True) + EPS)
        hx_ref[...] = (x * r * g_ref[...]).astype(BF16)

    o_ref[...] = _mxu(hx_ref[...], w_ref[...])


def norm_proj(x, g, w_bf16, *, tm=1024, tn=896):
    n, d = x.shape
    k = w_bf16.shape[1]
    return pl.pallas_call(
        _norm_proj_kernel,
        grid=(n // tm, k // tn),
        in_specs=[
            pl.BlockSpec((tm, d), lambda i, j: (i, 0)),
            pl.BlockSpec((1, d), lambda i, j: (0, 0)),
            pl.BlockSpec((d, tn), lambda i, j: (0, j)),
        ],
        out_specs=pl.BlockSpec((tm, tn), lambda i, j: (i, j)),
        out_shape=jax.ShapeDtypeStruct((n, k), F32),
        scratch_shapes=[pltpu.VMEM((tm, d), BF16)],
        compiler_params=pltpu.CompilerParams(
            dimension_semantics=("parallel", "arbitrary"), vmem_limit_bytes=VMEM_LIMIT),
        name="norm_proj",
    )(x, g.reshape(1, d), w_bf16)


def _rms_rows_kernel(x_ref, g_ref, o_ref):
    x = x_ref[...]
    r = lax.rsqrt(jnp.mean(x * x, axis=-1, keepdims=True) + EPS)
    o_ref[...] = (x * r * g_ref[...]).astype(o_ref.dtype)


def rms_rows(x, g, out_dtype, *, tr=2048):
    rows, c = x.shape
    tr = min(tr, rows)
    return pl.pallas_call(
        _rms_rows_kernel,
        grid=(rows // tr,),
        in_specs=[pl.BlockSpec((tr, c), lambda i: (i, 0)), pl.BlockSpec((1, c), lambda i: (0, 0))],
        out_specs=pl.BlockSpec((tr, c), lambda i: (i, 0)),
        out_shape=jax.ShapeDtypeStruct((rows, c), out_dtype),
        compiler_params=pltpu.CompilerParams(dimension_semantics=("parallel",)),
        name="rms_rows",
    )(x, g.reshape(1, c))


def _split_bf16(x):
    hi = lax.bitcast_convert_type(
        lax.bitcast_convert_type(x, jnp.int32) & jnp.int32(-65536), F32)
    return hi.astype(BF16), (x - hi).astype(BF16)


def _dot_split(a, b_hi, b_lo):
    a_hi, a_lo = _split_bf16(a)
    return _mxu(a_hi, b_hi) + (_mxu(a_lo, b_hi) + _mxu(a_hi, b_lo))


def _compress_kernel(x_ref, pe_ref, w1_ref, w1l_ref, w2_ref, w2l_ref, g_ref, o_ref, *, normalize, precise):
    x = x_ref[...] + pe_ref[...]
    if precise:
        h = _dot_split(x, w1_ref[...], w1l_ref[...])
    else:
        h = _mxu(x.astype(BF16), w1_ref[...])
    h = h * jax.nn.sigmoid(h)
    if precise:
        y = _dot_split(h, w2_ref[...], w2l_ref[...])
    else:
        y = _mxu(h.astype(BF16), w2_ref[...])
    if normalize:
        y = y * lax.rsqrt(jnp.mean(y * y, axis=-1, keepdims=True) + EPS) * g_ref[...]
    o_ref[...] = y.astype(o_ref.dtype)


def compress_rows(x, pe, w1, w2, g, *, normalize, precise=False, tr=512):
    rows, k = x.shape
    f = w1.shape[-1]
    d = w2.shape[1]
    tr = min(tr, rows)
    w1_hi, w1_lo = _split_bf16(w1.reshape(k, f))
    w2_hi, w2_lo = _split_bf16(w2)
    if not precise:
        w1_hi, w2_hi = w1.reshape(k, f).astype(BF16), w2.astype(BF16)
    return pl.pallas_call(
        functools.partial(_compress_kernel, normalize=normalize, precise=precise),
        grid=(rows // tr,),
        in_specs=[
            pl.BlockSpec((tr, k), lambda i: (i, 0)),
            pl.BlockSpec((1, k), lambda i: (0, 0)),
            pl.BlockSpec((k, f), lambda i: (0, 0)),
            pl.BlockSpec((k, f), lambda i: (0, 0)),
            pl.BlockSpec((f, d), lambda i: (0, 0)),
            pl.BlockSpec((f, d), lambda i: (0, 0)),
            pl.BlockSpec((1, d), lambda i: (0, 0)),
        ],
        out_specs=pl.BlockSpec((tr, d), lambda i: (i, 0)),
        out_shape=jax.ShapeDtypeStruct((rows, d), F32 if precise else BF16),
        compiler_params=pltpu.CompilerParams(dimension_semantics=("parallel",)),
        name="nsa_compress",
    )(x, pe.reshape(1, k), w1_hi, w1_lo, w2_hi, w2_lo, g.reshape(1, d))


CMP_TILE_ROWS = 1024
CMP_D_PER_STEP = 2


def _compress_paged_kernel(x_ref, pe_ref, w_hi_ref, w_lo_ref, w2_hi_ref, w2_lo_ref, g_ref, o_ref, acc_ref, *,
                           normalize, cpp, hidden, hd):
    dp = pl.program_id(1)

    @pl.when(dp == 0)
    def _():
        acc_ref[...] = jnp.zeros_like(acc_ref)

    x_hi, x_lo = _split_bf16(x_ref[...] + pe_ref[...])
    w_hi = w_hi_ref[0]
    acc_ref[...] += _mxu(jnp.concatenate([x_hi, x_lo, x_hi], axis=1),
                         jnp.concatenate([w_hi, w_hi, w_lo_ref[0]], axis=0))

    @pl.when(dp == pl.num_programs(1) - 1)
    def _():
        acc = acc_ref[...]
        hid = acc * jax.nn.sigmoid(acc)
        for c in range(cpp):
            y = _dot_split(hid[:, c * hidden:(c + 1) * hidden], w2_hi_ref[...], w2_lo_ref[...])
            if normalize:
                y = y * lax.rsqrt(jnp.mean(y * y, axis=-1, keepdims=True) + EPS) * g_ref[...]
            o_ref[:, c * hd:(c + 1) * hd] = y


def compress_paged(x_tiles, pe, w1, w2, g, *, normalize):
    r, hd, page = x_tiles.shape
    cpp = page // CMP_BLK
    hidden = w1.shape[-1]
    eye = jnp.eye(cpp, dtype=F32)
    nd = CMP_D_PER_STEP
    w_big = jnp.einsum('cC,ldf->dclCf', eye, w1).reshape(hd // nd, nd * page, cpp * hidden)
    w_hi, w_lo = _split_bf16(w_big)
    w2_hi, w2_lo = _split_bf16(w2)
    pe_t = jnp.tile(pe.T, (1, cpp)).reshape(1, hd * page)
    tr = min(CMP_TILE_ROWS, r)
    const2 = lambda i, dp: (0, 0)
    return pl.pallas_call(
        functools.partial(_compress_paged_kernel, normalize=normalize, cpp=cpp, hidden=hidden, hd=hd),
        grid=(r // tr, hd // nd),
        in_specs=[
            pl.BlockSpec((tr, nd * page), lambda i, dp: (i, dp)),
            pl.BlockSpec((1, nd * page), lambda i, dp: (0, dp)),
            pl.BlockSpec((1, nd * page, cpp * hidden), lambda i, dp: (dp, 0, 0)),
            pl.BlockSpec((1, nd * page, cpp * hidden), lambda i, dp: (dp, 0, 0)),
            pl.BlockSpec((hidden, hd), const2),
            pl.BlockSpec((hidden, hd), const2),
            pl.BlockSpec((1, hd), const2),
        ],
        out_specs=pl.BlockSpec((tr, cpp * hd), lambda i, dp: (i, 0)),
        out_shape=jax.ShapeDtypeStruct((r, cpp * hd), F32),
        scratch_shapes=[pltpu.VMEM((tr, cpp * hidden), F32)],
        compiler_params=pltpu.CompilerParams(
            dimension_semantics=("parallel", "arbitrary"), vmem_limit_bytes=VMEM_LIMIT),
        name="nsa_compress_paged",
    )(x_tiles.reshape(r, hd * page), pe_t, w_hi, w_lo, w2_hi, w2_lo, g.reshape(1, hd))


NSA_TQ = 128
NSA_TK = 512
HPG = N_HEADS // N_KV
NSA_Q = HPG * NSA_TQ
NEG_BIG = -1e30


def _nsa_prompt_kernel(qT_ref, gq_ref, gate_ref, kc_ref, vcT_ref, ks_ref, vsT_ref, kw_ref, vwT_ref,
                       o_ref, sc_ref, sel_ref, imp_ref, *, n_cmp, n_sel):
    j = pl.program_id(2)
    s0 = j * NSA_TQ
    q = qT_ref[0, 0, 0]
    r = lax.rsqrt(jnp.mean(q * q, axis=0, keepdims=True) + EPS)
    qn = (q * r * gq_ref[...] * (HEAD_DIM ** -0.5 * math.log2(math.e))).astype(BF16)
    lane = lax.broadcasted_iota(jnp.int32, (1, NSA_Q), 1)
    t_row = s0 + (lane & (NSA_TQ - 1))

    s = _mxu(kc_ref[0, 0], qn)
    c_end = lax.broadcasted_iota(jnp.int32, (n_cmp, NSA_Q), 0) * CMP_BLK + (CMP_BLK - 1)
    mask_c = c_end <= t_row
    s = jnp.where(mask_c, s, -jnp.inf)
    m = jnp.max(s, axis=0, keepdims=True)
    m = jnp.where(m > -jnp.inf, m, 0.0)
    e = jnp.where(mask_c, jnp.exp2(s - m), 0.0)
    p_c = e / jnp.maximum(jnp.sum(e, axis=0, keepdims=True), 1e-30)
    o_c = _mxu(vcT_ref[0, 0], p_c.astype(BF16))

    imp_ref[...] = (p_c[:, 0:NSA_TQ] + p_c[:, NSA_TQ:2 * NSA_TQ]
                    + p_c[:, 2 * NSA_TQ:3 * NSA_TQ] + p_c[:, 3 * NSA_TQ:4 * NSA_TQ])
    imp = imp_ref[pl.ds(0, n_sel, stride=2), :] + imp_ref[pl.ds(1, n_sel, stride=2), :]
    n_iota = lax.broadcasted_iota(jnp.int32, (n_sel, NSA_TQ), 0)
    cur = (s0 + lax.broadcasted_iota(jnp.int32, (n_sel, NSA_TQ), 1)) // SEL_BLK
    valid = n_iota <= cur
    forced = (n_iota == 0) | (n_iota == cur) | (n_iota == cur - 1)
    score = jnp.where(valid, jnp.where(forced, FORCE_SCORE, imp), -jnp.inf)
    sc_ref[...] = score

    def rank_body(n2, rank):
        row = jnp.broadcast_to(sc_ref[pl.ds(n2, 1), :], (n_sel, NSA_TQ))
        beats = (row > score) | ((row == score) & (n2 < n_iota))
        return rank + jnp.where(beats, 1.0, 0.0)

    n_live = jnp.minimum((s0 + NSA_TQ - 1) // SEL_BLK + 1, n_sel)
    rank = lax.fori_loop(0, n_live, rank_body, jnp.zeros((n_sel, NSA_TQ), F32))
    sel_ref[...] = jnp.where(valid & (rank < N_SELECT), 0.0, NEG_BIG)

    key_iota = lax.broadcasted_iota(jnp.int32, (NSA_TK, NSA_TQ), 0)
    t_q = s0 + lax.broadcasted_iota(jnp.int32, (1, NSA_TQ), 1)

    def attend(k_ref, vT_ref, c_lo, c_hi, bias_fn):
        def body(c, carry):
            m_i, l_i, acc = carry
            k = k_ref[0, 0, pl.ds(pl.multiple_of(c * NSA_TK, NSA_TK), NSA_TK), :]
            sk = _mxu(k, qn)
            bias = bias_fn(c)
            ms, ls, ps = [], [], []
            for h in range(HPG):
                sl = slice(h * NSA_TQ, (h + 1) * NSA_TQ)
                s_h = sk[:, sl] + bias
                m_h = jnp.maximum(m_i[:, sl], jnp.max(s_h, axis=0, keepdims=True))
                p_h = jnp.exp2(s_h - m_h)
                ms.append(m_h)
                ls.append(jnp.sum(p_h, axis=0, keepdims=True))
                ps.append(p_h.astype(BF16))
            m_new = jnp.concatenate(ms, axis=1)
            alpha = jnp.exp2(m_i - m_new)
            l_new = alpha * l_i + jnp.concatenate(ls, axis=1)
            acc = alpha * acc + _mxu(vT_ref[0, 0, c], jnp.concatenate(ps, axis=1))
            return m_new, l_new, acc

        init = (jnp.full((1, NSA_Q), NEG_BIG, F32), jnp.zeros((1, NSA_Q), F32),
                jnp.zeros((HEAD_DIM, NSA_Q), F32))
        _, l_f, acc_f = lax.fori_loop(c_lo, c_hi, body, init)
        return acc_f / jnp.maximum(l_f, 1e-30)

    def sel_bias(c):
        pos = c * NSA_TK + key_iota
        per_blk = [jnp.broadcast_to(
            sel_ref[pl.ds(jnp.minimum(c * (NSA_TK // SEL_BLK) + i, n_sel - 1), 1), :], (SEL_BLK, NSA_TQ))
            for i in range(NSA_TK // SEL_BLK)]
        return jnp.where(pos <= t_q, jnp.concatenate(per_blk, axis=0), NEG_BIG)

    def win_bias(c):
        pos = c * NSA_TK + key_iota
        return jnp.where((pos <= t_q) & (pos > t_q - WINDOW), 0.0, NEG_BIG)

    c_hi = (s0 + NSA_TQ + NSA_TK - 1) // NSA_TK
    o_s = attend(ks_ref, vsT_ref, 0, c_hi, sel_bias)
    o_w = attend(kw_ref, vwT_ref, jnp.maximum(s0 - (WINDOW - 1), 0) // NSA_TK, c_hi, win_bias)

    g = jax.nn.sigmoid(gate_ref[0, 0, 0])
    o_ref[0, 0, 0] = g[0:1] * o_c + g[1:2] * o_s + g[2:3] * o_w


def nsa_prompt(nq, ng, kv_new, g_q, g_kc, g_ks, g_kw, pe_ck, w_ck1, w_ck2, pe_cv, w_cv1, w_cv2):
    B, T = nq.shape[:2]
    G = N_KV
    nqb = T // NSA_TQ
    n_cmp = T // CMP_BLK
    n_sel = T // SEL_BLK
    nch = T // NSA_TK
    qT = nq.reshape(B, nqb, NSA_TQ, G, HPG, HEAD_DIM).transpose(0, 3, 1, 5, 4, 2).reshape(B, G, nqb, HEAD_DIM, NSA_Q)
    gT = ng.reshape(B, nqb, NSA_TQ, G, HPG, 3).transpose(0, 3, 1, 5, 4, 2).reshape(B, G, nqb, 3, NSA_Q)
    kv_g = kv_new.transpose(2, 0, 3, 1, 4)
    xk = kv_g[0].reshape(B * G * n_cmp, CMP_BLK * HEAD_DIM)
    xv = kv_g[1].reshape(B * G * n_cmp, CMP_BLK * HEAD_DIM)
    kc = compress_rows(xk, pe_ck, w_ck1, w_ck2, g_kc, normalize=True).reshape(B, G, n_cmp, HEAD_DIM)
    vc = compress_rows(xv, pe_cv, w_cv1, w_cv2, g_kc, normalize=False).reshape(B, G, n_cmp, HEAD_DIM)
    vcT = vc.transpose(0, 1, 3, 2)
    ks = rms_rows(kv_g[2].reshape(B * G * T, HEAD_DIM), g_ks, BF16).reshape(B, G, T, HEAD_DIM)
    kw = rms_rows(kv_g[4].reshape(B * G * T, HEAD_DIM), g_kw, BF16).reshape(B, G, T, HEAD_DIM)
    vsT = kv_g[3].astype(BF16).reshape(B, G, nch, NSA_TK, HEAD_DIM).transpose(0, 1, 2, 4, 3)
    vwT = kv_g[5].astype(BF16).reshape(B, G, nch, NSA_TK, HEAD_DIM).transpose(0, 1, 2, 4, 3)

    bg = lambda b, g, j: (b, g, 0, 0)
    bg5 = lambda b, g, j: (b, g, 0, 0, 0)
    bgj = lambda b, g, j: (b, g, j, 0, 0)
    oT = pl.pallas_call(
        functools.partial(_nsa_prompt_kernel, n_cmp=n_cmp, n_sel=n_sel),
        grid=(B, G, nqb),
        in_specs=[
            pl.BlockSpec((1, 1, 1, HEAD_DIM, NSA_Q), bgj),
            pl.BlockSpec((HEAD_DIM, 1), lambda b, g, j: (0, 0)),
            pl.BlockSpec((1, 1, 1, 3, NSA_Q), bgj),
            pl.BlockSpec((1, 1, n_cmp, HEAD_DIM), bg),
            pl.BlockSpec((1, 1, HEAD_DIM, n_cmp), bg),
            pl.BlockSpec((1, 1, T, HEAD_DIM), bg),
            pl.BlockSpec((1, 1, nch, HEAD_DIM, NSA_TK), bg5),
            pl.BlockSpec((1, 1, T, HEAD_DIM), bg),
            pl.BlockSpec((1, 1, nch, HEAD_DIM, NSA_TK), bg5),
        ],
        out_specs=pl.BlockSpec((1, 1, 1, HEAD_DIM, NSA_Q), bgj),
        out_shape=jax.ShapeDtypeStruct((B, G, nqb, HEAD_DIM, NSA_Q), F32),
        scratch_shapes=[pltpu.VMEM((n_sel, NSA_TQ), F32), pltpu.VMEM((n_sel, NSA_TQ), F32),
                        pltpu.VMEM((n_cmp, NSA_TQ), F32)],
        compiler_params=pltpu.CompilerParams(
            dimension_semantics=("parallel", "parallel", "arbitrary"), vmem_limit_bytes=VMEM_LIMIT),
        name="nsa_prompt",
    )(qT, g_q.reshape(HEAD_DIM, 1), gT, kc, vcT, ks, vsT, kw, vwT)
    return oT.reshape(B, G, nqb, HEAD_DIM, HPG, NSA_TQ).transpose(0, 2, 5, 1, 4, 3).reshape(B, T, N_QW)


def rms_norm(x, g):
    xf = x.astype(F32)
    y = xf * lax.rsqrt(jnp.mean(xf * xf, axis=-1, keepdims=True) + EPS)
    return (y * g.astype(F32)).astype(x.dtype)


def masked_softmax(s, mask):
    s = jnp.where(mask, s.astype(F32), -jnp.inf)
    m = jnp.max(s, axis=-1, keepdims=True)
    m = jnp.where(jnp.isfinite(m), m, 0.0)
    e = jnp.where(mask, jnp.exp(s - m), 0.0)
    return e / jnp.maximum(jnp.sum(e, axis=-1, keepdims=True), 1e-30)


def short_conv(u, prev, w):
    T = u.shape[1]
    full = jnp.concatenate([prev.astype(u.dtype), u], axis=1)
    out = full[:, 0:T] * w[0]
    for j in range(1, M_CONV):
        out = out + full[:, j:j + T] * w[j]
    return out, full[:, T:]


def mlstm_chunkwise(q, k, v, i_pre, logf, C0, n0, m0):
    B, T = q.shape[:2]
    L = M_CHUNK if T % M_CHUNK == 0 else T
    nc = T // L

    def chunks(a):
        a = a.reshape((B, nc, L) + a.shape[2:])
        return jnp.moveaxis(jnp.moveaxis(a, 3, 2), 1, 0)

    causal = jnp.tril(jnp.ones((L, L), bool))

    def step(carry, xs):
        C, n, m = carry
        qc, kc, vc, ic, fc = xs
        b = jnp.cumsum(fc, axis=-1)
        logD = jnp.where(causal, b[..., :, None] - b[..., None, :] + ic[..., None, :], -jnp.inf)
        inter = b + m[..., None]
        m_t = jnp.maximum(inter, jnp.max(logD, axis=-1))
        a = jnp.exp(inter - m_t)
        S = jnp.einsum('bhtd,bhsd->bhts', qc, kc) * jnp.exp(logD - m_t[..., None])
        num = a[..., None] * jnp.einsum('bhtd,bhde->bhte', qc, C) + jnp.einsum('bhts,bhse->bhte', S, vc)
        den = a * jnp.einsum('bhtd,bhd->bht', qc, n) + jnp.sum(S, axis=-1)
        h = num / jnp.maximum(jnp.abs(den), jnp.exp(-m_t))[..., None]
        m_new = m_t[..., -1]
        w = jnp.exp(b[..., -1:] - b + ic - m_new[..., None])
        aL = jnp.exp(b[..., -1] + m - m_new)
        C_new = aL[..., None, None] * C + jnp.einsum('bhs,bhsd,bhse->bhde', w, kc, vc)
        n_new = aL[..., None] * n + jnp.einsum('bhs,bhsd->bhd', w, kc)
        return (C_new, n_new, m_new), h

    (C, n, m), h = lax.scan(step, (C0, n0, m0), (chunks(q), chunks(k), chunks(v), chunks(i_pre), chunks(logf)))
    h = jnp.moveaxis(jnp.moveaxis(h, 0, 1), 2, 3).reshape(B, T, M_HEADS, M_DV)
    return h, C, n, m


def compress(rows, pe, w1, w2):
    B, Lk = rows.shape[:2]
    nc = Lk // CMP_BLK
    blk = rows[:, :nc * CMP_BLK].reshape(B, nc, CMP_BLK, N_KV, HEAD_DIM) + pe[:, None, :]
    hid = jax.nn.silu(jnp.einsum('bclgd,ldf->bcgf', blk, w1))
    return jnp.einsum('bcgf,fd->bcgd', hid, w2)


def nsa_attention(q, gates, kv_all, kw_full, q0, g_kc, g_ks, g_kw, pe_ck, w_ck1, w_ck2, pe_cv, w_cv1, w_cv2):
    B, T = q.shape[:2]
    Lk = kv_all.shape[1]
    hpg = N_HEADS // N_KV
    scale = HEAD_DIM ** -0.5
    kc = rms_norm(compress(kv_all[:, :, 0], pe_ck, w_ck1, w_ck2), g_kc)
    vc = compress(kv_all[:, :, 1], pe_cv, w_cv1, w_cv2)
    nc = kc.shape[1]
    n_sel = -(-Lk // SEL_BLK)
    pad = n_sel * SEL_BLK - Lk
    ks = jnp.pad(rms_norm(kv_all[:, :, 2], g_ks), ((0, 0), (0, pad), (0, 0), (0, 0)))
    ks = ks.reshape(B, n_sel, SEL_BLK, N_KV, HEAD_DIM).transpose(0, 3, 1, 2, 4)
    vs = jnp.pad(kv_all[:, :, 3], ((0, 0), (0, pad), (0, 0), (0, 0)))
    vs = vs.reshape(B, n_sel, SEL_BLK, N_KV, HEAD_DIM).transpose(0, 3, 1, 2, 4)
    kw = rms_norm(kw_full[:, :, 0], g_kw)
    vw = kw_full[:, :, 1]
    top = min(N_SELECT, n_sel)
    ratio = SEL_BLK // CMP_BLK
    qb = NSA_QBLK if T % NSA_QBLK == 0 else T
    nqb = T // qb
    qg = q.reshape(B, nqb, qb, N_KV, hpg, HEAD_DIM).transpose(1, 0, 3, 4, 2, 5)
    gg = jnp.moveaxis(gates.reshape(B, nqb, qb, N_HEADS, 3), 1, 0)
    cmp_end = (jnp.arange(nc) + 1) * CMP_BLK - 1
    blk_ids = jnp.arange(n_sel)
    b_ix = jnp.arange(B)[:, None, None, None]
    g_ix = jnp.arange(N_KV)[None, :, None, None]

    def block(args):
        qi, gi, j = args
        s0 = j * qb
        t = q0 + s0 + jnp.arange(qb)
        sc = jnp.einsum('bghqd,bcgd->bghqc', qi, kc) * scale
        p_c = masked_softmax(sc, cmp_end[None, :] <= t[:, None])
        o_c = jnp.einsum('bghqc,bcgd->bghqd', p_c.astype(vc.dtype), vc)
        imp = jnp.sum(p_c, axis=2)
        imp = jnp.pad(imp, ((0, 0), (0, 0), (0, 0), (0, n_sel * ratio - nc)))
        imp = imp.reshape(B, N_KV, qb, n_sel, ratio).sum(-1)
        cur = t // SEL_BLK
        valid = blk_ids[None, :] <= cur[:, None]
        forced = (blk_ids[None, :] == 0) | (blk_ids[None, :] == cur[:, None]) | (blk_ids[None, :] == cur[:, None] - 1)
        imp = jnp.where(valid, jnp.where(forced, FORCE_SCORE, imp), -jnp.inf)
        top_v, top_i = lax.top_k(imp, top)
        k_sel = ks[b_ix, g_ix, top_i]
        v_sel = vs[b_ix, g_ix, top_i]
        pos = top_i[..., None] * SEL_BLK + jnp.arange(SEL_BLK)
        m_s = (pos <= t[:, None, None]) & jnp.isfinite(top_v)[..., None]
        ss = (jnp.einsum('bghqd,bgqnkd->bghqnk', qi, k_sel) * scale).reshape(B, N_KV, hpg, qb, top * SEL_BLK)
        p_s = masked_softmax(ss, m_s.reshape(B, N_KV, 1, qb, top * SEL_BLK))
        o_s = jnp.einsum('bghqnk,bgqnkd->bghqd', p_s.reshape(B, N_KV, hpg, qb, top, SEL_BLK).astype(v_sel.dtype), v_sel)
        kwi = lax.dynamic_slice_in_dim(kw, s0, qb + WINDOW, axis=1)
        vwi = lax.dynamic_slice_in_dim(vw, s0, qb + WINDOW, axis=1)
        pw = q0 - WINDOW + s0 + jnp.arange(qb + WINDOW)
        m_w = (pw[None, :] <= t[:, None]) & (pw[None, :] > t[:, None] - WINDOW) & (pw[None, :] >= 0)
        sw = jnp.einsum('bghqd,bkgd->bghqk', qi, kwi) * scale
        p_w = masked_softmax(sw, m_w)
        o_w = jnp.einsum('bghqk,bkgd->bghqd', p_w.astype(vwi.dtype), vwi)
        o = jnp.stack([o_c, o_s, o_w], axis=-1)
        gi_r = gi.reshape(B, qb, N_KV, hpg, 3).transpose(0, 2, 3, 1, 4)
        return jnp.einsum('bghqdc,bghqc->bqghd', o, gi_r.astype(o.dtype))

    out = lax.map(block, (qg, gg, jnp.arange(nqb)))
    return jnp.moveaxis(out, 0, 1).reshape(B, T, N_QW)


def nsa_sample(q, gates, cache_kv, page_table, kv_rows, win_all, g_kc, g_ks, g_kw,
               pe_ck, w_ck1, w_ck2, pe_cv, w_cv1, w_cv2):
    B, T = q.shape[:2]
    n_pages = page_table.shape[1]
    page = cache_kv.shape[1]
    P = n_pages * page
    G, hd = N_KV, HEAD_DIM
    assert T < CMP_BLK and T <= SEL_BLK and P % SEL_BLK == 0 and page % SEL_BLK == 0
    assert win_all.shape[1] == WINDOW + T
    nc = P // CMP_BLK
    cpp = page // CMP_BLK
    n_last = P // SEL_BLK
    n_sel = n_last + 1
    top = min(N_SELECT, n_sel)
    ratio = SEL_BLK // CMP_BLK
    scale = hd ** -0.5
    t = P + jnp.arange(T)
    qg = q.reshape(B, T, G, HPG, hd)

    cache_t = cache_kv.transpose(0, 2, 3, 4, 1)
    flat_pages = page_table.reshape(-1)

    def cmp_tiles(kind):
        tiles = jax.vmap(lambda p: lax.dynamic_slice(cache_t, (p, kind, 0, 0, 0), (1, 1, G, hd, page))[0, 0])(flat_pages)
        return tiles.reshape(B * n_pages * G, hd, page)

    def cmp_rows(y):
        return y.reshape(B, n_pages, G, cpp, hd).transpose(0, 2, 1, 3, 4).reshape(B, G, nc, hd)

    kc = cmp_rows(compress_paged(cmp_tiles(0), pe_ck, w_ck1, w_ck2, g_kc, normalize=True))
    vc = cmp_rows(compress_paged(cmp_tiles(1), pe_cv, w_cv1, w_cv2, g_kc, normalize=False))
    cmp_end = (jnp.arange(nc) + 1) * CMP_BLK - 1
    sc = jnp.einsum('btghd,bgcd->bghtc', qg, kc) * scale
    p_c = masked_softmax(sc, cmp_end[None, :] <= t[:, None])
    o_c = jnp.einsum('bghtc,bgcd->bghtd', p_c, vc)

    imp = jnp.sum(p_c, axis=2)
    imp = jnp.pad(imp, ((0, 0), (0, 0), (0, 0), (0, n_sel * ratio - nc)))
    imp = imp.reshape(B, G, T, n_sel, ratio).sum(-1)
    blk_ids = jnp.arange(n_sel)
    cur = t // SEL_BLK
    valid = blk_ids[None, :] <= cur[:, None]
    forced = (blk_ids[None, :] == 0) | (blk_ids[None, :] == cur[:, None]) | (blk_ids[None, :] == cur[:, None] - 1)
    imp = jnp.where(valid, jnp.where(forced, FORCE_SCORE, imp), -jnp.inf)
    top_v, top_i = lax.top_k(imp, top)

    bpp = page // SEL_BLK
    blk_c = jnp.minimum(top_i, n_last - 1)
    b_ix = jnp.arange(B)[:, None, None, None]
    phys = page_table[b_ix, blk_c // bpp]
    g_ix = jnp.broadcast_to(jnp.arange(G)[None, :, None, None], top_i.shape)

    def take_tile(p, g):
        return lax.dynamic_slice(cache_t, (p, 2, g, 0, 0), (1, 2, 1, hd, page))[0, :, 0]

    tiles = jax.vmap(take_tile)(phys.reshape(-1), g_ix.reshape(-1)).reshape(B, G, T, top, 2, hd, page)
    new_tile = jnp.pad(kv_rows[:, :, 2:4], ((0, 0), (0, page - T), (0, 0), (0, 0), (0, 0)))
    new_tile = new_tile.transpose(0, 3, 2, 4, 1)[:, :, None, None]
    is_new = top_i == n_last
    tiles = jnp.where(is_new[..., None, None, None], new_tile, tiles)
    k_t = tiles[..., 0, :, :]
    v_t = tiles[..., 1, :, :]
    k_t = k_t * lax.rsqrt(jnp.mean(k_t * k_t, axis=-2, keepdims=True) + EPS) * g_ks[:, None]
    lane = jnp.arange(page)
    tile0 = jnp.where(is_new, n_last, blk_c // bpp * bpp)
    pos = tile0[..., None] * SEL_BLK + lane
    m_s = ((pos // SEL_BLK == top_i[..., None]) & (pos <= t[:, None, None])
           & jnp.isfinite(top_v)[..., None])
    ss = (jnp.einsum('btghd,bgtndp->bghtnp', qg, k_t) * scale).reshape(B, G, HPG, T, top * page)
    p_s = masked_softmax(ss, m_s.reshape(B, G, 1, T, top * page))
    o_s = jnp.einsum('bghtnp,bgtndp->bghtd', p_s.reshape(B, G, HPG, T, top, page), v_t)

    kw = rms_norm(win_all[:, :, 0], g_kw)
    vw = win_all[:, :, 1]
    pw = P - WINDOW + jnp.arange(T + WINDOW)
    m_w = (pw[None, :] <= t[:, None]) & (pw[None, :] > t[:, None] - WINDOW) & (pw[None, :] >= 0)
    sw = jnp.einsum('btghd,bkgd->bghtk', qg, kw) * scale
    p_w = masked_softmax(sw, m_w)
    o_w = jnp.einsum('bghtk,bkgd->bghtd', p_w, vw)

    o = jnp.stack([o_c, o_s, o_w], axis=-1)
    return jnp.einsum('bghtdc,btghc->btghd', o, gates.reshape(B, T, G, HPG, 3)).reshape(B, T, N_QW)


def moe_route(x, w_group, b_group, w_expert, b_expert):
    N = x.shape[0]
    pg = jax.nn.softmax((x @ w_group + b_group).astype(F32), axis=-1)
    grp = jnp.argmax(pg, axis=-1)
    p_grp = jnp.max(pg, axis=-1)
    le = (x @ w_expert + b_expert).astype(F32).reshape(N, N_GROUPS, EXPERTS_PER_GROUP)
    pe = jax.nn.softmax(le[jnp.arange(N), grp], axis=-1)
    top_p, top_i = lax.top_k(pe, TOP_K_FINE)
    wts = p_grp[:, None] * top_p / jnp.sum(top_p, axis=-1, keepdims=True)
    eid = grp[:, None] * EXPERTS_PER_GROUP + top_i
    return eid.astype(jnp.int32), wts


MOE_TM = 256


def _moe_ffn_kernel(be_ref, x_ref, w1_ref, w3_ref, w2_ref, wt_ref, o_ref, w1b_ref, w3b_ref, w2b_ref):
    i = pl.program_id(0)

    @pl.when((i == 0) | (be_ref[i] != be_ref[jnp.maximum(i - 1, 0)]))
    def _():
        w1b_ref[...] = w1_ref[0].astype(BF16)
        w3b_ref[...] = w3_ref[0].astype(BF16)
        w2b_ref[...] = w2_ref[0].astype(BF16)

    x = x_ref[...]
    a = _mxu(x, w1b_ref[...])
    b = _mxu(x, w3b_ref[...])
    h = (a * jax.nn.sigmoid(a) * b).astype(BF16)
    o_ref[...] = _mxu(h, w2b_ref[...]) * wt_ref[...]


def moe_ffn(rows, row_wt, blk_e, w1, w3, w2):
    m_pad, d = rows.shape
    n_blk = m_pad // MOE_TM
    f = w1.shape[2]
    return pl.pallas_call(
        _moe_ffn_kernel,
        grid_spec=pltpu.PrefetchScalarGridSpec(
            num_scalar_prefetch=1,
            grid=(n_blk,),
            in_specs=[
                pl.BlockSpec((MOE_TM, d), lambda i, be: (i, 0)),
                pl.BlockSpec((1, d, f), lambda i, be: (be[i], 0, 0)),
                pl.BlockSpec((1, d, f), lambda i, be: (be[i], 0, 0)),
                pl.BlockSpec((1, f, d), lambda i, be: (be[i], 0, 0)),
                pl.BlockSpec((MOE_TM, 1), lambda i, be: (i, 0)),
            ],
            out_specs=pl.BlockSpec((MOE_TM, d), lambda i, be: (i, 0)),
            scratch_shapes=[pltpu.VMEM((d, f), BF16), pltpu.VMEM((d, f), BF16), pltpu.VMEM((f, d), BF16)],
        ),
        out_shape=jax.ShapeDtypeStruct((m_pad, d), F32),
        compiler_params=pltpu.CompilerParams(
            dimension_semantics=("arbitrary",), vmem_limit_bytes=VMEM_LIMIT),
        name="moe_ffn",
    )(blk_e, rows, w1, w3, w2, row_wt)


def hier_moe_apply(x, eid, wts, w_e1, w_e3, w_e2):
    N, D = x.shape
    M = N * TOP_K_FINE
    eid_f = eid.reshape(-1)
    onehot = (eid_f[:, None] == jnp.arange(N_EXPERTS, dtype=jnp.int32)[None, :]).astype(jnp.int32)
    run = jnp.cumsum(onehot, axis=0)
    counts = run[-1]
    rank_in_e = jnp.sum(run * onehot, axis=1) - 1
    padded = (counts + MOE_TM - 1) // MOE_TM * MOE_TM
    p_end = jnp.cumsum(padded)
    p_start = p_end - padded
    pos = (p_start[eid_f] + rank_in_e).astype(jnp.int32)
    n_blk = (M + N_EXPERTS * (MOE_TM - 1) + MOE_TM - 1) // MOE_TM
    m_pad = n_blk * MOE_TM
    src_tok = jnp.zeros((m_pad,), jnp.int32).at[pos].set(jnp.arange(M, dtype=jnp.int32) // TOP_K_FINE)
    row_wt = jnp.zeros((m_pad,), F32).at[pos].set(wts.reshape(-1))
    blk_e = jnp.minimum(jnp.searchsorted(p_end, jnp.arange(n_blk, dtype=jnp.int32) * MOE_TM, side='right'),
                        N_EXPERTS - 1).astype(jnp.int32)
    rows = x.astype(BF16)[src_tok]
    out = moe_ffn(rows, row_wt.reshape(m_pad, 1), blk_e, w_e1, w_e3, w_e2)
    pos = pos.reshape(N, TOP_K_FINE)
    return out[pos[:, 0]] + out[pos[:, 1]]


def hybrid_layer(x, conv_prev, C0, n0, m0, paged, win_past, g_mix, w_in, w_conv, b_i, b_f, g_mh, g_q, g_kc, g_ks, g_kw,
                 pe_ck, w_ck1, w_ck2, pe_cv, w_cv1, w_cv2, w_proj_m, w_proj_n, w_out, g_ffn, w_group, b_group,
                 w_expert, b_expert, w_e1, w_e3, w_e2, *, use_pallas_proj):
    B, T, D = x.shape
    WB = win_past.shape[1]
    split_at = np.cumsum(SPLITS)[:-1].tolist()
    if use_pallas_proj:
        d_pad = -(-D_IN // 896) * 896
        w_pad = jnp.pad(w_in, ((0, 0), (0, d_pad - D_IN))).astype(BF16)
        proj = norm_proj(x.reshape(B * T, D), g_mix, w_pad)[:, :D_IN].reshape(B, T, D_IN)
    else:
        proj = rms_norm(x, g_mix) @ w_in
    mq, mk, mv, mi, mf, mo, nq, nkv, ng, gate_m, gate_n = jnp.split(proj, split_at, axis=-1)
    qk, conv_state = short_conv(jnp.concatenate([mq, mk], axis=-1), conv_prev, w_conv)
    qk = jax.nn.silu(qk)
    q_m = qk[..., :M_QK].reshape(B, T, M_HEADS, M_DQK).astype(F32)
    k_m = qk[..., M_QK:].reshape(B, T, M_HEADS, M_DQK).astype(F32) * (M_DQK ** -0.5)
    v_m = mv.reshape(B, T, M_HEADS, M_DV).astype(F32)
    i_pre = (mi + b_i).astype(F32)
    logf = jax.nn.log_sigmoid((mf + b_f).astype(F32))
    h_m, C1, n1, m1 = mlstm_chunkwise(q_m, k_m, v_m, i_pre, logf, C0.astype(F32), n0.astype(F32), m0.astype(F32))
    h_m = rms_norm(h_m, g_mh).reshape(B, T, M_VW).astype(x.dtype) * jax.nn.sigmoid(mo)
    qn = rms_norm(nq.reshape(B, T, N_HEADS, HEAD_DIM), g_q)
    kv_new = nkv.reshape(B, T, 6, N_KV, HEAD_DIM)
    kv_rows = kv_new[:, :, :4]
    win_all = jnp.concatenate([win_past.astype(x.dtype), kv_new[:, :, 4:]], axis=1)
    if paged is None:
        h_n = nsa_prompt(nq, ng, kv_new, g_q, g_kc, g_ks, g_kw, pe_ck, w_ck1, w_ck2, pe_cv, w_cv1, w_cv2)
    else:
        gates_n = jax.nn.sigmoid(ng).reshape(B, T, N_HEADS, 3)
        h_n = nsa_sample(qn, gates_n, paged[0], paged[1], kv_rows, win_all, g_kc, g_ks, g_kw,
                         pe_ck, w_ck1, w_ck2, pe_cv, w_cv1, w_cv2)
    win_state = win_all[:, -min(WINDOW, WB + T):]
    u = jax.nn.sigmoid(gate_m) * (h_m @ w_proj_m) + jax.nn.sigmoid(gate_n) * (h_n.astype(x.dtype) @ w_proj_n)
    x1 = x + u @ w_out
    h2 = rms_norm(x1, g_ffn).reshape(B * T, D)
    eid, wts = moe_route(h2, w_group, b_group, w_expert, b_expert)
    return x1, h2, eid, wts, kv_rows, win_state, C1, n1, m1, conv_state


def kernel(x_prompt, x_sample, cache_kv, page_table, state_win, state_C, state_n, state_m, state_conv, g_mix, w_in, w_conv, b_i, b_f, g_mh, g_q, g_kc, g_ks, g_kw, pe_ck, w_ck1, w_ck2, pe_cv, w_cv1, w_cv2, w_proj_m, w_proj_n, w_out, g_ffn, w_group, b_group, w_expert, b_expert, w_e1, w_e3, w_e2):
    B = x_prompt.shape[0]
    DB = x_sample.shape[0]
    n_pages = page_table.shape[1]
    dt = x_prompt.dtype
    weights = (g_mix, w_in, w_conv, b_i, b_f, g_mh, g_q, g_kc, g_ks, g_kw, pe_ck, w_ck1, w_ck2, pe_cv, w_cv1, w_cv2,
               w_proj_m, w_proj_n, w_out, g_ffn, w_group, b_group, w_expert, b_expert, w_e1, w_e3, w_e2)
    x1_p, h2_p, eid_p, wts_p, kv_p, win_p, C_p, n_p, m_p, conv_p = hybrid_layer(
        x_prompt, jnp.zeros((B, M_CONV - 1, 2 * M_QK), dt), jnp.zeros((B, M_HEADS, M_DQK, M_DV), F32),
        jnp.zeros((B, M_HEADS, M_DQK), F32), jnp.zeros((B, M_HEADS), F32),
        None, jnp.zeros((B, 0, 2, N_KV, HEAD_DIM), dt), *weights, use_pallas_proj=True)
    with jax.default_matmul_precision("highest"):
        x1_s, h2_s, eid_s, wts_s, kv_s, win_s, C_s, n_s, m_s, conv_s = hybrid_layer(
            x_sample, state_conv, state_C, state_n, state_m, (cache_kv, page_table), state_win, *weights,
            use_pallas_proj=False)
    n_p_tok = h2_p.shape[0]
    moe = hier_moe_apply(jnp.concatenate([h2_p, h2_s], axis=0), jnp.concatenate([eid_p, eid_s], axis=0),
                         jnp.concatenate([wts_p, wts_s], axis=0), w_e1, w_e3, w_e2)
    y_p = x1_p + moe[:n_p_tok].reshape(x1_p.shape)
    y_s = x1_s + moe[n_p_tok:].reshape(x1_s.shape)
    return (y_p, y_s, kv_p, kv_s, win_p, win_s, C_p, C_s, n_p, n_s, m_p, m_s, conv_p, conv_s)
```

```python
import functools
import math

import jax
import jax.numpy as jnp
import numpy as np
from jax import lax
from jax.experimental import pallas as pl
from jax.experimental.pallas import tpu as pltpu

D_MODEL = 2048
M_HEADS = 4
M_DQK = 128
M_DV = 256
M_CONV = 4
M_CHUNK = 64
N_HEADS = 16
N_KV = 4
HEAD_DIM = 64
CMP_BLK = 32
SEL_BLK = 64
N_SELECT = 16
WINDOW = 512
NSA_QBLK = 64
FORCE_SCORE = 1e4
N_GROUPS = 4
EXPERTS_PER_GROUP = 8
N_EXPERTS = N_GROUPS * EXPERTS_PER_GROUP
TOP_K_FINE = 2
MOE_BLK = 128
EPS = 1e-6
M_QK = M_HEADS * M_DQK
M_VW = M_HEADS * M_DV
N_QW = N_HEADS * HEAD_DIM
N_KVW = 6 * N_KV * HEAD_DIM
SPLITS = (M_QK, M_QK, M_VW, M_HEADS, M_HEADS, M_VW, N_QW, N_KVW, 3 * N_HEADS, D_MODEL, D_MODEL)
D_IN = sum(SPLITS)

F32 = jnp.float32
BF16 = jnp.bfloat16
VMEM_LIMIT = 48 * 1024 * 1024


def _mxu(a, b):
    return jnp.dot(a, b, preferred_element_type=F32, precision=lax.Precision.DEFAULT)


def _norm_proj_kernel(x_ref, g_ref, w_ref, o_ref, hx_ref):
    @pl.when(pl.program_id(1) == 0)
    def _():
        x = x_ref[...]
        r = lax.rsqrt(jnp.mean(x * x, axis=-1, keepdims=True) + EPS)
        hx_ref[...] = (x * r * g_ref[...]).astype(BF16)

    o_ref[...] = _mxu(hx_ref[...], w_ref[...])


def norm_proj(x, g, w_bf16, *, tm=1024, tn=896):
    n, d = x.shape
    k = w_bf16.shape[1]
    return pl.pallas_call(
        _norm_proj_kernel,
        grid=(n // tm, k // tn),
        in_specs=[
            pl.BlockSpec((tm, d), lambda i, j: (i, 0)),
            pl.BlockSpec((1, d), lambda i, j: (0, 0)),
            pl.BlockSpec((d, tn), lambda i, j: (0, j)),
        ],
        out_specs=pl.BlockSpec((tm, tn), lambda i, j: (i, j)),
        out_shape=jax.ShapeDtypeStruct((n, k), F32),
        scratch_shapes=[pltpu.VMEM((tm, d), BF16)],
        compiler_params=pltpu.CompilerParams(
            dimension_semantics=("parallel", "arbitrary"), vmem_limit_bytes=VMEM_LIMIT),
        name="norm_proj",
    )(x, g.reshape(1, d), w_bf16)


def _rms_rows_kernel(x_ref, g_ref, o_ref):
    x = x_ref[...]
    r = lax.rsqrt(jnp.mean(x * x, axis=-1, keepdims=True) + EPS)
    o_ref[...] = (x * r * g_ref[...]).astype(o_ref.dtype)


def rms_rows(x, g, out_dtype, *, tr=2048):
    rows, c = x.shape
    tr = min(tr, rows)
    return pl.pallas_call(
        _rms_rows_kernel,
        grid=(rows // tr,),
        in_specs=[pl.BlockSpec((tr, c), lambda i: (i, 0)), pl.BlockSpec((1, c), lambda i: (0, 0))],
        out_specs=pl.BlockSpec((tr, c), lambda i: (i, 0)),
        out_shape=jax.ShapeDtypeStruct((rows, c), out_dtype),
        compiler_params=pltpu.CompilerParams(dimension_semantics=("parallel",)),
        name="rms_rows",
    )(x, g.reshape(1, c))


def _split_bf16(x):
    hi = lax.bitcast_convert_type(
        lax.bitcast_convert_type(x, jnp.int32) & jnp.int32(-65536), F32)
    return hi.astype(BF16), (x - hi).astype(BF16)


def _dot_split(a, b_hi, b_lo):
    a_hi, a_lo = _split_bf16(a)
    return _mxu(a_hi, b_hi) + (_mxu(a_lo, b_hi) + _mxu(a_hi, b_lo))


def _compress_kernel(x_ref, pe_ref, w1_ref, w1l_ref, w2_ref, w2l_ref, g_ref, o_ref, *, normalize, precise):
    x = x_ref[...] + pe_ref[...]
    if precise:
        h = _dot_split(x, w1_ref[...], w1l_ref[...])
    else:
        h = _mxu(x.astype(BF16), w1_ref[...])
    h = h * jax.nn.sigmoid(h)
    if precise:
        y = _dot_split(h, w2_ref[...], w2l_ref[...])
    else:
        y = _mxu(h.astype(BF16), w2_ref[...])
    if normalize:
        y = y * lax.rsqrt(jnp.mean(y * y, axis=-1, keepdims=True) + EPS) * g_ref[...]
    o_ref[...] = y.astype(o_ref.dtype)


def compress_rows(x, pe, w1, w2, g, *, normalize, precise=False, tr=512):
    rows, k = x.shape
    f = w1.shape[-1]
    d = w2.shape[1]
    tr = min(tr, rows)
    w1_hi, w1_lo = _split_bf16(w1.reshape(k, f))
    w2_hi, w2_lo = _split_bf16(w2)
    if not precise:
        w1_hi, w2_hi = w1.reshape(k, f).astype(BF16), w2.astype(BF16)
    return pl.pallas_call(
        functools.partial(_compress_kernel, normalize=normalize, precise=precise),
        grid=(rows // tr,),
        in_specs=[
            pl.BlockSpec((tr, k), lambda i: (i, 0)),
            pl.BlockSpec((1, k), lambda i: (0, 0)),
            pl.BlockSpec((k, f), lambda i: (0, 0)),
            pl.BlockSpec((k, f), lambda i: (0, 0)),
            pl.BlockSpec((f, d), lambda i: (0, 0)),
            pl.BlockSpec((f, d), lambda i: (0, 0)),
            pl.BlockSpec((1, d), lambda i: (0, 0)),
        ],
        out_specs=pl.BlockSpec((tr, d), lambda i: (i, 0)),
        out_shape=jax.ShapeDtypeStruct((rows, d), F32 if precise else BF16),
        compiler_params=pltpu.CompilerParams(dimension_semantics=("parallel",)),
        name="nsa_compress",
    )(x, pe.reshape(1, k), w1_hi, w1_lo, w2_hi, w2_lo, g.reshape(1, d))


CMP_TILE_ROWS = 128


def _compress_paged_kernel(x_ref, pe_ref, w_hi_ref, w_lo_ref, w2_hi_ref, w2_lo_ref, g_ref, o_ref, *,
                           normalize, cpp, hidden):
    rows = x_ref.shape[0]
    hd = x_ref.shape[1]
    acc = jnp.zeros((rows, cpp * hidden), F32)
    for dp in range(hd // 2):
        a = x_ref[:, 2 * dp, :] + pe_ref[pl.ds(2 * dp, 1), :]
        b = x_ref[:, 2 * dp + 1, :] + pe_ref[pl.ds(2 * dp + 1, 1), :]
        acc = acc + _dot_split(jnp.concatenate([a, b], axis=1), w_hi_ref[dp], w_lo_ref[dp])
    hid = acc * jax.nn.sigmoid(acc)
    for c in range(cpp):
        y = _dot_split(hid[:, c * hidden:(c + 1) * hidden], w2_hi_ref[...], w2_lo_ref[...])
        if normalize:
            y = y * lax.rsqrt(jnp.mean(y * y, axis=-1, keepdims=True) + EPS) * g_ref[...]
        o_ref[:, c * hd:(c + 1) * hd] = y


def compress_paged(x_tiles, pe, w1, w2, g, *, normalize):
    r, hd, page = x_tiles.shape
    cpp = page // CMP_BLK
    hidden = w1.shape[-1]
    eye = jnp.eye(cpp, dtype=F32)
    w_big = jnp.einsum('cC,ldf->dclCf', eye, w1).reshape(hd // 2, 2 * page, cpp * hidden)
    w_hi, w_lo = _split_bf16(w_big)
    w2_hi, w2_lo = _split_bf16(w2)
    pe_t = jnp.tile(pe.T, (1, cpp))
    const3 = lambda i: (0, 0, 0)
    const2 = lambda i: (0, 0)
    return pl.pallas_call(
        functools.partial(_compress_paged_kernel, normalize=normalize, cpp=cpp, hidden=hidden),
        grid=(r // CMP_TILE_ROWS,),
        in_specs=[
            pl.BlockSpec((CMP_TILE_ROWS, hd, page), lambda i: (i, 0, 0)),
            pl.BlockSpec((hd, page), const2),
            pl.BlockSpec((hd // 2, 2 * page, cpp * hidden), const3, pipeline_mode=pl.Buffered(1)),
            pl.BlockSpec((hd // 2, 2 * page, cpp * hidden), const3, pipeline_mode=pl.Buffered(1)),
            pl.BlockSpec((hidden, hd), const2),
            pl.BlockSpec((hidden, hd), const2),
            pl.BlockSpec((1, hd), const2),
        ],
        out_specs=pl.BlockSpec((CMP_TILE_ROWS, cpp * hd), lambda i: (i, 0)),
        out_shape=jax.ShapeDtypeStruct((r, cpp * hd), F32),
        compiler_params=pltpu.CompilerParams(dimension_semantics=("parallel",), vmem_limit_bytes=VMEM_LIMIT),
        name="nsa_compress_paged",
    )(x_tiles, pe_t, w_hi, w_lo, w2_hi, w2_lo, g.reshape(1, hd))


NSA_TQ = 128
NSA_TK = 512
HPG = N_HEADS // N_KV
NSA_Q = HPG * NSA_TQ
NEG_BIG = -1e30


def _nsa_prompt_kernel(qT_ref, gq_ref, gate_ref, kc_ref, vcT_ref, ks_ref, vsT_ref, kw_ref, vwT_ref,
                       o_ref, sc_ref, sel_ref, imp_ref, *, n_cmp, n_sel):
    j = pl.program_id(2)
    s0 = j * NSA_TQ
    q = qT_ref[0, 0, 0]
    r = lax.rsqrt(jnp.mean(q * q, axis=0, keepdims=True) + EPS)
    qn = (q * r * gq_ref[...] * (HEAD_DIM ** -0.5 * math.log2(math.e))).astype(BF16)
    lane = lax.broadcasted_iota(jnp.int32, (1, NSA_Q), 1)
    t_row = s0 + (lane & (NSA_TQ - 1))

    s = _mxu(kc_ref[0, 0], qn)
    c_end = lax.broadcasted_iota(jnp.int32, (n_cmp, NSA_Q), 0) * CMP_BLK + (CMP_BLK - 1)
    mask_c = c_end <= t_row
    s = jnp.where(mask_c, s, -jnp.inf)
    m = jnp.max(s, axis=0, keepdims=True)
    m = jnp.where(m > -jnp.inf, m, 0.0)
    e = jnp.where(mask_c, jnp.exp2(s - m), 0.0)
    p_c = e / jnp.maximum(jnp.sum(e, axis=0, keepdims=True), 1e-30)
    o_c = _mxu(vcT_ref[0, 0], p_c.astype(BF16))

    imp_ref[...] = (p_c[:, 0:NSA_TQ] + p_c[:, NSA_TQ:2 * NSA_TQ]
                    + p_c[:, 2 * NSA_TQ:3 * NSA_TQ] + p_c[:, 3 * NSA_TQ:4 * NSA_TQ])
    imp = imp_ref[pl.ds(0, n_sel, stride=2), :] + imp_ref[pl.ds(1, n_sel, stride=2), :]
    n_iota = lax.broadcasted_iota(jnp.int32, (n_sel, NSA_TQ), 0)
    cur = (s0 + lax.broadcasted_iota(jnp.int32, (n_sel, NSA_TQ), 1)) // SEL_BLK
    valid = n_iota <= cur
    forced = (n_iota == 0) | (n_iota == cur) | (n_iota == cur - 1)
    score = jnp.where(valid, jnp.where(forced, FORCE_SCORE, imp), -jnp.inf)
    sc_ref[...] = score

    def rank_body(n2, rank):
        row = jnp.broadcast_to(sc_ref[pl.ds(n2, 1), :], (n_sel, NSA_TQ))
        beats = (row > score) | ((row == score) & (n2 < n_iota))
        return rank + jnp.where(beats, 1.0, 0.0)

    n_live = jnp.minimum((s0 + NSA_TQ - 1) // SEL_BLK + 1, n_sel)
    rank = lax.fori_loop(0, n_live, rank_body, jnp.zeros((n_sel, NSA_TQ), F32))
    sel_ref[...] = jnp.where(valid & (rank < N_SELECT), 0.0, NEG_BIG)

    key_iota = lax.broadcasted_iota(jnp.int32, (NSA_TK, NSA_TQ), 0)
    t_q = s0 + lax.broadcasted_iota(jnp.int32, (1, NSA_TQ), 1)

    def attend(k_ref, vT_ref, c_lo, c_hi, bias_fn):
        def body(c, carry):
            m_i, l_i, acc = carry
            k = k_ref[0, 0, pl.ds(pl.multiple_of(c * NSA_TK, NSA_TK), NSA_TK), :]
            sk = _mxu(k, qn)
            bias = bias_fn(c)
            ms, ls, ps = [], [], []
            for h in range(HPG):
                sl = slice(h * NSA_TQ, (h + 1) * NSA_TQ)
                s_h = sk[:, sl] + bias
                m_h = jnp.maximum(m_i[:, sl], jnp.max(s_h, axis=0, keepdims=True))
                p_h = jnp.exp2(s_h - m_h)
                ms.append(m_h)
                ls.append(jnp.sum(p_h, axis=0, keepdims=True))
                ps.append(p_h.astype(BF16))
            m_new = jnp.concatenate(ms, axis=1)
            alpha = jnp.exp2(m_i - m_new)
            l_new = alpha * l_i + jnp.concatenate(ls, axis=1)
            acc = alpha * acc + _mxu(vT_ref[0, 0, c], jnp.concatenate(ps, axis=1))
            return m_new, l_new, acc

        init = (jnp.full((1, NSA_Q), NEG_BIG, F32), jnp.zeros((1, NSA_Q), F32),
                jnp.zeros((HEAD_DIM, NSA_Q), F32))
        _, l_f, acc_f = lax.fori_loop(c_lo, c_hi, body, init)
        return acc_f / jnp.maximum(l_f, 1e-30)

    def sel_bias(c):
        pos = c * NSA_TK + key_iota
        per_blk = [jnp.broadcast_to(
            sel_ref[pl.ds(jnp.minimum(c * (NSA_TK // SEL_BLK) + i, n_sel - 1), 1), :], (SEL_BLK, NSA_TQ))
            for i in range(NSA_TK // SEL_BLK)]
        return jnp.where(pos <= t_q, jnp.concatenate(per_blk, axis=0), NEG_BIG)

    def win_bias(c):
        pos = c * NSA_TK + key_iota
        return jnp.where((pos <= t_q) & (pos > t_q - WINDOW), 0.0, NEG_BIG)

    c_hi = (s0 + NSA_TQ + NSA_TK - 1) // NSA_TK
    o_s = attend(ks_ref, vsT_ref, 0, c_hi, sel_bias)
    o_w = attend(kw_ref, vwT_ref, jnp.maximum(s0 - (WINDOW - 1), 0) // NSA_TK, c_hi, win_bias)

    g = jax.nn.sigmoid(gate_ref[0, 0, 0])
    o_ref[0, 0, 0] = g[0:1] * o_c + g[1:2] * o_s + g[2:3] * o_w


def nsa_prompt(nq, ng, kv_new, g_q, g_kc, g_ks, g_kw, pe_ck, w_ck1, w_ck2, pe_cv, w_cv1, w_cv2):
    B, T = nq.shape[:2]
    G = N_KV
    nqb = T // NSA_TQ
    n_cmp = T // CMP_BLK
    n_sel = T // SEL_BLK
    nch = T // NSA_TK
    qT = nq.reshape(B, nqb, NSA_TQ, G, HPG, HEAD_DIM).transpose(0, 3, 1, 5, 4, 2).reshape(B, G, nqb, HEAD_DIM, NSA_Q)
    gT = ng.reshape(B, nqb, NSA_TQ, G, HPG, 3).transpose(0, 3, 1, 5, 4, 2).reshape(B, G, nqb, 3, NSA_Q)
    kv_g = kv_new.transpose(2, 0, 3, 1, 4)
    xk = kv_g[0].reshape(B * G * n_cmp, CMP_BLK * HEAD_DIM)
    xv = kv_g[1].reshape(B * G * n_cmp, CMP_BLK * HEAD_DIM)
    kc = compress_rows(xk, pe_ck, w_ck1, w_ck2, g_kc, normalize=True).reshape(B, G, n_cmp, HEAD_DIM)
    vc = compress_rows(xv, pe_cv, w_cv1, w_cv2, g_kc, normalize=False).reshape(B, G, n_cmp, HEAD_DIM)
    vcT = vc.transpose(0, 1, 3, 2)
    ks = rms_rows(kv_g[2].reshape(B * G * T, HEAD_DIM), g_ks, BF16).reshape(B, G, T, HEAD_DIM)
    kw = rms_rows(kv_g[4].reshape(B * G * T, HEAD_DIM), g_kw, BF16).reshape(B, G, T, HEAD_DIM)
    vsT = kv_g[3].astype(BF16).reshape(B, G, nch, NSA_TK, HEAD_DIM).transpose(0, 1, 2, 4, 3)
    vwT = kv_g[5].astype(BF16).reshape(B, G, nch, NSA_TK, HEAD_DIM).transpose(0, 1, 2, 4, 3)

    bg = lambda b, g, j: (b, g, 0, 0)
    bg5 = lambda b, g, j: (b, g, 0, 0, 0)
    bgj = lambda b, g, j: (b, g, j, 0, 0)
    oT = pl.pallas_call(
        functools.partial(_nsa_prompt_kernel, n_cmp=n_cmp, n_sel=n_sel),
        grid=(B, G, nqb),
        in_specs=[
            pl.BlockSpec((1, 1, 1, HEAD_DIM, NSA_Q), bgj),
            pl.BlockSpec((HEAD_DIM, 1), lambda b, g, j: (0, 0)),
            pl.BlockSpec((1, 1, 1, 3, NSA_Q), bgj),
            pl.BlockSpec((1, 1, n_cmp, HEAD_DIM), bg),
            pl.BlockSpec((1, 1, HEAD_DIM, n_cmp), bg),
            pl.BlockSpec((1, 1, T, HEAD_DIM), bg),
            pl.BlockSpec((1, 1, nch, HEAD_DIM, NSA_TK), bg5),
            pl.BlockSpec((1, 1, T, HEAD_DIM), bg),
            pl.BlockSpec((1, 1, nch, HEAD_DIM, NSA_TK), bg5),
        ],
        out_specs=pl.BlockSpec((1, 1, 1, HEAD_DIM, NSA_Q), bgj),
        out_shape=jax.ShapeDtypeStruct((B, G, nqb, HEAD_DIM, NSA_Q), F32),
        scratch_shapes=[pltpu.VMEM((n_sel, NSA_TQ), F32), pltpu.VMEM((n_sel, NSA_TQ), F32),
                        pltpu.VMEM((n_cmp, NSA_TQ), F32)],
        compiler_params=pltpu.CompilerParams(
            dimension_semantics=("parallel", "parallel", "arbitrary"), vmem_limit_bytes=VMEM_LIMIT),
        name="nsa_prompt",
    )(qT, g_q.reshape(HEAD_DIM, 1), gT, kc, vcT, ks, vsT, kw, vwT)
    return oT.reshape(B, G, nqb, HEAD_DIM, HPG, NSA_TQ).transpose(0, 2, 5, 1, 4, 3).reshape(B, T, N_QW)


def rms_norm(x, g):
    xf = x.astype(F32)
    y = xf * lax.rsqrt(jnp.mean(xf * xf, axis=-1, keepdims=True) + EPS)
    return (y * g.astype(F32)).astype(x.dtype)


def masked_softmax(s, mask):
    s = jnp.where(mask, s.astype(F32), -jnp.inf)
    m = jnp.max(s, axis=-1, keepdims=True)
    m = jnp.where(jnp.isfinite(m), m, 0.0)
    e = jnp.where(mask, jnp.exp(s - m), 0.0)
    return e / jnp.maximum(jnp.sum(e, axis=-1, keepdims=True), 1e-30)


def short_conv(u, prev, w):
    T = u.shape[1]
    full = jnp.concatenate([prev.astype(u.dtype), u], axis=1)
    out = full[:, 0:T] * w[0]
    for j in range(1, M_CONV):
        out = out + full[:, j:j + T] * w[j]
    return out, full[:, T:]


def mlstm_chunkwise(q, k, v, i_pre, logf, C0, n0, m0):
    B, T = q.shape[:2]
    L = M_CHUNK if T % M_CHUNK == 0 else T
    nc = T // L

    def chunks(a):
        a = a.reshape((B, nc, L) + a.shape[2:])
        return jnp.moveaxis(jnp.moveaxis(a, 3, 2), 1, 0)

    causal = jnp.tril(jnp.ones((L, L), bool))

    def step(carry, xs):
        C, n, m = carry
        qc, kc, vc, ic, fc = xs
        b = jnp.cumsum(fc, axis=-1)
        logD = jnp.where(causal, b[..., :, None] - b[..., None, :] + ic[..., None, :], -jnp.inf)
        inter = b + m[..., None]
        m_t = jnp.maximum(inter, jnp.max(logD, axis=-1))
        a = jnp.exp(inter - m_t)
        S = jnp.einsum('bhtd,bhsd->bhts', qc, kc) * jnp.exp(logD - m_t[..., None])
        num = a[..., None] * jnp.einsum('bhtd,bhde->bhte', qc, C) + jnp.einsum('bhts,bhse->bhte', S, vc)
        den = a * jnp.einsum('bhtd,bhd->bht', qc, n) + jnp.sum(S, axis=-1)
        h = num / jnp.maximum(jnp.abs(den), jnp.exp(-m_t))[..., None]
        m_new = m_t[..., -1]
        w = jnp.exp(b[..., -1:] - b + ic - m_new[..., None])
        aL = jnp.exp(b[..., -1] + m - m_new)
        C_new = aL[..., None, None] * C + jnp.einsum('bhs,bhsd,bhse->bhde', w, kc, vc)
        n_new = aL[..., None] * n + jnp.einsum('bhs,bhsd->bhd', w, kc)
        return (C_new, n_new, m_new), h

    (C, n, m), h = lax.scan(step, (C0, n0, m0), (chunks(q), chunks(k), chunks(v), chunks(i_pre), chunks(logf)))
    h = jnp.moveaxis(jnp.moveaxis(h, 0, 1), 2, 3).reshape(B, T, M_HEADS, M_DV)
    return h, C, n, m


def compress(rows, pe, w1, w2):
    B, Lk = rows.shape[:2]
    nc = Lk // CMP_BLK
    blk = rows[:, :nc * CMP_BLK].reshape(B, nc, CMP_BLK, N_KV, HEAD_DIM) + pe[:, None, :]
    hid = jax.nn.silu(jnp.einsum('bclgd,ldf->bcgf', blk, w1))
    return jnp.einsum('bcgf,fd->bcgd', hid, w2)


def nsa_attention(q, gates, kv_all, kw_full, q0, g_kc, g_ks, g_kw, pe_ck, w_ck1, w_ck2, pe_cv, w_cv1, w_cv2):
    B, T = q.shape[:2]
    Lk = kv_all.shape[1]
    hpg = N_HEADS // N_KV
    scale = HEAD_DIM ** -0.5
    kc = rms_norm(compress(kv_all[:, :, 0], pe_ck, w_ck1, w_ck2), g_kc)
    vc = compress(kv_all[:, :, 1], pe_cv, w_cv1, w_cv2)
    nc = kc.shape[1]
    n_sel = -(-Lk // SEL_BLK)
    pad = n_sel * SEL_BLK - Lk
    ks = jnp.pad(rms_norm(kv_all[:, :, 2], g_ks), ((0, 0), (0, pad), (0, 0), (0, 0)))
    ks = ks.reshape(B, n_sel, SEL_BLK, N_KV, HEAD_DIM).transpose(0, 3, 1, 2, 4)
    vs = jnp.pad(kv_all[:, :, 3], ((0, 0), (0, pad), (0, 0), (0, 0)))
    vs = vs.reshape(B, n_sel, SEL_BLK, N_KV, HEAD_DIM).transpose(0, 3, 1, 2, 4)
    kw = rms_norm(kw_full[:, :, 0], g_kw)
    vw = kw_full[:, :, 1]
    top = min(N_SELECT, n_sel)
    ratio = SEL_BLK // CMP_BLK
    qb = NSA_QBLK if T % NSA_QBLK == 0 else T
    nqb = T // qb
    qg = q.reshape(B, nqb, qb, N_KV, hpg, HEAD_DIM).transpose(1, 0, 3, 4, 2, 5)
    gg = jnp.moveaxis(gates.reshape(B, nqb, qb, N_HEADS, 3), 1, 0)
    cmp_end = (jnp.arange(nc) + 1) * CMP_BLK - 1
    blk_ids = jnp.arange(n_sel)
    b_ix = jnp.arange(B)[:, None, None, None]
    g_ix = jnp.arange(N_KV)[None, :, None, None]

    def block(args):
        qi, gi, j = args
        s0 = j * qb
        t = q0 + s0 + jnp.arange(qb)
        sc = jnp.einsum('bghqd,bcgd->bghqc', qi, kc) * scale
        p_c = masked_softmax(sc, cmp_end[None, :] <= t[:, None])
        o_c = jnp.einsum('bghqc,bcgd->bghqd', p_c.astype(vc.dtype), vc)
        imp = jnp.sum(p_c, axis=2)
        imp = jnp.pad(imp, ((0, 0), (0, 0), (0, 0), (0, n_sel * ratio - nc)))
        imp = imp.reshape(B, N_KV, qb, n_sel, ratio).sum(-1)
        cur = t // SEL_BLK
        valid = blk_ids[None, :] <= cur[:, None]
        forced = (blk_ids[None, :] == 0) | (blk_ids[None, :] == cur[:, None]) | (blk_ids[None, :] == cur[:, None] - 1)
        imp = jnp.where(valid, jnp.where(forced, FORCE_SCORE, imp), -jnp.inf)
        top_v, top_i = lax.top_k(imp, top)
        k_sel = ks[b_ix, g_ix, top_i]
        v_sel = vs[b_ix, g_ix, top_i]
        pos = top_i[..., None] * SEL_BLK + jnp.arange(SEL_BLK)
        m_s = (pos <= t[:, None, None]) & jnp.isfinite(top_v)[..., None]
        ss = (jnp.einsum('bghqd,bgqnkd->bghqnk', qi, k_sel) * scale).reshape(B, N_KV, hpg, qb, top * SEL_BLK)
        p_s = masked_softmax(ss, m_s.reshape(B, N_KV, 1, qb, top * SEL_BLK))
        o_s = jnp.einsum('bghqnk,bgqnkd->bghqd', p_s.reshape(B, N_KV, hpg, qb, top, SEL_BLK).astype(v_sel.dtype), v_sel)
        kwi = lax.dynamic_slice_in_dim(kw, s0, qb + WINDOW, axis=1)
        vwi = lax.dynamic_slice_in_dim(vw, s0, qb + WINDOW, axis=1)
        pw = q0 - WINDOW + s0 + jnp.arange(qb + WINDOW)
        m_w = (pw[None, :] <= t[:, None]) & (pw[None, :] > t[:, None] - WINDOW) & (pw[None, :] >= 0)
        sw = jnp.einsum('bghqd,bkgd->bghqk', qi, kwi) * scale
        p_w = masked_softmax(sw, m_w)
        o_w = jnp.einsum('bghqk,bkgd->bghqd', p_w.astype(vwi.dtype), vwi)
        o = jnp.stack([o_c, o_s, o_w], axis=-1)
        gi_r = gi.reshape(B, qb, N_KV, hpg, 3).transpose(0, 2, 3, 1, 4)
        return jnp.einsum('bghqdc,bghqc->bqghd', o, gi_r.astype(o.dtype))

    out = lax.map(block, (qg, gg, jnp.arange(nqb)))
    return jnp.moveaxis(out, 0, 1).reshape(B, T, N_QW)


def nsa_sample(q, gates, cache_kv, page_table, kv_rows, win_all, g_kc, g_ks, g_kw,
               pe_ck, w_ck1, w_ck2, pe_cv, w_cv1, w_cv2):
    B, T = q.shape[:2]
    n_pages = page_table.shape[1]
    page = cache_kv.shape[1]
    P = n_pages * page
    G, hd = N_KV, HEAD_DIM
    assert T < CMP_BLK and T <= SEL_BLK and P % SEL_BLK == 0 and page % SEL_BLK == 0
    assert win_all.shape[1] == WINDOW + T
    nc = P // CMP_BLK
    cpp = page // CMP_BLK
    n_last = P // SEL_BLK
    n_sel = n_last + 1
    top = min(N_SELECT, n_sel)
    ratio = SEL_BLK // CMP_BLK
    scale = hd ** -0.5
    t = P + jnp.arange(T)
    qg = q.reshape(B, T, G, HPG, hd)

    cache_t = cache_kv.transpose(0, 2, 3, 4, 1)
    flat_pages = page_table.reshape(-1)

    def cmp_tiles(kind):
        tiles = jax.vmap(lambda p: lax.dynamic_slice(cache_t, (p, kind, 0, 0, 0), (1, 1, G, hd, page))[0, 0])(flat_pages)
        return tiles.reshape(B * n_pages * G, hd, page)

    def cmp_rows(y):
        return y.reshape(B, n_pages, G, cpp, hd).transpose(0, 2, 1, 3, 4).reshape(B, G, nc, hd)

    kc = cmp_rows(compress_paged(cmp_tiles(0), pe_ck, w_ck1, w_ck2, g_kc, normalize=True))
    vc = cmp_rows(compress_paged(cmp_tiles(1), pe_cv, w_cv1, w_cv2, g_kc, normalize=False))
    cmp_end = (jnp.arange(nc) + 1) * CMP_BLK - 1
    sc = jnp.einsum('btghd,bgcd->bghtc', qg, kc) * scale
    p_c = masked_softmax(sc, cmp_end[None, :] <= t[:, None])
    o_c = jnp.einsum('bghtc,bgcd->bghtd', p_c, vc)

    imp = jnp.sum(p_c, axis=2)
    imp = jnp.pad(imp, ((0, 0), (0, 0), (0, 0), (0, n_sel * ratio - nc)))
    imp = imp.reshape(B, G, T, n_sel, ratio).sum(-1)
    blk_ids = jnp.arange(n_sel)
    cur = t // SEL_BLK
    valid = blk_ids[None, :] <= cur[:, None]
    forced = (blk_ids[None, :] == 0) | (blk_ids[None, :] == cur[:, None]) | (blk_ids[None, :] == cur[:, None] - 1)
    imp = jnp.where(valid, jnp.where(forced, FORCE_SCORE, imp), -jnp.inf)
    top_v, top_i = lax.top_k(imp, top)

    bpp = page // SEL_BLK
    blk_c = jnp.minimum(top_i, n_last - 1)
    b_ix = jnp.arange(B)[:, None, None, None]
    phys = page_table[b_ix, blk_c // bpp]
    g_ix = jnp.broadcast_to(jnp.arange(G)[None, :, None, None], top_i.shape)

    def take_tile(p, g):
        return lax.dynamic_slice(cache_t, (p, 2, g, 0, 0), (1, 2, 1, hd, page))[0, :, 0]

    tiles = jax.vmap(take_tile)(phys.reshape(-1), g_ix.reshape(-1)).reshape(B, G, T, top, 2, hd, page)
    new_tile = jnp.pad(kv_rows[:, :, 2:4], ((0, 0), (0, page - T), (0, 0), (0, 0), (0, 0)))
    new_tile = new_tile.transpose(0, 3, 2, 4, 1)[:, :, None, None]
    is_new = top_i == n_last
    tiles = jnp.where(is_new[..., None, None, None], new_tile, tiles)
    k_t = tiles[..., 0, :, :]
    v_t = tiles[..., 1, :, :]
    k_t = k_t * lax.rsqrt(jnp.mean(k_t * k_t, axis=-2, keepdims=True) + EPS) * g_ks[:, None]
    lane = jnp.arange(page)
    tile0 = jnp.where(is_new, n_last, blk_c // bpp * bpp)
    pos = tile0[..., None] * SEL_BLK + lane
    m_s = ((pos // SEL_BLK == top_i[..., None]) & (pos <= t[:, None, None])
           & jnp.isfinite(top_v)[..., None])
    ss = (jnp.einsum('btghd,bgtndp->bghtnp', qg, k_t) * scale).reshape(B, G, HPG, T, top * page)
    p_s = masked_softmax(ss, m_s.reshape(B, G, 1, T, top * page))
    o_s = jnp.einsum('bghtnp,bgtndp->bghtd', p_s.reshape(B, G, HPG, T, top, page), v_t)

    kw = rms_norm(win_all[:, :, 0], g_kw)
    vw = win_all[:, :, 1]
    pw = P - WINDOW + jnp.arange(T + WINDOW)
    m_w = (pw[None, :] <= t[:, None]) & (pw[None, :] > t[:, None] - WINDOW) & (pw[None, :] >= 0)
    sw = jnp.einsum('btghd,bkgd->bghtk', qg, kw) * scale
    p_w = masked_softmax(sw, m_w)
    o_w = jnp.einsum('bghtk,bkgd->bghtd', p_w, vw)

    o = jnp.stack([o_c, o_s, o_w], axis=-1)
    return jnp.einsum('bghtdc,btghc->btghd', o, gates.reshape(B, T, G, HPG, 3)).reshape(B, T, N_QW)


def moe_route(group_logits, expert_logits):
    assert TOP_K_FINE == 2
    N = group_logits.shape[0]
    pg = jax.nn.softmax(group_logits.astype(F32), axis=-1)
    grp = jnp.argmax(pg, axis=-1)
    p_grp = jnp.max(pg, axis=-1)
    le = expert_logits.astype(F32).reshape(N, N_GROUPS, EXPERTS_PER_GROUP)
    pe = jax.nn.softmax(le[jnp.arange(N), grp], axis=-1)
    i1 = jnp.argmax(pe, axis=-1)
    hit1 = jnp.arange(EXPERTS_PER_GROUP)[None, :] == i1[:, None]
    rest = jnp.where(hit1, -jnp.inf, pe)
    i2 = jnp.argmax(rest, axis=-1)
    top_i = jnp.stack([i1, i2], axis=-1)
    top_p = jnp.stack([jnp.max(pe, axis=-1), jnp.max(rest, axis=-1)], axis=-1)
    wts = p_grp[:, None] * top_p / jnp.sum(top_p, axis=-1, keepdims=True)
    eid = grp[:, None] * EXPERTS_PER_GROUP + top_i
    return eid.astype(jnp.int32), wts


MOE_TM = 256


def _moe_ffn_kernel(be_ref, x_ref, w1_ref, w3_ref, w2_ref, wt_ref, o_ref, w1b_ref, w3b_ref, w2b_ref):
    i = pl.program_id(0)

    @pl.when((i == 0) | (be_ref[i] != be_ref[jnp.maximum(i - 1, 0)]))
    def _():
        w1b_ref[...] = w1_ref[0].astype(BF16)
        w3b_ref[...] = w3_ref[0].astype(BF16)
        w2b_ref[...] = w2_ref[0].astype(BF16)

    x = x_ref[...]
    a = _mxu(x, w1b_ref[...])
    b = _mxu(x, w3b_ref[...])
    h = (a * jax.nn.sigmoid(a) * b).astype(BF16)
    o_ref[...] = _mxu(h, w2b_ref[...]) * wt_ref[...]


def moe_ffn(rows, row_wt, blk_e, w1, w3, w2):
    m_pad, d = rows.shape
    n_blk = m_pad // MOE_TM
    f = w1.shape[2]
    return pl.pallas_call(
        _moe_ffn_kernel,
        grid_spec=pltpu.PrefetchScalarGridSpec(
            num_scalar_prefetch=1,
            grid=(n_blk,),
            in_specs=[
                pl.BlockSpec((MOE_TM, d), lambda i, be: (i, 0)),
                pl.BlockSpec((1, d, f), lambda i, be: (be[i], 0, 0)),
                pl.BlockSpec((1, d, f), lambda i, be: (be[i], 0, 0)),
                pl.BlockSpec((1, f, d), lambda i, be: (be[i], 0, 0)),
                pl.BlockSpec((MOE_TM, 1), lambda i, be: (i, 0)),
            ],
            out_specs=pl.BlockSpec((MOE_TM, d), lambda i, be: (i, 0)),
            scratch_shapes=[pltpu.VMEM((d, f), BF16), pltpu.VMEM((d, f), BF16), pltpu.VMEM((f, d), BF16)],
        ),
        out_shape=jax.ShapeDtypeStruct((m_pad, d), F32),
        compiler_params=pltpu.CompilerParams(
            dimension_semantics=("arbitrary",), vmem_limit_bytes=VMEM_LIMIT),
        name="moe_ffn",
    )(blk_e, rows, w1, w3, w2, row_wt)


def hier_moe_apply(x, eid, wts, w_e1, w_e3, w_e2):
    N, D = x.shape
    M = N * TOP_K_FINE
    eid_f = eid.reshape(-1)
    onehot = (eid_f[:, None] == jnp.arange(N_EXPERTS, dtype=jnp.int32)[None, :]).astype(jnp.int32)
    run = jnp.cumsum(onehot, axis=0)
    counts = run[-1]
    rank_in_e = jnp.sum(run * onehot, axis=1) - 1
    padded = (counts + MOE_TM - 1) // MOE_TM * MOE_TM
    p_end = jnp.cumsum(padded)
    p_start = p_end - padded
    pos = (p_start[eid_f] + rank_in_e).astype(jnp.int32)
    n_blk = (M + N_EXPERTS * (MOE_TM - 1) + MOE_TM - 1) // MOE_TM
    m_pad = n_blk * MOE_TM
    src_tok = jnp.zeros((m_pad,), jnp.int32).at[pos].set(jnp.arange(M, dtype=jnp.int32) // TOP_K_FINE)
    row_wt = jnp.zeros((m_pad,), F32).at[pos].set(wts.reshape(-1))
    blk_e = jnp.minimum(jnp.searchsorted(p_end, jnp.arange(n_blk, dtype=jnp.int32) * MOE_TM, side='right'),
                        N_EXPERTS - 1).astype(jnp.int32)
    rows = x.astype(BF16)[src_tok]
    out = moe_ffn(rows, row_wt.reshape(m_pad, 1), blk_e, w_e1, w_e3, w_e2)
    pos = pos.reshape(N, TOP_K_FINE)
    return out[pos[:, 0]] + out[pos[:, 1]]


OUT_TM = 256
ROUTER_PAD = 128


def _out_proj_kernel(x_ref, gm_ref, gn_ref, hm_ref, hn_ref, wm_ref, wn_ref, wo_ref, gf_ref, wr_ref, br_ref,
                     x1_ref, h2_ref, lg_ref):
    pm = _mxu(hm_ref[...].astype(BF16), wm_ref[...])
    pn = _mxu(hn_ref[...].astype(BF16), wn_ref[...])
    u = jax.nn.sigmoid(gm_ref[...]) * pm + jax.nn.sigmoid(gn_ref[...]) * pn
    x1 = x_ref[...] + _mxu(u.astype(BF16), wo_ref[...])
    x1_ref[...] = x1
    h2 = (x1 * lax.rsqrt(jnp.mean(x1 * x1, axis=-1, keepdims=True) + EPS) * gf_ref[...]).astype(BF16)
    h2_ref[...] = h2
    lg_ref[...] = _mxu(h2, wr_ref[...]) + br_ref[...]


def out_proj(x, proj, gate_col_blocks, h_m, h_n, w_proj_m, w_proj_n, w_out, g_ffn, w_group, b_group, w_expert, b_expert):
    n, d = x.shape
    n_r = N_GROUPS + N_EXPERTS
    w_r = jnp.pad(jnp.concatenate([w_group, w_expert], axis=1), ((0, 0), (0, ROUTER_PAD - n_r))).astype(BF16)
    b_r = jnp.pad(jnp.concatenate([b_group, b_expert]), (0, ROUTER_PAD - n_r)).reshape(1, ROUTER_PAD)
    cm, cn = gate_col_blocks
    row = lambda i: (i, 0)
    const = lambda i: (0, 0)
    once = pl.Buffered(1)
    return pl.pallas_call(
        _out_proj_kernel,
        grid=(n // OUT_TM,),
        in_specs=[
            pl.BlockSpec((OUT_TM, d), row),
            pl.BlockSpec((OUT_TM, d), lambda i: (i, cm)),
            pl.BlockSpec((OUT_TM, d), lambda i: (i, cn)),
            pl.BlockSpec((OUT_TM, h_m.shape[1]), row),
            pl.BlockSpec((OUT_TM, h_n.shape[1]), row),
            pl.BlockSpec(w_proj_m.shape, const, pipeline_mode=once),
            pl.BlockSpec(w_proj_n.shape, const, pipeline_mode=once),
            pl.BlockSpec(w_out.shape, const, pipeline_mode=once),
            pl.BlockSpec((1, d), const),
            pl.BlockSpec((d, ROUTER_PAD), const),
            pl.BlockSpec((1, ROUTER_PAD), const),
        ],
        out_specs=[pl.BlockSpec((OUT_TM, d), row), pl.BlockSpec((OUT_TM, d), row),
                   pl.BlockSpec((OUT_TM, ROUTER_PAD), row)],
        out_shape=[jax.ShapeDtypeStruct((n, d), F32), jax.ShapeDtypeStruct((n, d), BF16),
                   jax.ShapeDtypeStruct((n, ROUTER_PAD), F32)],
        compiler_params=pltpu.CompilerParams(dimension_semantics=("parallel",), vmem_limit_bytes=VMEM_LIMIT),
        name="out_proj",
    )(x, proj, proj, h_m, h_n, w_proj_m.astype(BF16), w_proj_n.astype(BF16), w_out.astype(BF16),
      g_ffn.reshape(1, d), w_r, b_r)


_SEG = dict(zip(("mq", "mk", "mv", "mi", "mf", "mo", "nq", "nkv", "ng", "gate_m", "gate_n"),
                zip(np.cumsum((0,) + SPLITS[:-1]).tolist(), SPLITS)))
_PROJ_ORDER = ("gate_m", "gate_n", "mv", "mo", "nq", "mq", "mk", "nkv", "mi", "mf", "ng")
_PROJ_PERM = np.concatenate([np.arange(_SEG[k][0], _SEG[k][0] + _SEG[k][1]) for k in _PROJ_ORDER])
_PROJ_OFF = dict(zip(_PROJ_ORDER, np.cumsum([0] + [_SEG[k][1] for k in _PROJ_ORDER[:-1]]).tolist()))
D_IN_PAD = -(-D_IN // 896) * 896


def hybrid_layer(x, conv_prev, C0, n0, m0, paged, win_past, g_mix, w_in, w_conv, b_i, b_f, g_mh, g_q, g_kc, g_ks, g_kw,
                 pe_ck, w_ck1, w_ck2, pe_cv, w_cv1, w_cv2, w_proj_m, w_proj_n, w_out, g_ffn, w_group, b_group,
                 w_expert, b_expert, w_e1, w_e3, w_e2, *, use_pallas_proj):
    B, T, D = x.shape
    WB = win_past.shape[1]
    split_at = np.cumsum(SPLITS)[:-1].tolist()
    if use_pallas_proj:
        w_pad = jnp.pad(w_in[:, _PROJ_PERM], ((0, 0), (0, D_IN_PAD - D_IN))).astype(BF16)
        proj2d = norm_proj(x.reshape(B * T, D), g_mix, w_pad)
        seg = lambda k: proj2d[:, _PROJ_OFF[k]:_PROJ_OFF[k] + _SEG[k][1]].reshape(B, T, _SEG[k][1])
        mq, mk, mv, mi, mf, mo, nq, nkv, ng = (seg(k) for k in ("mq", "mk", "mv", "mi", "mf", "mo", "nq", "nkv", "ng"))
    else:
        proj = rms_norm(x, g_mix) @ w_in
        mq, mk, mv, mi, mf, mo, nq, nkv, ng, gate_m, gate_n = jnp.split(proj, split_at, axis=-1)
    qk, conv_state = short_conv(jnp.concatenate([mq, mk], axis=-1), conv_prev, w_conv)
    qk = jax.nn.silu(qk)
    q_m = qk[..., :M_QK].reshape(B, T, M_HEADS, M_DQK).astype(F32)
    k_m = qk[..., M_QK:].reshape(B, T, M_HEADS, M_DQK).astype(F32) * (M_DQK ** -0.5)
    v_m = mv.reshape(B, T, M_HEADS, M_DV).astype(F32)
    i_pre = (mi + b_i).astype(F32)
    logf = jax.nn.log_sigmoid((mf + b_f).astype(F32))
    h_m, C1, n1, m1 = mlstm_chunkwise(q_m, k_m, v_m, i_pre, logf, C0.astype(F32), n0.astype(F32), m0.astype(F32))
    h_m = rms_norm(h_m, g_mh).reshape(B, T, M_VW).astype(x.dtype) * jax.nn.sigmoid(mo)
    qn = rms_norm(nq.reshape(B, T, N_HEADS, HEAD_DIM), g_q)
    kv_new = nkv.reshape(B, T, 6, N_KV, HEAD_DIM)
    kv_rows = kv_new[:, :, :4]
    win_all = jnp.concatenate([win_past.astype(x.dtype), kv_new[:, :, 4:]], axis=1)
    if paged is None:
        h_n = nsa_prompt(nq, ng, kv_new, g_q, g_kc, g_ks, g_kw, pe_ck, w_ck1, w_ck2, pe_cv, w_cv1, w_cv2)
    else:
        gates_n = jax.nn.sigmoid(ng).reshape(B, T, N_HEADS, 3)
        h_n = nsa_sample(qn, gates_n, paged[0], paged[1], kv_rows, win_all, g_kc, g_ks, g_kw,
                         pe_ck, w_ck1, w_ck2, pe_cv, w_cv1, w_cv2)
    win_state = win_all[:, -min(WINDOW, WB + T):]
    if use_pallas_proj:
        assert _PROJ_OFF["gate_m"] == 0 and _PROJ_OFF["gate_n"] == D
        x1, h2, logits = out_proj(x.reshape(B * T, D), proj2d, (0, 1), h_m.reshape(B * T, M_VW),
                                  h_n.reshape(B * T, N_QW), w_proj_m, w_proj_n, w_out, g_ffn,
                                  w_group, b_group, w_expert, b_expert)
        x1 = x1.reshape(B, T, D)
        lg, le = logits[:, :N_GROUPS], logits[:, N_GROUPS:N_GROUPS + N_EXPERTS]
    else:
        u = jax.nn.sigmoid(gate_m) * (h_m @ w_proj_m) + jax.nn.sigmoid(gate_n) * (h_n.astype(x.dtype) @ w_proj_n)
        x1 = x + u @ w_out
        h2 = rms_norm(x1, g_ffn).reshape(B * T, D)
        lg, le = h2 @ w_group + b_group, h2 @ w_expert + b_expert
    eid, wts = moe_route(lg, le)
    return x1, h2.astype(BF16), eid, wts, kv_rows, win_state, C1, n1, m1, conv_state


def kernel(x_prompt, x_sample, cache_kv, page_table, state_win, state_C, state_n, state_m, state_conv, g_mix, w_in, w_conv, b_i, b_f, g_mh, g_q, g_kc, g_ks, g_kw, pe_ck, w_ck1, w_ck2, pe_cv, w_cv1, w_cv2, w_proj_m, w_proj_n, w_out, g_ffn, w_group, b_group, w_expert, b_expert, w_e1, w_e3, w_e2):
    B = x_prompt.shape[0]
    DB = x_sample.shape[0]
    n_pages = page_table.shape[1]
    dt = x_prompt.dtype
    weights = (g_mix, w_in, w_conv, b_i, b_f, g_mh, g_q, g_kc, g_ks, g_kw, pe_ck, w_ck1, w_ck2, pe_cv, w_cv1, w_cv2,
               w_proj_m, w_proj_n, w_out, g_ffn, w_group, b_group, w_expert, b_expert, w_e1, w_e3, w_e2)
    x1_p, h2_p, eid_p, wts_p, kv_p, win_p, C_p, n_p, m_p, conv_p = hybrid_layer(
        x_prompt, jnp.zeros((B, M_CONV - 1, 2 * M_QK), dt), jnp.zeros((B, M_HEADS, M_DQK, M_DV), F32),
        jnp.zeros((B, M_HEADS, M_DQK), F32), jnp.zeros((B, M_HEADS), F32),
        None, jnp.zeros((B, 0, 2, N_KV, HEAD_DIM), dt), *weights, use_pallas_proj=True)
    with jax.default_matmul_precision("highest"):
        x1_s, h2_s, eid_s, wts_s, kv_s, win_s, C_s, n_s, m_s, conv_s = hybrid_layer(
            x_sample, state_conv, state_C, state_n, state_m, (cache_kv, page_table), state_win, *weights,
            use_pallas_proj=False)
    n_p_tok = h2_p.shape[0]
    moe = hier_moe_apply(jnp.concatenate([h2_p, h2_s], axis=0), jnp.concatenate([eid_p, eid_s], axis=0),
                         jnp.concatenate([wts_p, wts_s], axis=0), w_e1, w_e3, w_e2)
    y_p = x1_p + moe[:n_p_tok].reshape(x1_p.shape)
    y_s = x1_s + moe[n_p_tok:].reshape(x1_s.shape)
    return (y_p, y_s, kv_p, kv_s, win_p, win_s, C_p, C_s, n_p, n_s, m_p, m_s, conv_p, conv_s)
```

```python
import functools
import math

import jax
import jax.numpy as jnp
import numpy as np
from jax import lax
from jax.experimental import pallas as pl
from jax.experimental.pallas import tpu as pltpu

D_MODEL = 2048
M_HEADS = 4
M_DQK = 128
M_DV = 256
M_CONV = 4
M_CHUNK = 64
N_HEADS = 16
N_KV = 4
HEAD_DIM = 64
CMP_BLK = 32
SEL_BLK = 64
N_SELECT = 16
WINDOW = 512
NSA_QBLK = 64
FORCE_SCORE = 1e4
N_GROUPS = 4
EXPERTS_PER_GROUP = 8
N_EXPERTS = N_GROUPS * EXPERTS_PER_GROUP
TOP_K_FINE = 2
MOE_BLK = 128
EPS = 1e-6
M_QK = M_HEADS * M_DQK
M_VW = M_HEADS * M_DV
N_QW = N_HEADS * HEAD_DIM
N_KVW = 6 * N_KV * HEAD_DIM
SPLITS = (M_QK, M_QK, M_VW, M_HEADS, M_HEADS, M_VW, N_QW, N_KVW, 3 * N_HEADS, D_MODEL, D_MODEL)
D_IN = sum(SPLITS)

F32 = jnp.float32
BF16 = jnp.bfloat16
VMEM_LIMIT = 48 * 1024 * 1024


def _mxu(a, b):
    return jnp.dot(a, b, preferred_element_type=F32, precision=lax.Precision.DEFAULT)


def _norm_proj_kernel(x_ref, g_ref, w_ref, o_ref, hx_ref):
    @pl.when(pl.program_id(1) == 0)
    def _():
        x = x_ref[...]
        r = lax.rsqrt(jnp.mean(x * x, axis=-1, keepdims=True) + EPS)
        hx_ref[...] = (x * r * g_ref[...]).astype(BF16)

    o_ref[...] = _mxu(hx_ref[...], w_ref[...])


def norm_proj(x, g, w_bf16, *, tm=1024, tn=896):
    n, d = x.shape
    k = w_bf16.shape[1]
    return pl.pallas_call(
        _norm_proj_kernel,
        grid=(n // tm, k // tn),
        in_specs=[
            pl.BlockSpec((tm, d), lambda i, j: (i, 0)),
            pl.BlockSpec((1, d), lambda i, j: (0, 0)),
            pl.BlockSpec((d, tn), lambda i, j: (0, j)),
        ],
        out_specs=pl.BlockSpec((tm, tn), lambda i, j: (i, j)),
        out_shape=jax.ShapeDtypeStruct((n, k), F32),
        scratch_shapes=[pltpu.VMEM((tm, d), BF16)],
        compiler_params=pltpu.CompilerParams(
            dimension_semantics=("parallel", "arbitrary"), vmem_limit_bytes=VMEM_LIMIT),
        name="norm_proj",
    )(x, g.reshape(1, d), w_bf16)


def _rms_rows_kernel(x_ref, g_ref, o_ref):
    x = x_ref[...]
    r = lax.rsqrt(jnp.mean(x * x, axis=-1, keepdims=True) + EPS)
    o_ref[...] = (x * r * g_ref[...]).astype(o_ref.dtype)


def rms_rows(x, g, out_dtype, *, tr=2048):
    rows, c = x.shape
    tr = min(tr, rows)
    return pl.pallas_call(
        _rms_rows_kernel,
        grid=(rows // tr,),
        in_specs=[pl.BlockSpec((tr, c), lambda i: (i, 0)), pl.BlockSpec((1, c), lambda i: (0, 0))],
        out_specs=pl.BlockSpec((tr, c), lambda i: (i, 0)),
        out_shape=jax.ShapeDtypeStruct((rows, c), out_dtype),
        compiler_params=pltpu.CompilerParams(dimension_semantics=("parallel",)),
        name="rms_rows",
    )(x, g.reshape(1, c))


def _split_bf16(x):
    hi = lax.bitcast_convert_type(
        lax.bitcast_convert_type(x, jnp.int32) & jnp.int32(-65536), F32)
    return hi.astype(BF16), (x - hi).astype(BF16)


def _dot_split(a, b_hi, b_lo):
    a_hi, a_lo = _split_bf16(a)
    return _mxu(a_hi, b_hi) + (_mxu(a_lo, b_hi) + _mxu(a_hi, b_lo))


def _compress_kernel(x_ref, pe_ref, w1_ref, w1l_ref, w2_ref, w2l_ref, g_ref, o_ref, *, normalize, precise):
    x = x_ref[...] + pe_ref[...]
    if precise:
        h = _dot_split(x, w1_ref[...], w1l_ref[...])
    else:
        h = _mxu(x.astype(BF16), w1_ref[...])
    h = h * jax.nn.sigmoid(h)
    if precise:
        y = _dot_split(h, w2_ref[...], w2l_ref[...])
    else:
        y = _mxu(h.astype(BF16), w2_ref[...])
    if normalize:
        y = y * lax.rsqrt(jnp.mean(y * y, axis=-1, keepdims=True) + EPS) * g_ref[...]
    o_ref[...] = y.astype(o_ref.dtype)


def compress_rows(x, pe, w1, w2, g, *, normalize, precise=False, tr=512):
    rows, k = x.shape
    f = w1.shape[-1]
    d = w2.shape[1]
    tr = min(tr, rows)
    w1_hi, w1_lo = _split_bf16(w1.reshape(k, f))
    w2_hi, w2_lo = _split_bf16(w2)
    if not precise:
        w1_hi, w2_hi = w1.reshape(k, f).astype(BF16), w2.astype(BF16)
    return pl.pallas_call(
        functools.partial(_compress_kernel, normalize=normalize, precise=precise),
        grid=(rows // tr,),
        in_specs=[
            pl.BlockSpec((tr, k), lambda i: (i, 0)),
            pl.BlockSpec((1, k), lambda i: (0, 0)),
            pl.BlockSpec((k, f), lambda i: (0, 0)),
            pl.BlockSpec((k, f), lambda i: (0, 0)),
            pl.BlockSpec((f, d), lambda i: (0, 0)),
            pl.BlockSpec((f, d), lambda i: (0, 0)),
            pl.BlockSpec((1, d), lambda i: (0, 0)),
        ],
        out_specs=pl.BlockSpec((tr, d), lambda i: (i, 0)),
        out_shape=jax.ShapeDtypeStruct((rows, d), F32 if precise else BF16),
        compiler_params=pltpu.CompilerParams(dimension_semantics=("parallel",)),
        name="nsa_compress",
    )(x, pe.reshape(1, k), w1_hi, w1_lo, w2_hi, w2_lo, g.reshape(1, d))


CMP_TILE_ROWS = 128


def _compress_paged_kernel(x_ref, pe_ref, w_hi_ref, w_lo_ref, w2_hi_ref, w2_lo_ref, g_ref, o_ref, *,
                           normalize, cpp, hidden):
    rows = x_ref.shape[0]
    hd = x_ref.shape[1]
    acc = jnp.zeros((rows, cpp * hidden), F32)
    for dp in range(hd // 2):
        a = x_ref[:, 2 * dp, :] + pe_ref[pl.ds(2 * dp, 1), :]
        b = x_ref[:, 2 * dp + 1, :] + pe_ref[pl.ds(2 * dp + 1, 1), :]
        acc = acc + _dot_split(jnp.concatenate([a, b], axis=1), w_hi_ref[dp], w_lo_ref[dp])
    hid = acc * jax.nn.sigmoid(acc)
    for c in range(cpp):
        y = _dot_split(hid[:, c * hidden:(c + 1) * hidden], w2_hi_ref[...], w2_lo_ref[...])
        if normalize:
            y = y * lax.rsqrt(jnp.mean(y * y, axis=-1, keepdims=True) + EPS) * g_ref[...]
        o_ref[:, c * hd:(c + 1) * hd] = y


def compress_paged(x_tiles, pe, w1, w2, g, *, normalize):
    r, hd, page = x_tiles.shape
    cpp = page // CMP_BLK
    hidden = w1.shape[-1]
    eye = jnp.eye(cpp, dtype=F32)
    w_big = jnp.einsum('cC,ldf->dclCf', eye, w1).reshape(hd // 2, 2 * page, cpp * hidden)
    w_hi, w_lo = _split_bf16(w_big)
    w2_hi, w2_lo = _split_bf16(w2)
    pe_t = jnp.tile(pe.T, (1, cpp))
    const3 = lambda i: (0, 0, 0)
    const2 = lambda i: (0, 0)
    return pl.pallas_call(
        functools.partial(_compress_paged_kernel, normalize=normalize, cpp=cpp, hidden=hidden),
        grid=(r // CMP_TILE_ROWS,),
        in_specs=[
            pl.BlockSpec((CMP_TILE_ROWS, hd, page), lambda i: (i, 0, 0)),
            pl.BlockSpec((hd, page), const2),
            pl.BlockSpec((hd // 2, 2 * page, cpp * hidden), const3, pipeline_mode=pl.Buffered(1)),
            pl.BlockSpec((hd // 2, 2 * page, cpp * hidden), const3, pipeline_mode=pl.Buffered(1)),
            pl.BlockSpec((hidden, hd), const2),
            pl.BlockSpec((hidden, hd), const2),
            pl.BlockSpec((1, hd), const2),
        ],
        out_specs=pl.BlockSpec((CMP_TILE_ROWS, cpp * hd), lambda i: (i, 0)),
        out_shape=jax.ShapeDtypeStruct((r, cpp * hd), F32),
        compiler_params=pltpu.CompilerParams(dimension_semantics=("parallel",), vmem_limit_bytes=VMEM_LIMIT),
        name="nsa_compress_paged",
    )(x_tiles, pe_t, w_hi, w_lo, w2_hi, w2_lo, g.reshape(1, hd))


NSA_TQ = 128
NSA_TK = 512
HPG = N_HEADS // N_KV
NSA_Q = HPG * NSA_TQ
NEG_BIG = -1e30


def _nsa_prompt_kernel(qT_ref, gq_ref, gate_ref, kc_ref, vcT_ref, ks_ref, vsT_ref, kw_ref, vwT_ref,
                       o_ref, sc_ref, sel_ref, imp_ref, *, n_cmp, n_sel):
    j = pl.program_id(2)
    s0 = j * NSA_TQ
    q = qT_ref[0, 0, 0]
    r = lax.rsqrt(jnp.mean(q * q, axis=0, keepdims=True) + EPS)
    qn = (q * r * gq_ref[...] * (HEAD_DIM ** -0.5 * math.log2(math.e))).astype(BF16)
    lane = lax.broadcasted_iota(jnp.int32, (1, NSA_Q), 1)
    t_row = s0 + (lane & (NSA_TQ - 1))

    s = _mxu(kc_ref[0, 0], qn)
    c_end = lax.broadcasted_iota(jnp.int32, (n_cmp, NSA_Q), 0) * CMP_BLK + (CMP_BLK - 1)
    mask_c = c_end <= t_row
    s = jnp.where(mask_c, s, -jnp.inf)
    m = jnp.max(s, axis=0, keepdims=True)
    m = jnp.where(m > -jnp.inf, m, 0.0)
    e = jnp.where(mask_c, jnp.exp2(s - m), 0.0)
    p_c = e / jnp.maximum(jnp.sum(e, axis=0, keepdims=True), 1e-30)
    o_c = _mxu(vcT_ref[0, 0], p_c.astype(BF16))

    imp_ref[...] = (p_c[:, 0:NSA_TQ] + p_c[:, NSA_TQ:2 * NSA_TQ]
                    + p_c[:, 2 * NSA_TQ:3 * NSA_TQ] + p_c[:, 3 * NSA_TQ:4 * NSA_TQ])
    imp = imp_ref[pl.ds(0, n_sel, stride=2), :] + imp_ref[pl.ds(1, n_sel, stride=2), :]
    n_iota = lax.broadcasted_iota(jnp.int32, (n_sel, NSA_TQ), 0)
    cur = (s0 + lax.broadcasted_iota(jnp.int32, (n_sel, NSA_TQ), 1)) // SEL_BLK
    valid = n_iota <= cur
    forced = (n_iota == 0) | (n_iota == cur) | (n_iota == cur - 1)
    score = jnp.where(valid, jnp.where(forced, FORCE_SCORE, imp), -jnp.inf)
    sc_ref[...] = score

    def rank_body(n2, rank):
        row = jnp.broadcast_to(sc_ref[pl.ds(n2, 1), :], (n_sel, NSA_TQ))
        beats = (row > score) | ((row == score) & (n2 < n_iota))
        return rank + jnp.where(beats, 1.0, 0.0)

    n_live = jnp.minimum((s0 + NSA_TQ - 1) // SEL_BLK + 1, n_sel)
    rank = lax.fori_loop(0, n_live, rank_body, jnp.zeros((n_sel, NSA_TQ), F32))
    sel_ref[...] = jnp.where(valid & (rank < N_SELECT), 0.0, NEG_BIG)

    key_iota = lax.broadcasted_iota(jnp.int32, (NSA_TK, NSA_TQ), 0)
    t_q = s0 + lax.broadcasted_iota(jnp.int32, (1, NSA_TQ), 1)

    def attend(k_ref, vT_ref, c_lo, c_hi, bias_fn):
        def body(c, carry):
            m_i, l_i, acc = carry
            k = k_ref[0, 0, pl.ds(pl.multiple_of(c * NSA_TK, NSA_TK), NSA_TK), :]
            sk = _mxu(k, qn)
            bias = bias_fn(c)
            ms, ls, ps = [], [], []
            for h in range(HPG):
                sl = slice(h * NSA_TQ, (h + 1) * NSA_TQ)
                s_h = sk[:, sl] + bias
                m_h = jnp.maximum(m_i[:, sl], jnp.max(s_h, axis=0, keepdims=True))
                p_h = jnp.exp2(s_h - m_h)
                ms.append(m_h)
                ls.append(jnp.sum(p_h, axis=0, keepdims=True))
                ps.append(p_h.astype(BF16))
            m_new = jnp.concatenate(ms, axis=1)
            alpha = jnp.exp2(m_i - m_new)
            l_new = alpha * l_i + jnp.concatenate(ls, axis=1)
            acc = alpha * acc + _mxu(vT_ref[0, 0, c], jnp.concatenate(ps, axis=1))
            return m_new, l_new, acc

        init = (jnp.full((1, NSA_Q), NEG_BIG, F32), jnp.zeros((1, NSA_Q), F32),
                jnp.zeros((HEAD_DIM, NSA_Q), F32))
        _, l_f, acc_f = lax.fori_loop(c_lo, c_hi, body, init)
        return acc_f / jnp.maximum(l_f, 1e-30)

    def sel_bias(c):
        pos = c * NSA_TK + key_iota
        per_blk = [jnp.broadcast_to(
            sel_ref[pl.ds(jnp.minimum(c * (NSA_TK // SEL_BLK) + i, n_sel - 1), 1), :], (SEL_BLK, NSA_TQ))
            for i in range(NSA_TK // SEL_BLK)]
        return jnp.where(pos <= t_q, jnp.concatenate(per_blk, axis=0), NEG_BIG)

    def win_bias(c):
        pos = c * NSA_TK + key_iota
        return jnp.where((pos <= t_q) & (pos > t_q - WINDOW), 0.0, NEG_BIG)

    c_hi = (s0 + NSA_TQ + NSA_TK - 1) // NSA_TK
    o_s = attend(ks_ref, vsT_ref, 0, c_hi, sel_bias)
    o_w = attend(kw_ref, vwT_ref, jnp.maximum(s0 - (WINDOW - 1), 0) // NSA_TK, c_hi, win_bias)

    g = jax.nn.sigmoid(gate_ref[0, 0, 0])
    o_ref[0, 0, 0] = g[0:1] * o_c + g[1:2] * o_s + g[2:3] * o_w


def nsa_prompt(nq, ng, kv_new, g_q, g_kc, g_ks, g_kw, pe_ck, w_ck1, w_ck2, pe_cv, w_cv1, w_cv2):
    B, T = nq.shape[:2]
    G = N_KV
    nqb = T // NSA_TQ
    n_cmp = T // CMP_BLK
    n_sel = T // SEL_BLK
    nch = T // NSA_TK
    qT = nq.reshape(B, nqb, NSA_TQ, G, HPG, HEAD_DIM).transpose(0, 3, 1, 5, 4, 2).reshape(B, G, nqb, HEAD_DIM, NSA_Q)
    gT = ng.reshape(B, nqb, NSA_TQ, G, HPG, 3).transpose(0, 3, 1, 5, 4, 2).reshape(B, G, nqb, 3, NSA_Q)
    kv_g = kv_new.transpose(2, 0, 3, 1, 4)
    xk = kv_g[0].reshape(B * G * n_cmp, CMP_BLK * HEAD_DIM)
    xv = kv_g[1].reshape(B * G * n_cmp, CMP_BLK * HEAD_DIM)
    kc = compress_rows(xk, pe_ck, w_ck1, w_ck2, g_kc, normalize=True).reshape(B, G, n_cmp, HEAD_DIM)
    vc = compress_rows(xv, pe_cv, w_cv1, w_cv2, g_kc, normalize=False).reshape(B, G, n_cmp, HEAD_DIM)
    vcT = vc.transpose(0, 1, 3, 2)
    ks = rms_rows(kv_g[2].reshape(B * G * T, HEAD_DIM), g_ks, BF16).reshape(B, G, T, HEAD_DIM)
    kw = rms_rows(kv_g[4].reshape(B * G * T, HEAD_DIM), g_kw, BF16).reshape(B, G, T, HEAD_DIM)
    vsT = kv_g[3].astype(BF16).reshape(B, G, nch, NSA_TK, HEAD_DIM).transpose(0, 1, 2, 4, 3)
    vwT = kv_g[5].astype(BF16).reshape(B, G, nch, NSA_TK, HEAD_DIM).transpose(0, 1, 2, 4, 3)

    bg = lambda b, g, j: (b, g, 0, 0)
    bg5 = lambda b, g, j: (b, g, 0, 0, 0)
    bgj = lambda b, g, j: (b, g, j, 0, 0)
    oT = pl.pallas_call(
        functools.partial(_nsa_prompt_kernel, n_cmp=n_cmp, n_sel=n_sel),
        grid=(B, G, nqb),
        in_specs=[
            pl.BlockSpec((1, 1, 1, HEAD_DIM, NSA_Q), bgj),
            pl.BlockSpec((HEAD_DIM, 1), lambda b, g, j: (0, 0)),
            pl.BlockSpec((1, 1, 1, 3, NSA_Q), bgj),
            pl.BlockSpec((1, 1, n_cmp, HEAD_DIM), bg),
            pl.BlockSpec((1, 1, HEAD_DIM, n_cmp), bg),
            pl.BlockSpec((1, 1, T, HEAD_DIM), bg),
            pl.BlockSpec((1, 1, nch, HEAD_DIM, NSA_TK), bg5),
            pl.BlockSpec((1, 1, T, HEAD_DIM), bg),
            pl.BlockSpec((1, 1, nch, HEAD_DIM, NSA_TK), bg5),
        ],
        out_specs=pl.BlockSpec((1, 1, 1, HEAD_DIM, NSA_Q), bgj),
        out_shape=jax.ShapeDtypeStruct((B, G, nqb, HEAD_DIM, NSA_Q), F32),
        scratch_shapes=[pltpu.VMEM((n_sel, NSA_TQ), F32), pltpu.VMEM((n_sel, NSA_TQ), F32),
                        pltpu.VMEM((n_cmp, NSA_TQ), F32)],
        compiler_params=pltpu.CompilerParams(
            dimension_semantics=("parallel", "parallel", "arbitrary"), vmem_limit_bytes=VMEM_LIMIT),
        name="nsa_prompt",
    )(qT, g_q.reshape(HEAD_DIM, 1), gT, kc, vcT, ks, vsT, kw, vwT)
    return oT.reshape(B, G, nqb, HEAD_DIM, HPG, NSA_TQ).transpose(0, 2, 5, 1, 4, 3).reshape(B, T, N_QW)


def rms_norm(x, g):
    xf = x.astype(F32)
    y = xf * lax.rsqrt(jnp.mean(xf * xf, axis=-1, keepdims=True) + EPS)
    return (y * g.astype(F32)).astype(x.dtype)


def masked_softmax(s, mask):
    s = jnp.where(mask, s.astype(F32), -jnp.inf)
    m = jnp.max(s, axis=-1, keepdims=True)
    m = jnp.where(jnp.isfinite(m), m, 0.0)
    e = jnp.where(mask, jnp.exp(s - m), 0.0)
    return e / jnp.maximum(jnp.sum(e, axis=-1, keepdims=True), 1e-30)


def short_conv(u, prev, w):
    T = u.shape[1]
    full = jnp.concatenate([prev.astype(u.dtype), u], axis=1)
    out = full[:, 0:T] * w[0]
    for j in range(1, M_CONV):
        out = out + full[:, j:j + T] * w[j]
    return out, full[:, T:]


def mlstm_chunkwise(q, k, v, i_pre, logf, C0, n0, m0):
    B, T = q.shape[:2]
    L = M_CHUNK if T % M_CHUNK == 0 else T
    nc = T // L

    def chunks(a):
        a = a.reshape((B, nc, L) + a.shape[2:])
        return jnp.moveaxis(jnp.moveaxis(a, 3, 2), 1, 0)

    causal = jnp.tril(jnp.ones((L, L), bool))

    def step(carry, xs):
        C, n, m = carry
        qc, kc, vc, ic, fc = xs
        b = jnp.cumsum(fc, axis=-1)
        logD = jnp.where(causal, b[..., :, None] - b[..., None, :] + ic[..., None, :], -jnp.inf)
        inter = b + m[..., None]
        m_t = jnp.maximum(inter, jnp.max(logD, axis=-1))
        a = jnp.exp(inter - m_t)
        S = jnp.einsum('bhtd,bhsd->bhts', qc, kc) * jnp.exp(logD - m_t[..., None])
        num = a[..., None] * jnp.einsum('bhtd,bhde->bhte', qc, C) + jnp.einsum('bhts,bhse->bhte', S, vc)
        den = a * jnp.einsum('bhtd,bhd->bht', qc, n) + jnp.sum(S, axis=-1)
        h = num / jnp.maximum(jnp.abs(den), jnp.exp(-m_t))[..., None]
        m_new = m_t[..., -1]
        w = jnp.exp(b[..., -1:] - b + ic - m_new[..., None])
        aL = jnp.exp(b[..., -1] + m - m_new)
        C_new = aL[..., None, None] * C + jnp.einsum('bhs,bhsd,bhse->bhde', w, kc, vc)
        n_new = aL[..., None] * n + jnp.einsum('bhs,bhsd->bhd', w, kc)
        return (C_new, n_new, m_new), h

    (C, n, m), h = lax.scan(step, (C0, n0, m0), (chunks(q), chunks(k), chunks(v), chunks(i_pre), chunks(logf)))
    h = jnp.moveaxis(jnp.moveaxis(h, 0, 1), 2, 3).reshape(B, T, M_HEADS, M_DV)
    return h, C, n, m


def compress(rows, pe, w1, w2):
    B, Lk = rows.shape[:2]
    nc = Lk // CMP_BLK
    blk = rows[:, :nc * CMP_BLK].reshape(B, nc, CMP_BLK, N_KV, HEAD_DIM) + pe[:, None, :]
    hid = jax.nn.silu(jnp.einsum('bclgd,ldf->bcgf', blk, w1))
    return jnp.einsum('bcgf,fd->bcgd', hid, w2)


def nsa_attention(q, gates, kv_all, kw_full, q0, g_kc, g_ks, g_kw, pe_ck, w_ck1, w_ck2, pe_cv, w_cv1, w_cv2):
    B, T = q.shape[:2]
    Lk = kv_all.shape[1]
    hpg = N_HEADS // N_KV
    scale = HEAD_DIM ** -0.5
    kc = rms_norm(compress(kv_all[:, :, 0], pe_ck, w_ck1, w_ck2), g_kc)
    vc = compress(kv_all[:, :, 1], pe_cv, w_cv1, w_cv2)
    nc = kc.shape[1]
    n_sel = -(-Lk // SEL_BLK)
    pad = n_sel * SEL_BLK - Lk
    ks = jnp.pad(rms_norm(kv_all[:, :, 2], g_ks), ((0, 0), (0, pad), (0, 0), (0, 0)))
    ks = ks.reshape(B, n_sel, SEL_BLK, N_KV, HEAD_DIM).transpose(0, 3, 1, 2, 4)
    vs = jnp.pad(kv_all[:, :, 3], ((0, 0), (0, pad), (0, 0), (0, 0)))
    vs = vs.reshape(B, n_sel, SEL_BLK, N_KV, HEAD_DIM).transpose(0, 3, 1, 2, 4)
    kw = rms_norm(kw_full[:, :, 0], g_kw)
    vw = kw_full[:, :, 1]
    top = min(N_SELECT, n_sel)
    ratio = SEL_BLK // CMP_BLK
    qb = NSA_QBLK if T % NSA_QBLK == 0 else T
    nqb = T // qb
    qg = q.reshape(B, nqb, qb, N_KV, hpg, HEAD_DIM).transpose(1, 0, 3, 4, 2, 5)
    gg = jnp.moveaxis(gates.reshape(B, nqb, qb, N_HEADS, 3), 1, 0)
    cmp_end = (jnp.arange(nc) + 1) * CMP_BLK - 1
    blk_ids = jnp.arange(n_sel)
    b_ix = jnp.arange(B)[:, None, None, None]
    g_ix = jnp.arange(N_KV)[None, :, None, None]

    def block(args):
        qi, gi, j = args
        s0 = j * qb
        t = q0 + s0 + jnp.arange(qb)
        sc = jnp.einsum('bghqd,bcgd->bghqc', qi, kc) * scale
        p_c = masked_softmax(sc, cmp_end[None, :] <= t[:, None])
        o_c = jnp.einsum('bghqc,bcgd->bghqd', p_c.astype(vc.dtype), vc)
        imp = jnp.sum(p_c, axis=2)
        imp = jnp.pad(imp, ((0, 0), (0, 0), (0, 0), (0, n_sel * ratio - nc)))
        imp = imp.reshape(B, N_KV, qb, n_sel, ratio).sum(-1)
        cur = t // SEL_BLK
        valid = blk_ids[None, :] <= cur[:, None]
        forced = (blk_ids[None, :] == 0) | (blk_ids[None, :] == cur[:, None]) | (blk_ids[None, :] == cur[:, None] - 1)
        imp = jnp.where(valid, jnp.where(forced, FORCE_SCORE, imp), -jnp.inf)
        top_v, top_i = lax.top_k(imp, top)
        k_sel = ks[b_ix, g_ix, top_i]
        v_sel = vs[b_ix, g_ix, top_i]
        pos = top_i[..., None] * SEL_BLK + jnp.arange(SEL_BLK)
        m_s = (pos <= t[:, None, None]) & jnp.isfinite(top_v)[..., None]
        ss = (jnp.einsum('bghqd,bgqnkd->bghqnk', qi, k_sel) * scale).reshape(B, N_KV, hpg, qb, top * SEL_BLK)
        p_s = masked_softmax(ss, m_s.reshape(B, N_KV, 1, qb, top * SEL_BLK))
        o_s = jnp.einsum('bghqnk,bgqnkd->bghqd', p_s.reshape(B, N_KV, hpg, qb, top, SEL_BLK).astype(v_sel.dtype), v_sel)
        kwi = lax.dynamic_slice_in_dim(kw, s0, qb + WINDOW, axis=1)
        vwi = lax.dynamic_slice_in_dim(vw, s0, qb + WINDOW, axis=1)
        pw = q0 - WINDOW + s0 + jnp.arange(qb + WINDOW)
        m_w = (pw[None, :] <= t[:, None]) & (pw[None, :] > t[:, None] - WINDOW) & (pw[None, :] >= 0)
        sw = jnp.einsum('bghqd,bkgd->bghqk', qi, kwi) * scale
        p_w = masked_softmax(sw, m_w)
        o_w = jnp.einsum('bghqk,bkgd->bghqd', p_w.astype(vwi.dtype), vwi)
        o = jnp.stack([o_c, o_s, o_w], axis=-1)
        gi_r = gi.reshape(B, qb, N_KV, hpg, 3).transpose(0, 2, 3, 1, 4)
        return jnp.einsum('bghqdc,bghqc->bqghd', o, gi_r.astype(o.dtype))

    out = lax.map(block, (qg, gg, jnp.arange(nqb)))
    return jnp.moveaxis(out, 0, 1).reshape(B, T, N_QW)


def nsa_sample(q, gates, cache_kv, page_table, kv_rows, win_all, g_kc, g_ks, g_kw,
               pe_ck, w_ck1, w_ck2, pe_cv, w_cv1, w_cv2):
    B, T = q.shape[:2]
    n_pages = page_table.shape[1]
    page = cache_kv.shape[1]
    P = n_pages * page
    G, hd = N_KV, HEAD_DIM
    assert T < CMP_BLK and T <= SEL_BLK and P % SEL_BLK == 0 and page % SEL_BLK == 0
    assert win_all.shape[1] == WINDOW + T
    nc = P // CMP_BLK
    cpp = page // CMP_BLK
    n_last = P // SEL_BLK
    n_sel = n_last + 1
    top = min(N_SELECT, n_sel)
    ratio = SEL_BLK // CMP_BLK
    scale = hd ** -0.5
    t = P + jnp.arange(T)
    qg = q.reshape(B, T, G, HPG, hd)

    cache_t = cache_kv.transpose(0, 2, 3, 4, 1)
    flat_pages = page_table.reshape(-1)

    def cmp_tiles(kind):
        tiles = jax.vmap(lambda p: lax.dynamic_slice(cache_t, (p, kind, 0, 0, 0), (1, 1, G, hd, page))[0, 0])(flat_pages)
        return tiles.reshape(B * n_pages * G, hd, page)

    def cmp_rows(y):
        return y.reshape(B, n_pages, G, cpp, hd).transpose(0, 2, 1, 3, 4).reshape(B, G, nc, hd)

    kc = cmp_rows(compress_paged(cmp_tiles(0), pe_ck, w_ck1, w_ck2, g_kc, normalize=True))
    vc = cmp_rows(compress_paged(cmp_tiles(1), pe_cv, w_cv1, w_cv2, g_kc, normalize=False))
    cmp_end = (jnp.arange(nc) + 1) * CMP_BLK - 1
    sc = jnp.einsum('btghd,bgcd->bghtc', qg, kc) * scale
    p_c = masked_softmax(sc, cmp_end[None, :] <= t[:, None])
    o_c = jnp.einsum('bghtc,bgcd->bghtd', p_c, vc)

    imp = jnp.sum(p_c, axis=2)
    imp = jnp.pad(imp, ((0, 0), (0, 0), (0, 0), (0, n_sel * ratio - nc)))
    imp = imp.reshape(B, G, T, n_sel, ratio).sum(-1)
    blk_ids = jnp.arange(n_sel)
    cur = t // SEL_BLK
    valid = blk_ids[None, :] <= cur[:, None]
    forced = (blk_ids[None, :] == 0) | (blk_ids[None, :] == cur[:, None]) | (blk_ids[None, :] == cur[:, None] - 1)
    imp = jnp.where(valid, jnp.where(forced, FORCE_SCORE, imp), -jnp.inf)
    ids = jnp.arange(n_sel)
    rest, top_v, top_i = imp, [], []
    for _ in range(top):
        i_max = jnp.argmax(rest, axis=-1)
        top_i.append(i_max)
        top_v.append(jnp.max(rest, axis=-1))
        rest = jnp.where(ids == i_max[..., None], -jnp.inf, rest)
    top_v, top_i = jnp.stack(top_v, axis=-1), jnp.stack(top_i, axis=-1)

    bpp = page // SEL_BLK
    blk_c = jnp.minimum(top_i, n_last - 1)
    b_ix = jnp.arange(B)[:, None, None, None]
    phys = page_table[b_ix, blk_c // bpp]
    g_ix = jnp.broadcast_to(jnp.arange(G)[None, :, None, None], top_i.shape)

    def take_tile(p, g):
        return lax.dynamic_slice(cache_t, (p, 2, g, 0, 0), (1, 2, 1, hd, page))[0, :, 0]

    tiles = jax.vmap(take_tile)(phys.reshape(-1), g_ix.reshape(-1)).reshape(B, G, T, top, 2, hd, page)
    new_tile = jnp.pad(kv_rows[:, :, 2:4], ((0, 0), (0, page - T), (0, 0), (0, 0), (0, 0)))
    new_tile = new_tile.transpose(0, 3, 2, 4, 1)[:, :, None, None]
    is_new = top_i == n_last
    tiles = jnp.where(is_new[..., None, None, None], new_tile, tiles)
    k_t = tiles[..., 0, :, :]
    v_t = tiles[..., 1, :, :]
    k_t = k_t * lax.rsqrt(jnp.mean(k_t * k_t, axis=-2, keepdims=True) + EPS) * g_ks[:, None]
    lane = jnp.arange(page)
    tile0 = jnp.where(is_new, n_last, blk_c // bpp * bpp)
    pos = tile0[..., None] * SEL_BLK + lane
    m_s = ((pos // SEL_BLK == top_i[..., None]) & (pos <= t[:, None, None])
           & jnp.isfinite(top_v)[..., None])
    ss = (jnp.einsum('btghd,bgtndp->bghtnp', qg, k_t) * scale).reshape(B, G, HPG, T, top * page)
    p_s = masked_softmax(ss, m_s.reshape(B, G, 1, T, top * page))
    o_s = jnp.einsum('bghtnp,bgtndp->bghtd', p_s.reshape(B, G, HPG, T, top, page), v_t)

    kw = rms_norm(win_all[:, :, 0], g_kw)
    vw = win_all[:, :, 1]
    pw = P - WINDOW + jnp.arange(T + WINDOW)
    m_w = (pw[None, :] <= t[:, None]) & (pw[None, :] > t[:, None] - WINDOW) & (pw[None, :] >= 0)
    sw = jnp.einsum('btghd,bkgd->bghtk', qg, kw) * scale
    p_w = masked_softmax(sw, m_w)
    o_w = jnp.einsum('bghtk,bkgd->bghtd', p_w, vw)

    o = jnp.stack([o_c, o_s, o_w], axis=-1)
    return jnp.einsum('bghtdc,btghc->btghd', o, gates.reshape(B, T, G, HPG, 3)).reshape(B, T, N_QW)


def moe_route(group_logits, expert_logits):
    assert TOP_K_FINE == 2
    N = group_logits.shape[0]
    pg = jax.nn.softmax(group_logits.astype(F32), axis=-1)
    grp = jnp.argmax(pg, axis=-1)
    p_grp = jnp.max(pg, axis=-1)
    le = expert_logits.astype(F32).reshape(N, N_GROUPS, EXPERTS_PER_GROUP)
    pe = jax.nn.softmax(le[jnp.arange(N), grp], axis=-1)
    i1 = jnp.argmax(pe, axis=-1)
    hit1 = jnp.arange(EXPERTS_PER_GROUP)[None, :] == i1[:, None]
    rest = jnp.where(hit1, -jnp.inf, pe)
    i2 = jnp.argmax(rest, axis=-1)
    top_i = jnp.stack([i1, i2], axis=-1)
    top_p = jnp.stack([jnp.max(pe, axis=-1), jnp.max(rest, axis=-1)], axis=-1)
    wts = p_grp[:, None] * top_p / jnp.sum(top_p, axis=-1, keepdims=True)
    eid = grp[:, None] * EXPERTS_PER_GROUP + top_i
    return eid.astype(jnp.int32), wts


MOE_TM = 256


def _moe_ffn_kernel(be_ref, x_ref, w1_ref, w3_ref, w2_ref, wt_ref, o_ref, w1b_ref, w3b_ref, w2b_ref):
    i = pl.program_id(0)

    @pl.when((i == 0) | (be_ref[i] != be_ref[jnp.maximum(i - 1, 0)]))
    def _():
        w1b_ref[...] = w1_ref[0].astype(BF16)
        w3b_ref[...] = w3_ref[0].astype(BF16)
        w2b_ref[...] = w2_ref[0].astype(BF16)

    x = x_ref[...]
    a = _mxu(x, w1b_ref[...])
    b = _mxu(x, w3b_ref[...])
    h = (a * jax.nn.sigmoid(a) * b).astype(BF16)
    o_ref[...] = _mxu(h, w2b_ref[...]) * wt_ref[...]


def moe_ffn(rows, row_wt, blk_e, w1, w3, w2):
    m_pad, d = rows.shape
    n_blk = m_pad // MOE_TM
    f = w1.shape[2]
    return pl.pallas_call(
        _moe_ffn_kernel,
        grid_spec=pltpu.PrefetchScalarGridSpec(
            num_scalar_prefetch=1,
            grid=(n_blk,),
            in_specs=[
                pl.BlockSpec((MOE_TM, d), lambda i, be: (i, 0)),
                pl.BlockSpec((1, d, f), lambda i, be: (be[i], 0, 0)),
                pl.BlockSpec((1, d, f), lambda i, be: (be[i], 0, 0)),
                pl.BlockSpec((1, f, d), lambda i, be: (be[i], 0, 0)),
                pl.BlockSpec((MOE_TM, 1), lambda i, be: (i, 0)),
            ],
            out_specs=pl.BlockSpec((MOE_TM, d), lambda i, be: (i, 0)),
            scratch_shapes=[pltpu.VMEM((d, f), BF16), pltpu.VMEM((d, f), BF16), pltpu.VMEM((f, d), BF16)],
        ),
        out_shape=jax.ShapeDtypeStruct((m_pad, d), F32),
        compiler_params=pltpu.CompilerParams(
            dimension_semantics=("arbitrary",), vmem_limit_bytes=VMEM_LIMIT),
        name="moe_ffn",
    )(blk_e, rows, w1, w3, w2, row_wt)


def hier_moe_apply(x, eid, wts, w_e1, w_e3, w_e2):
    N, D = x.shape
    M = N * TOP_K_FINE
    eid_f = eid.reshape(-1)
    onehot = (eid_f[:, None] == jnp.arange(N_EXPERTS, dtype=jnp.int32)[None, :]).astype(jnp.int32)
    run = jnp.cumsum(onehot, axis=0)
    counts = run[-1]
    rank_in_e = jnp.sum(run * onehot, axis=1) - 1
    padded = (counts + MOE_TM - 1) // MOE_TM * MOE_TM
    p_end = jnp.cumsum(padded)
    p_start = p_end - padded
    pos = (p_start[eid_f] + rank_in_e).astype(jnp.int32)
    n_blk = (M + N_EXPERTS * (MOE_TM - 1) + MOE_TM - 1) // MOE_TM
    m_pad = n_blk * MOE_TM
    src_tok = jnp.zeros((m_pad,), jnp.int32).at[pos].set(jnp.arange(M, dtype=jnp.int32) // TOP_K_FINE)
    row_wt = jnp.zeros((m_pad,), F32).at[pos].set(wts.reshape(-1))
    blk_e = jnp.minimum(jnp.searchsorted(p_end, jnp.arange(n_blk, dtype=jnp.int32) * MOE_TM, side='right'),
                        N_EXPERTS - 1).astype(jnp.int32)
    rows = x.astype(BF16)[src_tok]
    out = moe_ffn(rows, row_wt.reshape(m_pad, 1), blk_e, w_e1, w_e3, w_e2)
    pos = pos.reshape(N, TOP_K_FINE)
    return out[pos[:, 0]] + out[pos[:, 1]]


OUT_TM = 256
ROUTER_PAD = 128


def _out_proj_kernel(x_ref, gm_ref, gn_ref, hm_ref, hn_ref, wm_ref, wn_ref, wo_ref, gf_ref, wr_ref, br_ref,
                     x1_ref, h2_ref, lg_ref):
    pm = _mxu(hm_ref[...].astype(BF16), wm_ref[...])
    pn = _mxu(hn_ref[...].astype(BF16), wn_ref[...])
    u = jax.nn.sigmoid(gm_ref[...]) * pm + jax.nn.sigmoid(gn_ref[...]) * pn
    x1 = x_ref[...] + _mxu(u.astype(BF16), wo_ref[...])
    x1_ref[...] = x1
    h2 = (x1 * lax.rsqrt(jnp.mean(x1 * x1, axis=-1, keepdims=True) + EPS) * gf_ref[...]).astype(BF16)
    h2_ref[...] = h2
    lg_ref[...] = _mxu(h2, wr_ref[...]) + br_ref[...]


def out_proj(x, proj, gate_col_blocks, h_m, h_n, w_proj_m, w_proj_n, w_out, g_ffn, w_group, b_group, w_expert, b_expert):
    n, d = x.shape
    n_r = N_GROUPS + N_EXPERTS
    w_r = jnp.pad(jnp.concatenate([w_group, w_expert], axis=1), ((0, 0), (0, ROUTER_PAD - n_r))).astype(BF16)
    b_r = jnp.pad(jnp.concatenate([b_group, b_expert]), (0, ROUTER_PAD - n_r)).reshape(1, ROUTER_PAD)
    cm, cn = gate_col_blocks
    row = lambda i: (i, 0)
    const = lambda i: (0, 0)
    once = pl.Buffered(1)
    return pl.pallas_call(
        _out_proj_kernel,
        grid=(n // OUT_TM,),
        in_specs=[
            pl.BlockSpec((OUT_TM, d), row),
            pl.BlockSpec((OUT_TM, d), lambda i: (i, cm)),
            pl.BlockSpec((OUT_TM, d), lambda i: (i, cn)),
            pl.BlockSpec((OUT_TM, h_m.shape[1]), row),
            pl.BlockSpec((OUT_TM, h_n.shape[1]), row),
            pl.BlockSpec(w_proj_m.shape, const, pipeline_mode=once),
            pl.BlockSpec(w_proj_n.shape, const, pipeline_mode=once),
            pl.BlockSpec(w_out.shape, const, pipeline_mode=once),
            pl.BlockSpec((1, d), const),
            pl.BlockSpec((d, ROUTER_PAD), const),
            pl.BlockSpec((1, ROUTER_PAD), const),
        ],
        out_specs=[pl.BlockSpec((OUT_TM, d), row), pl.BlockSpec((OUT_TM, d), row),
                   pl.BlockSpec((OUT_TM, ROUTER_PAD), row)],
        out_shape=[jax.ShapeDtypeStruct((n, d), F32), jax.ShapeDtypeStruct((n, d), BF16),
                   jax.ShapeDtypeStruct((n, ROUTER_PAD), F32)],
        compiler_params=pltpu.CompilerParams(dimension_semantics=("parallel",), vmem_limit_bytes=VMEM_LIMIT),
        name="out_proj",
    )(x, proj, proj, h_m, h_n, w_proj_m.astype(BF16), w_proj_n.astype(BF16), w_out.astype(BF16),
      g_ffn.reshape(1, d), w_r, b_r)


_SEG = dict(zip(("mq", "mk", "mv", "mi", "mf", "mo", "nq", "nkv", "ng", "gate_m", "gate_n"),
                zip(np.cumsum((0,) + SPLITS[:-1]).tolist(), SPLITS)))
_PROJ_ORDER = ("gate_m", "gate_n", "mv", "mo", "nq", "mq", "mk", "nkv", "mi", "mf", "ng")
_PROJ_OFF = dict(zip(_PROJ_ORDER, np.cumsum([0] + [_SEG[k][1] for k in _PROJ_ORDER[:-1]]).tolist()))
D_IN_PAD = -(-D_IN // 896) * 896


def hybrid_layer(x, conv_prev, C0, n0, m0, paged, win_past, g_mix, w_in, w_conv, b_i, b_f, g_mh, g_q, g_kc, g_ks, g_kw,
                 pe_ck, w_ck1, w_ck2, pe_cv, w_cv1, w_cv2, w_proj_m, w_proj_n, w_out, g_ffn, w_group, b_group,
                 w_expert, b_expert, w_e1, w_e3, w_e2, *, use_pallas_proj):
    B, T, D = x.shape
    WB = win_past.shape[1]
    split_at = np.cumsum(SPLITS)[:-1].tolist()
    if use_pallas_proj:
        w_pad = jnp.concatenate(
            [w_in[:, _SEG[k][0]:_SEG[k][0] + _SEG[k][1]].astype(BF16) for k in _PROJ_ORDER]
            + [jnp.zeros((D, D_IN_PAD - D_IN), BF16)], axis=1)
        proj2d = norm_proj(x.reshape(B * T, D), g_mix, w_pad)
        seg = lambda k: proj2d[:, _PROJ_OFF[k]:_PROJ_OFF[k] + _SEG[k][1]].reshape(B, T, _SEG[k][1])
        mq, mk, mv, mi, mf, mo, nq, nkv, ng = (seg(k) for k in ("mq", "mk", "mv", "mi", "mf", "mo", "nq", "nkv", "ng"))
    else:
        proj = rms_norm(x, g_mix) @ w_in
        mq, mk, mv, mi, mf, mo, nq, nkv, ng, gate_m, gate_n = jnp.split(proj, split_at, axis=-1)
    qk, conv_state = short_conv(jnp.concatenate([mq, mk], axis=-1), conv_prev, w_conv)
    qk = jax.nn.silu(qk)
    q_m = qk[..., :M_QK].reshape(B, T, M_HEADS, M_DQK).astype(F32)
    k_m = qk[..., M_QK:].reshape(B, T, M_HEADS, M_DQK).astype(F32) * (M_DQK ** -0.5)
    v_m = mv.reshape(B, T, M_HEADS, M_DV).astype(F32)
    i_pre = (mi + b_i).astype(F32)
    logf = jax.nn.log_sigmoid((mf + b_f).astype(F32))
    h_m, C1, n1, m1 = mlstm_chunkwise(q_m, k_m, v_m, i_pre, logf, C0.astype(F32), n0.astype(F32), m0.astype(F32))
    h_m = rms_norm(h_m, g_mh).reshape(B, T, M_VW).astype(x.dtype) * jax.nn.sigmoid(mo)
    qn = rms_norm(nq.reshape(B, T, N_HEADS, HEAD_DIM), g_q)
    kv_new = nkv.reshape(B, T, 6, N_KV, HEAD_DIM)
    kv_rows = kv_new[:, :, :4]
    win_all = jnp.concatenate([win_past.astype(x.dtype), kv_new[:, :, 4:]], axis=1)
    if paged is None:
        h_n = nsa_prompt(nq, ng, kv_new, g_q, g_kc, g_ks, g_kw, pe_ck, w_ck1, w_ck2, pe_cv, w_cv1, w_cv2)
    else:
        gates_n = jax.nn.sigmoid(ng).reshape(B, T, N_HEADS, 3)
        h_n = nsa_sample(qn, gates_n, paged[0], paged[1], kv_rows, win_all, g_kc, g_ks, g_kw,
                         pe_ck, w_ck1, w_ck2, pe_cv, w_cv1, w_cv2)
    win_state = win_all[:, -min(WINDOW, WB + T):]
    if use_pallas_proj:
        assert _PROJ_OFF["gate_m"] == 0 and _PROJ_OFF["gate_n"] == D
        x1, h2, logits = out_proj(x.reshape(B * T, D), proj2d, (0, 1), h_m.reshape(B * T, M_VW),
                                  h_n.reshape(B * T, N_QW), w_proj_m, w_proj_n, w_out, g_ffn,
                                  w_group, b_group, w_expert, b_expert)
        x1 = x1.reshape(B, T, D)
        lg, le = logits[:, :N_GROUPS], logits[:, N_GROUPS:N_GROUPS + N_EXPERTS]
    else:
        u = jax.nn.sigmoid(gate_m) * (h_m @ w_proj_m) + jax.nn.sigmoid(gate_n) * (h_n.astype(x.dtype) @ w_proj_n)
        x1 = x + u @ w_out
        h2 = rms_norm(x1, g_ffn).reshape(B * T, D)
        lg, le = h2 @ w_group + b_group, h2 @ w_expert + b_expert
    eid, wts = moe_route(lg, le)
    return x1, h2.astype(BF16), eid, wts, kv_rows, win_state, C1, n1, m1, conv_state


def kernel(x_prompt, x_sample, cache_kv, page_table, state_win, state_C, state_n, state_m, state_conv, g_mix, w_in, w_conv, b_i, b_f, g_mh, g_q, g_kc, g_ks, g_kw, pe_ck, w_ck1, w_ck2, pe_cv, w_cv1, w_cv2, w_proj_m, w_proj_n, w_out, g_ffn, w_group, b_group, w_expert, b_expert, w_e1, w_e3, w_e2):
    B = x_prompt.shape[0]
    DB = x_sample.shape[0]
    n_pages = page_table.shape[1]
    dt = x_prompt.dtype
    weights = (g_mix, w_in, w_conv, b_i, b_f, g_mh, g_q, g_kc, g_ks, g_kw, pe_ck, w_ck1, w_ck2, pe_cv, w_cv1, w_cv2,
               w_proj_m, w_proj_n, w_out, g_ffn, w_group, b_group, w_expert, b_expert, w_e1, w_e3, w_e2)
    x1_p, h2_p, eid_p, wts_p, kv_p, win_p, C_p, n_p, m_p, conv_p = hybrid_layer(
        x_prompt, jnp.zeros((B, M_CONV - 1, 2 * M_QK), dt), jnp.zeros((B, M_HEADS, M_DQK, M_DV), F32),
        jnp.zeros((B, M_HEADS, M_DQK), F32), jnp.zeros((B, M_HEADS), F32),
        None, jnp.zeros((B, 0, 2, N_KV, HEAD_DIM), dt), *weights, use_pallas_proj=True)
    with jax.default_matmul_precision("highest"):
        x1_s, h2_s, eid_s, wts_s, kv_s, win_s, C_s, n_s, m_s, conv_s = hybrid_layer(
            x_sample, state_conv, state_C, state_n, state_m, (cache_kv, page_table), state_win, *weights,
            use_pallas_proj=False)
    n_p_tok = h2_p.shape[0]
    moe = hier_moe_apply(jnp.concatenate([h2_p, h2_s], axis=0), jnp.concatenate([eid_p, eid_s], axis=0),
                         jnp.concatenate([wts_p, wts_s], axis=0), w_e1, w_e3, w_e2)
    y_p = x1_p + moe[:n_p_tok].reshape(x1_p.shape)
    y_s = x1_s + moe[n_p_tok:].reshape(x1_s.shape)
    return (y_p, y_s, kv_p, kv_s, win_p, win_s, C_p, C_s, n_p, n_s, m_p, m_s, conv_p, conv_s)
```

```python
import functools
import math

import jax
import jax.numpy as jnp
import numpy as np
from jax import lax
from jax.experimental import pallas as pl
from jax.experimental.pallas import tpu as pltpu

D_MODEL = 2048
M_HEADS = 4
M_DQK = 128
M_DV = 256
M_CONV = 4
M_CHUNK = 64
N_HEADS = 16
N_KV = 4
HEAD_DIM = 64
CMP_BLK = 32
SEL_BLK = 64
N_SELECT = 16
WINDOW = 512
NSA_QBLK = 64
FORCE_SCORE = 1e4
N_GROUPS = 4
EXPERTS_PER_GROUP = 8
N_EXPERTS = N_GROUPS * EXPERTS_PER_GROUP
TOP_K_FINE = 2
MOE_BLK = 128
EPS = 1e-6
M_QK = M_HEADS * M_DQK
M_VW = M_HEADS * M_DV
N_QW = N_HEADS * HEAD_DIM
N_KVW = 6 * N_KV * HEAD_DIM
SPLITS = (M_QK, M_QK, M_VW, M_HEADS, M_HEADS, M_VW, N_QW, N_KVW, 3 * N_HEADS, D_MODEL, D_MODEL)
D_IN = sum(SPLITS)

F32 = jnp.float32
BF16 = jnp.bfloat16
VMEM_LIMIT = 48 * 1024 * 1024


def _mxu(a, b):
    return jnp.dot(a, b, preferred_element_type=F32, precision=lax.Precision.DEFAULT)


def _norm_proj_kernel(x_ref, g_ref, w_ref, o_ref, hx_ref):
    @pl.when(pl.program_id(1) == 0)
    def _():
        x = x_ref[...]
        r = lax.rsqrt(jnp.mean(x * x, axis=-1, keepdims=True) + EPS)
        hx_ref[...] = (x * r * g_ref[...]).astype(BF16)

    o_ref[...] = _mxu(hx_ref[...], w_ref[...])


def norm_proj(x, g, w_bf16, *, tm=1024, tn=896):
    n, d = x.shape
    k = w_bf16.shape[1]
    return pl.pallas_call(
        _norm_proj_kernel,
        grid=(n // tm, k // tn),
        in_specs=[
            pl.BlockSpec((tm, d), lambda i, j: (i, 0)),
            pl.BlockSpec((1, d), lambda i, j: (0, 0)),
            pl.BlockSpec((d, tn), lambda i, j: (0, j)),
        ],
        out_specs=pl.BlockSpec((tm, tn), lambda i, j: (i, j)),
        out_shape=jax.ShapeDtypeStruct((n, k), F32),
        scratch_shapes=[pltpu.VMEM((tm, d), BF16)],
        compiler_params=pltpu.CompilerParams(
            dimension_semantics=("parallel", "arbitrary"), vmem_limit_bytes=VMEM_LIMIT),
        name="norm_proj",
    )(x, g.reshape(1, d), w_bf16)


def _rms_rows_kernel(x_ref, g_ref, o_ref):
    x = x_ref[...]
    r = lax.rsqrt(jnp.mean(x * x, axis=-1, keepdims=True) + EPS)
    o_ref[...] = (x * r * g_ref[...]).astype(o_ref.dtype)


def rms_rows(x, g, out_dtype, *, tr=2048):
    rows, c = x.shape
    tr = min(tr, rows)
    return pl.pallas_call(
        _rms_rows_kernel,
        grid=(rows // tr,),
        in_specs=[pl.BlockSpec((tr, c), lambda i: (i, 0)), pl.BlockSpec((1, c), lambda i: (0, 0))],
        out_specs=pl.BlockSpec((tr, c), lambda i: (i, 0)),
        out_shape=jax.ShapeDtypeStruct((rows, c), out_dtype),
        compiler_params=pltpu.CompilerParams(dimension_semantics=("parallel",)),
        name="rms_rows",
    )(x, g.reshape(1, c))


def _split_bf16(x):
    hi = lax.bitcast_convert_type(
        lax.bitcast_convert_type(x, jnp.int32) & jnp.int32(-65536), F32)
    return hi.astype(BF16), (x - hi).astype(BF16)


def _dot_split(a, b_hi, b_lo):
    a_hi, a_lo = _split_bf16(a)
    return _mxu(a_hi, b_hi) + (_mxu(a_lo, b_hi) + _mxu(a_hi, b_lo))


def _compress_kernel(x_ref, pe_ref, w1_ref, w1l_ref, w2_ref, w2l_ref, g_ref, o_ref, *, normalize, precise):
    x = x_ref[...] + pe_ref[...]
    if precise:
        h = _dot_split(x, w1_ref[...], w1l_ref[...])
    else:
        h = _mxu(x.astype(BF16), w1_ref[...])
    h = h * jax.nn.sigmoid(h)
    if precise:
        y = _dot_split(h, w2_ref[...], w2l_ref[...])
    else:
        y = _mxu(h.astype(BF16), w2_ref[...])
    if normalize:
        y = y * lax.rsqrt(jnp.mean(y * y, axis=-1, keepdims=True) + EPS) * g_ref[...]
    o_ref[...] = y.astype(o_ref.dtype)


def compress_rows(x, pe, w1, w2, g, *, normalize, precise=False, tr=512):
    rows, k = x.shape
    f = w1.shape[-1]
    d = w2.shape[1]
    tr = min(tr, rows)
    w1_hi, w1_lo = _split_bf16(w1.reshape(k, f))
    w2_hi, w2_lo = _split_bf16(w2)
    if not precise:
        w1_hi, w2_hi = w1.reshape(k, f).astype(BF16), w2.astype(BF16)
    return pl.pallas_call(
        functools.partial(_compress_kernel, normalize=normalize, precise=precise),
        grid=(rows // tr,),
        in_specs=[
            pl.BlockSpec((tr, k), lambda i: (i, 0)),
            pl.BlockSpec((1, k), lambda i: (0, 0)),
            pl.BlockSpec((k, f), lambda i: (0, 0)),
            pl.BlockSpec((k, f), lambda i: (0, 0)),
            pl.BlockSpec((f, d), lambda i: (0, 0)),
            pl.BlockSpec((f, d), lambda i: (0, 0)),
            pl.BlockSpec((1, d), lambda i: (0, 0)),
        ],
        out_specs=pl.BlockSpec((tr, d), lambda i: (i, 0)),
        out_shape=jax.ShapeDtypeStruct((rows, d), F32 if precise else BF16),
        compiler_params=pltpu.CompilerParams(dimension_semantics=("parallel",)),
        name="nsa_compress",
    )(x, pe.reshape(1, k), w1_hi, w1_lo, w2_hi, w2_lo, g.reshape(1, d))


CMP_TILE_ROWS = 128


def _compress_paged_kernel(x_ref, pe_ref, w_hi_ref, w_lo_ref, w2_hi_ref, w2_lo_ref, g_ref, o_ref, *,
                           normalize, cpp, hidden):
    rows = x_ref.shape[0]
    hd = x_ref.shape[1]
    acc = jnp.zeros((rows, cpp * hidden), F32)
    for dp in range(hd // 2):
        a = x_ref[:, 2 * dp, :] + pe_ref[pl.ds(2 * dp, 1), :]
        b = x_ref[:, 2 * dp + 1, :] + pe_ref[pl.ds(2 * dp + 1, 1), :]
        acc = acc + _dot_split(jnp.concatenate([a, b], axis=1), w_hi_ref[dp], w_lo_ref[dp])
    hid = acc * jax.nn.sigmoid(acc)
    for c in range(cpp):
        y = _dot_split(hid[:, c * hidden:(c + 1) * hidden], w2_hi_ref[...], w2_lo_ref[...])
        if normalize:
            y = y * lax.rsqrt(jnp.mean(y * y, axis=-1, keepdims=True) + EPS) * g_ref[...]
        o_ref[:, c * hd:(c + 1) * hd] = y


def compress_paged(x_tiles, pe, w1, w2, g, *, normalize):
    r, hd, page = x_tiles.shape
    cpp = page // CMP_BLK
    hidden = w1.shape[-1]
    eye = jnp.eye(cpp, dtype=F32)
    w_big = jnp.einsum('cC,ldf->dclCf', eye, w1).reshape(hd // 2, 2 * page, cpp * hidden)
    w_hi, w_lo = _split_bf16(w_big)
    w2_hi, w2_lo = _split_bf16(w2)
    pe_t = jnp.tile(pe.T, (1, cpp))
    const3 = lambda i: (0, 0, 0)
    const2 = lambda i: (0, 0)
    return pl.pallas_call(
        functools.partial(_compress_paged_kernel, normalize=normalize, cpp=cpp, hidden=hidden),
        grid=(r // CMP_TILE_ROWS,),
        in_specs=[
            pl.BlockSpec((CMP_TILE_ROWS, hd, page), lambda i: (i, 0, 0)),
            pl.BlockSpec((hd, page), const2),
            pl.BlockSpec((hd // 2, 2 * page, cpp * hidden), const3, pipeline_mode=pl.Buffered(1)),
            pl.BlockSpec((hd // 2, 2 * page, cpp * hidden), const3, pipeline_mode=pl.Buffered(1)),
            pl.BlockSpec((hidden, hd), const2),
            pl.BlockSpec((hidden, hd), const2),
            pl.BlockSpec((1, hd), const2),
        ],
        out_specs=pl.BlockSpec((CMP_TILE_ROWS, cpp * hd), lambda i: (i, 0)),
        out_shape=jax.ShapeDtypeStruct((r, cpp * hd), F32),
        compiler_params=pltpu.CompilerParams(dimension_semantics=("parallel",), vmem_limit_bytes=VMEM_LIMIT),
        name="nsa_compress_paged",
    )(x_tiles, pe_t, w_hi, w_lo, w2_hi, w2_lo, g.reshape(1, hd))


NSA_TQ = 128
NSA_TK = 512
HPG = N_HEADS // N_KV
NSA_Q = HPG * NSA_TQ
NEG_BIG = -1e30


def _nsa_prompt_kernel(qT_ref, gq_ref, gate_ref, kc_ref, vcT_ref, ks_ref, vsT_ref, kw_ref, vwT_ref,
                       o_ref, sc_ref, sel_ref, imp_ref, *, n_cmp, n_sel):
    j = pl.program_id(2)
    s0 = j * NSA_TQ
    q = qT_ref[0, 0, 0]
    r = lax.rsqrt(jnp.mean(q * q, axis=0, keepdims=True) + EPS)
    qn = (q * r * gq_ref[...] * (HEAD_DIM ** -0.5 * math.log2(math.e))).astype(BF16)
    lane = lax.broadcasted_iota(jnp.int32, (1, NSA_Q), 1)
    t_row = s0 + (lane & (NSA_TQ - 1))

    s = _mxu(kc_ref[0, 0], qn)
    c_end = lax.broadcasted_iota(jnp.int32, (n_cmp, NSA_Q), 0) * CMP_BLK + (CMP_BLK - 1)
    mask_c = c_end <= t_row
    s = jnp.where(mask_c, s, -jnp.inf)
    m = jnp.max(s, axis=0, keepdims=True)
    m = jnp.where(m > -jnp.inf, m, 0.0)
    e = jnp.where(mask_c, jnp.exp2(s - m), 0.0)
    p_c = e / jnp.maximum(jnp.sum(e, axis=0, keepdims=True), 1e-30)
    o_c = _mxu(vcT_ref[0, 0], p_c.astype(BF16))

    imp_ref[...] = (p_c[:, 0:NSA_TQ] + p_c[:, NSA_TQ:2 * NSA_TQ]
                    + p_c[:, 2 * NSA_TQ:3 * NSA_TQ] + p_c[:, 3 * NSA_TQ:4 * NSA_TQ])
    imp = imp_ref[pl.ds(0, n_sel, stride=2), :] + imp_ref[pl.ds(1, n_sel, stride=2), :]
    n_iota = lax.broadcasted_iota(jnp.int32, (n_sel, NSA_TQ), 0)
    cur = (s0 + lax.broadcasted_iota(jnp.int32, (n_sel, NSA_TQ), 1)) // SEL_BLK
    valid = n_iota <= cur
    forced = (n_iota == 0) | (n_iota == cur) | (n_iota == cur - 1)
    score = jnp.where(valid, jnp.where(forced, FORCE_SCORE, imp), -jnp.inf)
    sc_ref[...] = score

    def rank_body(n2, rank):
        row = jnp.broadcast_to(sc_ref[pl.ds(n2, 1), :], (n_sel, NSA_TQ))
        beats = (row > score) | ((row == score) & (n2 < n_iota))
        return rank + jnp.where(beats, 1.0, 0.0)

    n_live = jnp.minimum((s0 + NSA_TQ - 1) // SEL_BLK + 1, n_sel)
    rank = lax.fori_loop(0, n_live, rank_body, jnp.zeros((n_sel, NSA_TQ), F32))
    sel_ref[...] = jnp.where(valid & (rank < N_SELECT), 0.0, NEG_BIG)

    key_iota = lax.broadcasted_iota(jnp.int32, (NSA_TK, NSA_TQ), 0)
    t_q = s0 + lax.broadcasted_iota(jnp.int32, (1, NSA_TQ), 1)

    def attend(k_ref, vT_ref, c_lo, c_hi, bias_fn):
        def body(c, carry):
            m_i, l_i, acc = carry
            k = k_ref[0, 0, pl.ds(pl.multiple_of(c * NSA_TK, NSA_TK), NSA_TK), :]
            sk = _mxu(k, qn)
            bias = bias_fn(c)
            ms, ls, ps = [], [], []
            for h in range(HPG):
                sl = slice(h * NSA_TQ, (h + 1) * NSA_TQ)
                s_h = sk[:, sl] + bias
                m_h = jnp.maximum(m_i[:, sl], jnp.max(s_h, axis=0, keepdims=True))
                p_h = jnp.exp2(s_h - m_h)
                ms.append(m_h)
                ls.append(jnp.sum(p_h, axis=0, keepdims=True))
                ps.append(p_h.astype(BF16))
            m_new = jnp.concatenate(ms, axis=1)
            alpha = jnp.exp2(m_i - m_new)
            l_new = alpha * l_i + jnp.concatenate(ls, axis=1)
            acc = alpha * acc + _mxu(vT_ref[0, 0, c], jnp.concatenate(ps, axis=1))
            return m_new, l_new, acc

        init = (jnp.full((1, NSA_Q), NEG_BIG, F32), jnp.zeros((1, NSA_Q), F32),
                jnp.zeros((HEAD_DIM, NSA_Q), F32))
        _, l_f, acc_f = lax.fori_loop(c_lo, c_hi, body, init)
        return acc_f / jnp.maximum(l_f, 1e-30)

    def sel_bias(c):
        pos = c * NSA_TK + key_iota
        per_blk = [jnp.broadcast_to(
            sel_ref[pl.ds(jnp.minimum(c * (NSA_TK // SEL_BLK) + i, n_sel - 1), 1), :], (SEL_BLK, NSA_TQ))
            for i in range(NSA_TK // SEL_BLK)]
        return jnp.where(pos <= t_q, jnp.concatenate(per_blk, axis=0), NEG_BIG)

    def win_bias(c):
        pos = c * NSA_TK + key_iota
        return jnp.where((pos <= t_q) & (pos > t_q - WINDOW), 0.0, NEG_BIG)

    c_hi = (s0 + NSA_TQ + NSA_TK - 1) // NSA_TK
    o_s = attend(ks_ref, vsT_ref, 0, c_hi, sel_bias)
    o_w = attend(kw_ref, vwT_ref, jnp.maximum(s0 - (WINDOW - 1), 0) // NSA_TK, c_hi, win_bias)

    g = jax.nn.sigmoid(gate_ref[0, 0, 0])
    o_ref[0, 0, 0] = g[0:1] * o_c + g[1:2] * o_s + g[2:3] * o_w


def nsa_prompt(nq, ng, kv_new, g_q, g_kc, g_ks, g_kw, pe_ck, w_ck1, w_ck2, pe_cv, w_cv1, w_cv2):
    B, T = nq.shape[:2]
    G = N_KV
    nqb = T // NSA_TQ
    n_cmp = T // CMP_BLK
    n_sel = T // SEL_BLK
    nch = T // NSA_TK
    qT = nq.reshape(B, nqb, NSA_TQ, G, HPG, HEAD_DIM).transpose(0, 3, 1, 5, 4, 2).reshape(B, G, nqb, HEAD_DIM, NSA_Q)
    gT = ng.reshape(B, nqb, NSA_TQ, G, HPG, 3).transpose(0, 3, 1, 5, 4, 2).reshape(B, G, nqb, 3, NSA_Q)
    kv_g = kv_new.transpose(2, 0, 3, 1, 4)
    xk = kv_g[0].reshape(B * G * n_cmp, CMP_BLK * HEAD_DIM)
    xv = kv_g[1].reshape(B * G * n_cmp, CMP_BLK * HEAD_DIM)
    kc = compress_rows(xk, pe_ck, w_ck1, w_ck2, g_kc, normalize=True).reshape(B, G, n_cmp, HEAD_DIM)
    vc = compress_rows(xv, pe_cv, w_cv1, w_cv2, g_kc, normalize=False).reshape(B, G, n_cmp, HEAD_DIM)
    vcT = vc.transpose(0, 1, 3, 2)
    ks = rms_rows(kv_g[2].reshape(B * G * T, HEAD_DIM), g_ks, BF16).reshape(B, G, T, HEAD_DIM)
    kw = rms_rows(kv_g[4].reshape(B * G * T, HEAD_DIM), g_kw, BF16).reshape(B, G, T, HEAD_DIM)
    vsT = kv_g[3].astype(BF16).reshape(B, G, nch, NSA_TK, HEAD_DIM).transpose(0, 1, 2, 4, 3)
    vwT = kv_g[5].astype(BF16).reshape(B, G, nch, NSA_TK, HEAD_DIM).transpose(0, 1, 2, 4, 3)

    bg = lambda b, g, j: (b, g, 0, 0)
    bg5 = lambda b, g, j: (b, g, 0, 0, 0)
    bgj = lambda b, g, j: (b, g, j, 0, 0)
    oT = pl.pallas_call(
        functools.partial(_nsa_prompt_kernel, n_cmp=n_cmp, n_sel=n_sel),
        grid=(B, G, nqb),
        in_specs=[
            pl.BlockSpec((1, 1, 1, HEAD_DIM, NSA_Q), bgj),
            pl.BlockSpec((HEAD_DIM, 1), lambda b, g, j: (0, 0)),
            pl.BlockSpec((1, 1, 1, 3, NSA_Q), bgj),
            pl.BlockSpec((1, 1, n_cmp, HEAD_DIM), bg),
            pl.BlockSpec((1, 1, HEAD_DIM, n_cmp), bg),
            pl.BlockSpec((1, 1, T, HEAD_DIM), bg),
            pl.BlockSpec((1, 1, nch, HEAD_DIM, NSA_TK), bg5),
            pl.BlockSpec((1, 1, T, HEAD_DIM), bg),
            pl.BlockSpec((1, 1, nch, HEAD_DIM, NSA_TK), bg5),
        ],
        out_specs=pl.BlockSpec((1, 1, 1, HEAD_DIM, NSA_Q), bgj),
        out_shape=jax.ShapeDtypeStruct((B, G, nqb, HEAD_DIM, NSA_Q), F32),
        scratch_shapes=[pltpu.VMEM((n_sel, NSA_TQ), F32), pltpu.VMEM((n_sel, NSA_TQ), F32),
                        pltpu.VMEM((n_cmp, NSA_TQ), F32)],
        compiler_params=pltpu.CompilerParams(
            dimension_semantics=("parallel", "parallel", "arbitrary"), vmem_limit_bytes=VMEM_LIMIT),
        name="nsa_prompt",
    )(qT, g_q.reshape(HEAD_DIM, 1), gT, kc, vcT, ks, vsT, kw, vwT)
    return oT.reshape(B, G, nqb, HEAD_DIM, HPG, NSA_TQ).transpose(0, 2, 5, 1, 4, 3).reshape(B, T, N_QW)


def rms_norm(x, g):
    xf = x.astype(F32)
    y = xf * lax.rsqrt(jnp.mean(xf * xf, axis=-1, keepdims=True) + EPS)
    return (y * g.astype(F32)).astype(x.dtype)


def masked_softmax(s, mask):
    s = jnp.where(mask, s.astype(F32), -jnp.inf)
    m = jnp.max(s, axis=-1, keepdims=True)
    m = jnp.where(jnp.isfinite(m), m, 0.0)
    e = jnp.where(mask, jnp.exp(s - m), 0.0)
    return e / jnp.maximum(jnp.sum(e, axis=-1, keepdims=True), 1e-30)


def short_conv(u, prev, w):
    T = u.shape[1]
    full = jnp.concatenate([prev.astype(u.dtype), u], axis=1)
    out = full[:, 0:T] * w[0]
    for j in range(1, M_CONV):
        out = out + full[:, j:j + T] * w[j]
    return out, full[:, T:]


def mlstm_chunkwise(q, k, v, i_pre, logf, C0, n0, m0):
    B, T = q.shape[:2]
    L = M_CHUNK if T % M_CHUNK == 0 else T
    nc = T // L

    def chunks(a):
        a = a.reshape((B, nc, L) + a.shape[2:])
        return jnp.moveaxis(jnp.moveaxis(a, 3, 2), 1, 0)

    causal = jnp.tril(jnp.ones((L, L), bool))

    def step(carry, xs):
        C, n, m = carry
        qc, kc, vc, ic, fc = xs
        b = jnp.cumsum(fc, axis=-1)
        logD = jnp.where(causal, b[..., :, None] - b[..., None, :] + ic[..., None, :], -jnp.inf)
        inter = b + m[..., None]
        m_t = jnp.maximum(inter, jnp.max(logD, axis=-1))
        a = jnp.exp(inter - m_t)
        S = jnp.einsum('bhtd,bhsd->bhts', qc, kc) * jnp.exp(logD - m_t[..., None])
        num = a[..., None] * jnp.einsum('bhtd,bhde->bhte', qc, C) + jnp.einsum('bhts,bhse->bhte', S, vc)
        den = a * jnp.einsum('bhtd,bhd->bht', qc, n) + jnp.sum(S, axis=-1)
        h = num / jnp.maximum(jnp.abs(den), jnp.exp(-m_t))[..., None]
        m_new = m_t[..., -1]
        w = jnp.exp(b[..., -1:] - b + ic - m_new[..., None])
        aL = jnp.exp(b[..., -1] + m - m_new)
        C_new = aL[..., None, None] * C + jnp.einsum('bhs,bhsd,bhse->bhde', w, kc, vc)
        n_new = aL[..., None] * n + jnp.einsum('bhs,bhsd->bhd', w, kc)
        return (C_new, n_new, m_new), h

    (C, n, m), h = lax.scan(step, (C0, n0, m0), (chunks(q), chunks(k), chunks(v), chunks(i_pre), chunks(logf)))
    h = jnp.moveaxis(jnp.moveaxis(h, 0, 1), 2, 3).reshape(B, T, M_HEADS, M_DV)
    return h, C, n, m


def compress(rows, pe, w1, w2):
    B, Lk = rows.shape[:2]
    nc = Lk // CMP_BLK
    blk = rows[:, :nc * CMP_BLK].reshape(B, nc, CMP_BLK, N_KV, HEAD_DIM) + pe[:, None, :]
    hid = jax.nn.silu(jnp.einsum('bclgd,ldf->bcgf', blk, w1))
    return jnp.einsum('bcgf,fd->bcgd', hid, w2)


def nsa_attention(q, gates, kv_all, kw_full, q0, g_kc, g_ks, g_kw, pe_ck, w_ck1, w_ck2, pe_cv, w_cv1, w_cv2):
    B, T = q.shape[:2]
    Lk = kv_all.shape[1]
    hpg = N_HEADS // N_KV
    scale = HEAD_DIM ** -0.5
    kc = rms_norm(compress(kv_all[:, :, 0], pe_ck, w_ck1, w_ck2), g_kc)
    vc = compress(kv_all[:, :, 1], pe_cv, w_cv1, w_cv2)
    nc = kc.shape[1]
    n_sel = -(-Lk // SEL_BLK)
    pad = n_sel * SEL_BLK - Lk
    ks = jnp.pad(rms_norm(kv_all[:, :, 2], g_ks), ((0, 0), (0, pad), (0, 0), (0, 0)))
    ks = ks.reshape(B, n_sel, SEL_BLK, N_KV, HEAD_DIM).transpose(0, 3, 1, 2, 4)
    vs = jnp.pad(kv_all[:, :, 3], ((0, 0), (0, pad), (0, 0), (0, 0)))
    vs = vs.reshape(B, n_sel, SEL_BLK, N_KV, HEAD_DIM).transpose(0, 3, 1, 2, 4)
    kw = rms_norm(kw_full[:, :, 0], g_kw)
    vw = kw_full[:, :, 1]
    top = min(N_SELECT, n_sel)
    ratio = SEL_BLK // CMP_BLK
    qb = NSA_QBLK if T % NSA_QBLK == 0 else T
    nqb = T // qb
    qg = q.reshape(B, nqb, qb, N_KV, hpg, HEAD_DIM).transpose(1, 0, 3, 4, 2, 5)
    gg = jnp.moveaxis(gates.reshape(B, nqb, qb, N_HEADS, 3), 1, 0)
    cmp_end = (jnp.arange(nc) + 1) * CMP_BLK - 1
    blk_ids = jnp.arange(n_sel)
    b_ix = jnp.arange(B)[:, None, None, None]
    g_ix = jnp.arange(N_KV)[None, :, None, None]

    def block(args):
        qi, gi, j = args
        s0 = j * qb
        t = q0 + s0 + jnp.arange(qb)
        sc = jnp.einsum('bghqd,bcgd->bghqc', qi, kc) * scale
        p_c = masked_softmax(sc, cmp_end[None, :] <= t[:, None])
        o_c = jnp.einsum('bghqc,bcgd->bghqd', p_c.astype(vc.dtype), vc)
        imp = jnp.sum(p_c, axis=2)
        imp = jnp.pad(imp, ((0, 0), (0, 0), (0, 0), (0, n_sel * ratio - nc)))
        imp = imp.reshape(B, N_KV, qb, n_sel, ratio).sum(-1)
        cur = t // SEL_BLK
        valid = blk_ids[None, :] <= cur[:, None]
        forced = (blk_ids[None, :] == 0) | (blk_ids[None, :] == cur[:, None]) | (blk_ids[None, :] == cur[:, None] - 1)
        imp = jnp.where(valid, jnp.where(forced, FORCE_SCORE, imp), -jnp.inf)
        top_v, top_i = lax.top_k(imp, top)
        k_sel = ks[b_ix, g_ix, top_i]
        v_sel = vs[b_ix, g_ix, top_i]
        pos = top_i[..., None] * SEL_BLK + jnp.arange(SEL_BLK)
        m_s = (pos <= t[:, None, None]) & jnp.isfinite(top_v)[..., None]
        ss = (jnp.einsum('bghqd,bgqnkd->bghqnk', qi, k_sel) * scale).reshape(B, N_KV, hpg, qb, top * SEL_BLK)
        p_s = masked_softmax(ss, m_s.reshape(B, N_KV, 1, qb, top * SEL_BLK))
        o_s = jnp.einsum('bghqnk,bgqnkd->bghqd', p_s.reshape(B, N_KV, hpg, qb, top, SEL_BLK).astype(v_sel.dtype), v_sel)
        kwi = lax.dynamic_slice_in_dim(kw, s0, qb + WINDOW, axis=1)
        vwi = lax.dynamic_slice_in_dim(vw, s0, qb + WINDOW, axis=1)
        pw = q0 - WINDOW + s0 + jnp.arange(qb + WINDOW)
        m_w = (pw[None, :] <= t[:, None]) & (pw[None, :] > t[:, None] - WINDOW) & (pw[None, :] >= 0)
        sw = jnp.einsum('bghqd,bkgd->bghqk', qi, kwi) * scale
        p_w = masked_softmax(sw, m_w)
        o_w = jnp.einsum('bghqk,bkgd->bghqd', p_w.astype(vwi.dtype), vwi)
        o = jnp.stack([o_c, o_s, o_w], axis=-1)
        gi_r = gi.reshape(B, qb, N_KV, hpg, 3).transpose(0, 2, 3, 1, 4)
        return jnp.einsum('bghqdc,bghqc->bqghd', o, gi_r.astype(o.dtype))

    out = lax.map(block, (qg, gg, jnp.arange(nqb)))
    return jnp.moveaxis(out, 0, 1).reshape(B, T, N_QW)


def nsa_sample(q, gates, cache_kv, page_table, kv_rows, win_all, g_kc, g_ks, g_kw,
               pe_ck, w_ck1, w_ck2, pe_cv, w_cv1, w_cv2):
    B, T = q.shape[:2]
    n_pages = page_table.shape[1]
    page = cache_kv.shape[1]
    P = n_pages * page
    G, hd = N_KV, HEAD_DIM
    assert T < CMP_BLK and T <= SEL_BLK and P % SEL_BLK == 0 and page % SEL_BLK == 0
    assert win_all.shape[1] == WINDOW + T
    nc = P // CMP_BLK
    cpp = page // CMP_BLK
    n_last = P // SEL_BLK
    n_sel = n_last + 1
    top = min(N_SELECT, n_sel)
    ratio = SEL_BLK // CMP_BLK
    scale = hd ** -0.5
    t = P + jnp.arange(T)
    qg = q.reshape(B, T, G, HPG, hd)

    cache_t = cache_kv.transpose(0, 2, 3, 4, 1)
    flat_pages = page_table.reshape(-1)

    def cmp_tiles(kind):
        tiles = jax.vmap(lambda p: lax.dynamic_slice(cache_t, (p, kind, 0, 0, 0), (1, 1, G, hd, page))[0, 0])(flat_pages)
        return tiles.reshape(B * n_pages * G, hd, page)

    def cmp_rows(y):
        return y.reshape(B, n_pages, G, cpp, hd).transpose(0, 2, 1, 3, 4).reshape(B, G, nc, hd)

    kc = cmp_rows(compress_paged(cmp_tiles(0), pe_ck, w_ck1, w_ck2, g_kc, normalize=True))
    vc = cmp_rows(compress_paged(cmp_tiles(1), pe_cv, w_cv1, w_cv2, g_kc, normalize=False))
    cmp_end = (jnp.arange(nc) + 1) * CMP_BLK - 1
    sc = jnp.einsum('btghd,bgcd->bghtc', qg, kc) * scale
    p_c = masked_softmax(sc, cmp_end[None, :] <= t[:, None])
    o_c = jnp.einsum('bghtc,bgcd->bghtd', p_c, vc)

    imp = jnp.sum(p_c, axis=2)
    imp = jnp.pad(imp, ((0, 0), (0, 0), (0, 0), (0, n_sel * ratio - nc)))
    imp = imp.reshape(B, G, T, n_sel, ratio).sum(-1)
    blk_ids = jnp.arange(n_sel)
    cur = t // SEL_BLK
    valid = blk_ids[None, :] <= cur[:, None]
    forced = (blk_ids[None, :] == 0) | (blk_ids[None, :] == cur[:, None]) | (blk_ids[None, :] == cur[:, None] - 1)
    imp = jnp.where(valid, jnp.where(forced, FORCE_SCORE, imp), -jnp.inf)
    ids = jnp.arange(n_sel)
    rest, top_v, top_i = imp, [], []
    for _ in range(top):
        i_max = jnp.argmax(rest, axis=-1)
        top_i.append(i_max)
        top_v.append(jnp.max(rest, axis=-1))
        rest = jnp.where(ids == i_max[..., None], -jnp.inf, rest)
    top_v, top_i = jnp.stack(top_v, axis=-1), jnp.stack(top_i, axis=-1)

    bpp = page // SEL_BLK
    blk_c = jnp.minimum(top_i, n_last - 1)
    b_ix = jnp.arange(B)[:, None, None, None]
    phys = page_table[b_ix, blk_c // bpp]
    g_ix = jnp.broadcast_to(jnp.arange(G)[None, :, None, None], top_i.shape)

    def take_tile(p, g):
        return lax.dynamic_slice(cache_t, (p, 2, g, 0, 0), (1, 2, 1, hd, page))[0, :, 0]

    tiles = jax.vmap(take_tile)(phys.reshape(-1), g_ix.reshape(-1)).reshape(B, G, T, top, 2, hd, page)
    new_tile = jnp.pad(kv_rows[:, :, 2:4], ((0, 0), (0, page - T), (0, 0), (0, 0), (0, 0)))
    new_tile = new_tile.transpose(0, 3, 2, 4, 1)[:, :, None, None]
    is_new = top_i == n_last
    tiles = jnp.where(is_new[..., None, None, None], new_tile, tiles)
    k_t = tiles[..., 0, :, :]
    v_t = tiles[..., 1, :, :]
    k_t = k_t * lax.rsqrt(jnp.mean(k_t * k_t, axis=-2, keepdims=True) + EPS) * g_ks[:, None]
    lane = jnp.arange(page)
    tile0 = jnp.where(is_new, n_last, blk_c // bpp * bpp)
    pos = tile0[..., None] * SEL_BLK + lane
    m_s = ((pos // SEL_BLK == top_i[..., None]) & (pos <= t[:, None, None])
           & jnp.isfinite(top_v)[..., None])
    ss = (jnp.einsum('btghd,bgtndp->bghtnp', qg, k_t) * scale).reshape(B, G, HPG, T, top * page)
    p_s = masked_softmax(ss, m_s.reshape(B, G, 1, T, top * page))
    o_s = jnp.einsum('bghtnp,bgtndp->bghtd', p_s.reshape(B, G, HPG, T, top, page), v_t)

    kw = rms_norm(win_all[:, :, 0], g_kw)
    vw = win_all[:, :, 1]
    pw = P - WINDOW + jnp.arange(T + WINDOW)
    m_w = (pw[None, :] <= t[:, None]) & (pw[None, :] > t[:, None] - WINDOW) & (pw[None, :] >= 0)
    sw = jnp.einsum('btghd,bkgd->bghtk', qg, kw) * scale
    p_w = masked_softmax(sw, m_w)
    o_w = jnp.einsum('bghtk,bkgd->bghtd', p_w, vw)

    o = jnp.stack([o_c, o_s, o_w], axis=-1)
    return jnp.einsum('bghtdc,btghc->btghd', o, gates.reshape(B, T, G, HPG, 3)).reshape(B, T, N_QW)


def moe_route(group_logits, expert_logits):
    assert TOP_K_FINE == 2
    N = group_logits.shape[0]
    pg = jax.nn.softmax(group_logits.astype(F32), axis=-1)
    grp = jnp.argmax(pg, axis=-1)
    p_grp = jnp.max(pg, axis=-1)
    le = expert_logits.astype(F32).reshape(N, N_GROUPS, EXPERTS_PER_GROUP)
    pe = jax.nn.softmax(le[jnp.arange(N), grp], axis=-1)
    i1 = jnp.argmax(pe, axis=-1)
    hit1 = jnp.arange(EXPERTS_PER_GROUP)[None, :] == i1[:, None]
    rest = jnp.where(hit1, -jnp.inf, pe)
    i2 = jnp.argmax(rest, axis=-1)
    top_i = jnp.stack([i1, i2], axis=-1)
    top_p = jnp.stack([jnp.max(pe, axis=-1), jnp.max(rest, axis=-1)], axis=-1)
    wts = p_grp[:, None] * top_p / jnp.sum(top_p, axis=-1, keepdims=True)
    eid = grp[:, None] * EXPERTS_PER_GROUP + top_i
    return eid.astype(jnp.int32), wts


MOE_TM = 256


def _moe_ffn_kernel(be_ref, x_ref, w1_ref, w3_ref, w2_ref, wt_ref, o_ref, w1b_ref, w3b_ref, w2b_ref):
    i = pl.program_id(0)

    @pl.when((i == 0) | (be_ref[i] != be_ref[jnp.maximum(i - 1, 0)]))
    def _():
        w1b_ref[...] = w1_ref[0].astype(BF16)
        w3b_ref[...] = w3_ref[0].astype(BF16)
        w2b_ref[...] = w2_ref[0].astype(BF16)

    x = x_ref[...]
    a = _mxu(x, w1b_ref[...])
    b = _mxu(x, w3b_ref[...])
    h = (a * jax.nn.sigmoid(a) * b).astype(BF16)
    o_ref[...] = _mxu(h, w2b_ref[...]) * wt_ref[...]


def moe_ffn(rows, row_wt, blk_e, w1, w3, w2):
    m_pad, d = rows.shape
    n_blk = m_pad // MOE_TM
    f = w1.shape[2]
    return pl.pallas_call(
        _moe_ffn_kernel,
        grid_spec=pltpu.PrefetchScalarGridSpec(
            num_scalar_prefetch=1,
            grid=(n_blk,),
            in_specs=[
                pl.BlockSpec((MOE_TM, d), lambda i, be: (i, 0)),
                pl.BlockSpec((1, d, f), lambda i, be: (be[i], 0, 0)),
                pl.BlockSpec((1, d, f), lambda i, be: (be[i], 0, 0)),
                pl.BlockSpec((1, f, d), lambda i, be: (be[i], 0, 0)),
                pl.BlockSpec((MOE_TM, 1), lambda i, be: (i, 0)),
            ],
            out_specs=pl.BlockSpec((MOE_TM, d), lambda i, be: (i, 0)),
            scratch_shapes=[pltpu.VMEM((d, f), BF16), pltpu.VMEM((d, f), BF16), pltpu.VMEM((f, d), BF16)],
        ),
        out_shape=jax.ShapeDtypeStruct((m_pad, d), F32),
        compiler_params=pltpu.CompilerParams(
            dimension_semantics=("arbitrary",), vmem_limit_bytes=VMEM_LIMIT),
        name="moe_ffn",
    )(blk_e, rows, w1, w3, w2, row_wt)


def hier_moe_apply(x, eid, wts, w_e1, w_e3, w_e2):
    N, D = x.shape
    M = N * TOP_K_FINE
    eid_f = eid.reshape(-1)
    onehot = (eid_f[:, None] == jnp.arange(N_EXPERTS, dtype=jnp.int32)[None, :]).astype(jnp.int32)
    run = jnp.cumsum(onehot, axis=0)
    counts = run[-1]
    rank_in_e = jnp.sum(run * onehot, axis=1) - 1
    padded = (counts + MOE_TM - 1) // MOE_TM * MOE_TM
    p_end = jnp.cumsum(padded)
    p_start = p_end - padded
    pos = (p_start[eid_f] + rank_in_e).astype(jnp.int32)
    n_blk = (M + N_EXPERTS * (MOE_TM - 1) + MOE_TM - 1) // MOE_TM
    m_pad = n_blk * MOE_TM
    src_tok = jnp.zeros((m_pad,), jnp.int32).at[pos].set(jnp.arange(M, dtype=jnp.int32) // TOP_K_FINE)
    row_wt = jnp.zeros((m_pad,), F32).at[pos].set(wts.reshape(-1))
    blk_e = jnp.minimum(jnp.searchsorted(p_end, jnp.arange(n_blk, dtype=jnp.int32) * MOE_TM, side='right'),
                        N_EXPERTS - 1).astype(jnp.int32)
    rows = x[src_tok].astype(BF16)
    out = moe_ffn(rows, row_wt.reshape(m_pad, 1), blk_e, w_e1, w_e3, w_e2)
    pos = pos.reshape(N, TOP_K_FINE)
    return out[pos[:, 0]] + out[pos[:, 1]]


OUT_TM = 256
ROUTER_PAD = 128


def _out_proj_kernel(x_ref, gm_ref, gn_ref, hm_ref, hn_ref, wm_ref, wn_ref, wo_ref, gf_ref, wr_ref, br_ref,
                     x1_ref, h2_ref, lg_ref):
    pm = _mxu(hm_ref[...].astype(BF16), wm_ref[...])
    pn = _mxu(hn_ref[...].astype(BF16), wn_ref[...])
    u = jax.nn.sigmoid(gm_ref[...]) * pm + jax.nn.sigmoid(gn_ref[...]) * pn
    x1 = x_ref[...] + _mxu(u.astype(BF16), wo_ref[...])
    x1_ref[...] = x1
    h2 = x1 * lax.rsqrt(jnp.mean(x1 * x1, axis=-1, keepdims=True) + EPS) * gf_ref[...]
    h2_ref[...] = h2
    lg_ref[...] = _mxu(h2.astype(BF16), wr_ref[...]) + br_ref[...]


def out_proj(x, proj, gate_col_blocks, h_m, h_n, w_proj_m, w_proj_n, w_out, g_ffn, w_group, b_group, w_expert, b_expert):
    n, d = x.shape
    n_r = N_GROUPS + N_EXPERTS
    w_r = jnp.pad(jnp.concatenate([w_group, w_expert], axis=1), ((0, 0), (0, ROUTER_PAD - n_r))).astype(BF16)
    b_r = jnp.pad(jnp.concatenate([b_group, b_expert]), (0, ROUTER_PAD - n_r)).reshape(1, ROUTER_PAD)
    cm, cn = gate_col_blocks
    row = lambda i: (i, 0)
    const = lambda i: (0, 0)
    once = pl.Buffered(1)
    return pl.pallas_call(
        _out_proj_kernel,
        grid=(n // OUT_TM,),
        in_specs=[
            pl.BlockSpec((OUT_TM, d), row),
            pl.BlockSpec((OUT_TM, d), lambda i: (i, cm)),
            pl.BlockSpec((OUT_TM, d), lambda i: (i, cn)),
            pl.BlockSpec((OUT_TM, h_m.shape[1]), row),
            pl.BlockSpec((OUT_TM, h_n.shape[1]), row),
            pl.BlockSpec(w_proj_m.shape, const, pipeline_mode=once),
            pl.BlockSpec(w_proj_n.shape, const, pipeline_mode=once),
            pl.BlockSpec(w_out.shape, const, pipeline_mode=once),
            pl.BlockSpec((1, d), const),
            pl.BlockSpec((d, ROUTER_PAD), const),
            pl.BlockSpec((1, ROUTER_PAD), const),
        ],
        out_specs=[pl.BlockSpec((OUT_TM, d), row), pl.BlockSpec((OUT_TM, d), row),
                   pl.BlockSpec((OUT_TM, ROUTER_PAD), row)],
        out_shape=[jax.ShapeDtypeStruct((n, d), F32), jax.ShapeDtypeStruct((n, d), F32),
                   jax.ShapeDtypeStruct((n, ROUTER_PAD), F32)],
        compiler_params=pltpu.CompilerParams(dimension_semantics=("parallel",), vmem_limit_bytes=VMEM_LIMIT),
        name="out_proj",
    )(x, proj, proj, h_m, h_n, w_proj_m.astype(BF16), w_proj_n.astype(BF16), w_out.astype(BF16),
      g_ffn.reshape(1, d), w_r, b_r)


_SEG = dict(zip(("mq", "mk", "mv", "mi", "mf", "mo", "nq", "nkv", "ng", "gate_m", "gate_n"),
                zip(np.cumsum((0,) + SPLITS[:-1]).tolist(), SPLITS)))
_PROJ_ORDER = ("gate_m", "gate_n", "mv", "mo", "nq", "mq", "mk", "nkv", "mi", "mf", "ng")
_PROJ_OFF = dict(zip(_PROJ_ORDER, np.cumsum([0] + [_SEG[k][1] for k in _PROJ_ORDER[:-1]]).tolist()))
D_IN_PAD = -(-D_IN // 896) * 896


def hybrid_layer(x, conv_prev, C0, n0, m0, paged, win_past, g_mix, w_in, w_conv, b_i, b_f, g_mh, g_q, g_kc, g_ks, g_kw,
                 pe_ck, w_ck1, w_ck2, pe_cv, w_cv1, w_cv2, w_proj_m, w_proj_n, w_out, g_ffn, w_group, b_group,
                 w_expert, b_expert, w_e1, w_e3, w_e2, *, use_pallas_proj):
    B, T, D = x.shape
    WB = win_past.shape[1]
    split_at = np.cumsum(SPLITS)[:-1].tolist()
    if use_pallas_proj:
        w_pad = jnp.concatenate(
            [w_in[:, _SEG[k][0]:_SEG[k][0] + _SEG[k][1]].astype(BF16) for k in _PROJ_ORDER]
            + [jnp.zeros((D, D_IN_PAD - D_IN), BF16)], axis=1)
        proj2d = norm_proj(x.reshape(B * T, D), g_mix, w_pad)
        seg = lambda k: proj2d[:, _PROJ_OFF[k]:_PROJ_OFF[k] + _SEG[k][1]].reshape(B, T, _SEG[k][1])
        mq, mk, mv, mi, mf, mo, nq, nkv, ng = (seg(k) for k in ("mq", "mk", "mv", "mi", "mf", "mo", "nq", "nkv", "ng"))
    else:
        proj = rms_norm(x, g_mix) @ w_in
        mq, mk, mv, mi, mf, mo, nq, nkv, ng, gate_m, gate_n = jnp.split(proj, split_at, axis=-1)
    qk, conv_state = short_conv(jnp.concatenate([mq, mk], axis=-1), conv_prev, w_conv)
    qk = jax.nn.silu(qk)
    q_m = qk[..., :M_QK].reshape(B, T, M_HEADS, M_DQK).astype(F32)
    k_m = qk[..., M_QK:].reshape(B, T, M_HEADS, M_DQK).astype(F32) * (M_DQK ** -0.5)
    v_m = mv.reshape(B, T, M_HEADS, M_DV).astype(F32)
    i_pre = (mi + b_i).astype(F32)
    logf = jax.nn.log_sigmoid((mf + b_f).astype(F32))
    h_m, C1, n1, m1 = mlstm_chunkwise(q_m, k_m, v_m, i_pre, logf, C0.astype(F32), n0.astype(F32), m0.astype(F32))
    h_m = rms_norm(h_m, g_mh).reshape(B, T, M_VW).astype(x.dtype) * jax.nn.sigmoid(mo)
    qn = rms_norm(nq.reshape(B, T, N_HEADS, HEAD_DIM), g_q)
    kv_new = nkv.reshape(B, T, 6, N_KV, HEAD_DIM)
    kv_rows = kv_new[:, :, :4]
    win_all = jnp.concatenate([win_past.astype(x.dtype), kv_new[:, :, 4:]], axis=1)
    if paged is None:
        h_n = nsa_prompt(nq, ng, kv_new, g_q, g_kc, g_ks, g_kw, pe_ck, w_ck1, w_ck2, pe_cv, w_cv1, w_cv2)
    else:
        gates_n = jax.nn.sigmoid(ng).reshape(B, T, N_HEADS, 3)
        h_n = nsa_sample(qn, gates_n, paged[0], paged[1], kv_rows, win_all, g_kc, g_ks, g_kw,
                         pe_ck, w_ck1, w_ck2, pe_cv, w_cv1, w_cv2)
    win_state = win_all[:, -min(WINDOW, WB + T):]
    if use_pallas_proj:
        assert _PROJ_OFF["gate_m"] == 0 and _PROJ_OFF["gate_n"] == D
        x1, h2, logits = out_proj(x.reshape(B * T, D), proj2d, (0, 1), h_m.reshape(B * T, M_VW),
                                  h_n.reshape(B * T, N_QW), w_proj_m, w_proj_n, w_out, g_ffn,
                                  w_group, b_group, w_expert, b_expert)
        x1 = x1.reshape(B, T, D)
        lg, le = logits[:, :N_GROUPS], logits[:, N_GROUPS:N_GROUPS + N_EXPERTS]
    else:
        u = jax.nn.sigmoid(gate_m) * (h_m @ w_proj_m) + jax.nn.sigmoid(gate_n) * (h_n.astype(x.dtype) @ w_proj_n)
        x1 = x + u @ w_out
        h2 = rms_norm(x1, g_ffn).reshape(B * T, D)
        lg, le = h2 @ w_group + b_group, h2 @ w_expert + b_expert
    eid, wts = moe_route(lg, le)
    return x1, h2, eid, wts, kv_rows, win_state, C1, n1, m1, conv_state


def kernel(x_prompt, x_sample, cache_kv, page_table, state_win, state_C, state_n, state_m, state_conv, g_mix, w_in, w_conv, b_i, b_f, g_mh, g_q, g_kc, g_ks, g_kw, pe_ck, w_ck1, w_ck2, pe_cv, w_cv1, w_cv2, w_proj_m, w_proj_n, w_out, g_ffn, w_group, b_group, w_expert, b_expert, w_e1, w_e3, w_e2):
    B = x_prompt.shape[0]
    dt = x_prompt.dtype
    weights = (g_mix, w_in, w_conv, b_i, b_f, g_mh, g_q, g_kc, g_ks, g_kw, pe_ck, w_ck1, w_ck2, pe_cv, w_cv1, w_cv2,
               w_proj_m, w_proj_n, w_out, g_ffn, w_group, b_group, w_expert, b_expert, w_e1, w_e3, w_e2)
    x1_p, h2_p, eid_p, wts_p, kv_p, win_p, C_p, n_p, m_p, conv_p = hybrid_layer(
        x_prompt, jnp.zeros((B, M_CONV - 1, 2 * M_QK), dt), jnp.zeros((B, M_HEADS, M_DQK, M_DV), F32),
        jnp.zeros((B, M_HEADS, M_DQK), F32), jnp.zeros((B, M_HEADS), F32),
        None, jnp.zeros((B, 0, 2, N_KV, HEAD_DIM), dt), *weights, use_pallas_proj=True)
    with jax.default_matmul_precision("highest"):
        x1_s, h2_s, eid_s, wts_s, kv_s, win_s, C_s, n_s, m_s, conv_s = hybrid_layer(
            x_sample, state_conv, state_C, state_n, state_m, (cache_kv, page_table), state_win, *weights,
            use_pallas_proj=False)
    n_p_tok = h2_p.shape[0]
    moe = hier_moe_apply(jnp.concatenate([h2_p, h2_s], axis=0), jnp.concatenate([eid_p, eid_s], axis=0),
                         jnp.concatenate([wts_p, wts_s], axis=0), w_e1, w_e3, w_e2)
    y_p = x1_p + moe[:n_p_tok].reshape(x1_p.shape)
    y_s = x1_s + moe[n_p_tok:].reshape(x1_s.shape)
    return (y_p, y_s, kv_p, kv_s, win_p, win_s, C_p, C_s, n_p, n_s, m_p, m_s, conv_p, conv_s)
```

```python
import functools
import math

import jax
import jax.numpy as jnp
import numpy as np
from jax import lax
from jax.experimental import pallas as pl
from jax.experimental.pallas import tpu as pltpu

D_MODEL = 2048
M_HEADS = 4
M_DQK = 128
M_DV = 256
M_CONV = 4
M_CHUNK = 64
N_HEADS = 16
N_KV = 4
HEAD_DIM = 64
CMP_BLK = 32
SEL_BLK = 64
N_SELECT = 16
WINDOW = 512
NSA_QBLK = 64
FORCE_SCORE = 1e4
N_GROUPS = 4
EXPERTS_PER_GROUP = 8
N_EXPERTS = N_GROUPS * EXPERTS_PER_GROUP
TOP_K_FINE = 2
MOE_BLK = 128
EPS = 1e-6
M_QK = M_HEADS * M_DQK
M_VW = M_HEADS * M_DV
N_QW = N_HEADS * HEAD_DIM
N_KVW = 6 * N_KV * HEAD_DIM
SPLITS = (M_QK, M_QK, M_VW, M_HEADS, M_HEADS, M_VW, N_QW, N_KVW, 3 * N_HEADS, D_MODEL, D_MODEL)
D_IN = sum(SPLITS)

F32 = jnp.float32
BF16 = jnp.bfloat16
VMEM_LIMIT = 48 * 1024 * 1024


def _mxu(a, b):
    return jnp.dot(a, b, preferred_element_type=F32, precision=lax.Precision.DEFAULT)


def _norm_proj_kernel(x_ref, g_ref, w_ref, o_ref, hx_ref):
    @pl.when(pl.program_id(1) == 0)
    def _():
        x = x_ref[...]
        r = lax.rsqrt(jnp.mean(x * x, axis=-1, keepdims=True) + EPS)
        hx_ref[...] = (x * r * g_ref[...]).astype(BF16)

    o_ref[...] = _mxu(hx_ref[...], w_ref[...])


def norm_proj(x, g, w_bf16, *, tm=1024, tn=896):
    n, d = x.shape
    k = w_bf16.shape[1]
    return pl.pallas_call(
        _norm_proj_kernel,
        grid=(n // tm, k // tn),
        in_specs=[
            pl.BlockSpec((tm, d), lambda i, j: (i, 0)),
            pl.BlockSpec((1, d), lambda i, j: (0, 0)),
            pl.BlockSpec((d, tn), lambda i, j: (0, j)),
        ],
        out_specs=pl.BlockSpec((tm, tn), lambda i, j: (i, j)),
        out_shape=jax.ShapeDtypeStruct((n, k), F32),
        scratch_shapes=[pltpu.VMEM((tm, d), BF16)],
        compiler_params=pltpu.CompilerParams(
            dimension_semantics=("parallel", "arbitrary"), vmem_limit_bytes=VMEM_LIMIT),
        name="norm_proj",
    )(x, g.reshape(1, d), w_bf16)


def _rms_rows_kernel(x_ref, g_ref, o_ref):
    x = x_ref[...]
    r = lax.rsqrt(jnp.mean(x * x, axis=-1, keepdims=True) + EPS)
    o_ref[...] = (x * r * g_ref[...]).astype(o_ref.dtype)


def rms_rows(x, g, out_dtype, *, tr=2048):
    rows, c = x.shape
    tr = min(tr, rows)
    return pl.pallas_call(
        _rms_rows_kernel,
        grid=(rows // tr,),
        in_specs=[pl.BlockSpec((tr, c), lambda i: (i, 0)), pl.BlockSpec((1, c), lambda i: (0, 0))],
        out_specs=pl.BlockSpec((tr, c), lambda i: (i, 0)),
        out_shape=jax.ShapeDtypeStruct((rows, c), out_dtype),
        compiler_params=pltpu.CompilerParams(dimension_semantics=("parallel",)),
        name="rms_rows",
    )(x, g.reshape(1, c))


def _split_bf16(x):
    hi = lax.bitcast_convert_type(
        lax.bitcast_convert_type(x, jnp.int32) & jnp.int32(-65536), F32)
    return hi.astype(BF16), (x - hi).astype(BF16)


def _dot_split(a, b_hi, b_lo):
    a_hi, a_lo = _split_bf16(a)
    return _mxu(a_hi, b_hi) + (_mxu(a_lo, b_hi) + _mxu(a_hi, b_lo))


def _compress_kernel(x_ref, pe_ref, w1_ref, w1l_ref, w2_ref, w2l_ref, g_ref, o_ref, *, normalize, precise):
    x = x_ref[...] + pe_ref[...]
    if precise:
        h = _dot_split(x, w1_ref[...], w1l_ref[...])
    else:
        h = _mxu(x.astype(BF16), w1_ref[...])
    h = h * jax.nn.sigmoid(h)
    if precise:
        y = _dot_split(h, w2_ref[...], w2l_ref[...])
    else:
        y = _mxu(h.astype(BF16), w2_ref[...])
    if normalize:
        y = y * lax.rsqrt(jnp.mean(y * y, axis=-1, keepdims=True) + EPS) * g_ref[...]
    o_ref[...] = y.astype(o_ref.dtype)


def compress_rows(x, pe, w1, w2, g, *, normalize, precise=False, tr=512):
    rows, k = x.shape
    f = w1.shape[-1]
    d = w2.shape[1]
    tr = min(tr, rows)
    w1_hi, w1_lo = _split_bf16(w1.reshape(k, f))
    w2_hi, w2_lo = _split_bf16(w2)
    if not precise:
        w1_hi, w2_hi = w1.reshape(k, f).astype(BF16), w2.astype(BF16)
    return pl.pallas_call(
        functools.partial(_compress_kernel, normalize=normalize, precise=precise),
        grid=(rows // tr,),
        in_specs=[
            pl.BlockSpec((tr, k), lambda i: (i, 0)),
            pl.BlockSpec((1, k), lambda i: (0, 0)),
            pl.BlockSpec((k, f), lambda i: (0, 0)),
            pl.BlockSpec((k, f), lambda i: (0, 0)),
            pl.BlockSpec((f, d), lambda i: (0, 0)),
            pl.BlockSpec((f, d), lambda i: (0, 0)),
            pl.BlockSpec((1, d), lambda i: (0, 0)),
        ],
        out_specs=pl.BlockSpec((tr, d), lambda i: (i, 0)),
        out_shape=jax.ShapeDtypeStruct((rows, d), F32 if precise else BF16),
        compiler_params=pltpu.CompilerParams(dimension_semantics=("parallel",)),
        name="nsa_compress",
    )(x, pe.reshape(1, k), w1_hi, w1_lo, w2_hi, w2_lo, g.reshape(1, d))


CMP_TILE_ROWS = 128


def _compress_paged_kernel(x_ref, pe_ref, w_hi_ref, w_lo_ref, w2_hi_ref, w2_lo_ref, g_ref, o_ref, *,
                           normalize, cpp, hidden):
    rows = x_ref.shape[0]
    hd = x_ref.shape[1]
    acc = jnp.zeros((rows, cpp * hidden), F32)
    for dp in range(hd // 2):
        a = x_ref[:, 2 * dp, :] + pe_ref[pl.ds(2 * dp, 1), :]
        b = x_ref[:, 2 * dp + 1, :] + pe_ref[pl.ds(2 * dp + 1, 1), :]
        acc = acc + _dot_split(jnp.concatenate([a, b], axis=1), w_hi_ref[dp], w_lo_ref[dp])
    hid = acc * jax.nn.sigmoid(acc)
    for c in range(cpp):
        y = _dot_split(hid[:, c * hidden:(c + 1) * hidden], w2_hi_ref[...], w2_lo_ref[...])
        if normalize:
            y = y * lax.rsqrt(jnp.mean(y * y, axis=-1, keepdims=True) + EPS) * g_ref[...]
        o_ref[:, c * hd:(c + 1) * hd] = y


def compress_paged(x_tiles, pe, w1, w2, g, *, normalize):
    r, hd, page = x_tiles.shape
    cpp = page // CMP_BLK
    hidden = w1.shape[-1]
    eye = jnp.eye(cpp, dtype=F32)
    w_big = jnp.einsum('cC,ldf->dclCf', eye, w1).reshape(hd // 2, 2 * page, cpp * hidden)
    w_hi, w_lo = _split_bf16(w_big)
    w2_hi, w2_lo = _split_bf16(w2)
    pe_t = jnp.tile(pe.T, (1, cpp))
    const3 = lambda i: (0, 0, 0)
    const2 = lambda i: (0, 0)
    return pl.pallas_call(
        functools.partial(_compress_paged_kernel, normalize=normalize, cpp=cpp, hidden=hidden),
        grid=(r // CMP_TILE_ROWS,),
        in_specs=[
            pl.BlockSpec((CMP_TILE_ROWS, hd, page), lambda i: (i, 0, 0)),
            pl.BlockSpec((hd, page), const2),
            pl.BlockSpec((hd // 2, 2 * page, cpp * hidden), const3, pipeline_mode=pl.Buffered(1)),
            pl.BlockSpec((hd // 2, 2 * page, cpp * hidden), const3, pipeline_mode=pl.Buffered(1)),
            pl.BlockSpec((hidden, hd), const2),
            pl.BlockSpec((hidden, hd), const2),
            pl.BlockSpec((1, hd), const2),
        ],
        out_specs=pl.BlockSpec((CMP_TILE_ROWS, cpp * hd), lambda i: (i, 0)),
        out_shape=jax.ShapeDtypeStruct((r, cpp * hd), F32),
        compiler_params=pltpu.CompilerParams(dimension_semantics=("parallel",), vmem_limit_bytes=VMEM_LIMIT),
        name="nsa_compress_paged",
    )(x_tiles, pe_t, w_hi, w_lo, w2_hi, w2_lo, g.reshape(1, hd))


NSA_TQ = 128
NSA_TK = 512
HPG = N_HEADS // N_KV
NSA_Q = HPG * NSA_TQ
NEG_BIG = -1e30


def _nsa_prompt_kernel(qT_ref, gq_ref, gate_ref, kc_ref, vcT_ref, ks_ref, vsT_ref, kw_ref, vwT_ref,
                       o_ref, sc_ref, sel_ref, imp_ref, *, n_cmp, n_sel):
    j = pl.program_id(2)
    s0 = j * NSA_TQ
    q = qT_ref[0, 0, 0]
    r = lax.rsqrt(jnp.mean(q * q, axis=0, keepdims=True) + EPS)
    qn = (q * r * gq_ref[...] * (HEAD_DIM ** -0.5 * math.log2(math.e))).astype(BF16)
    lane = lax.broadcasted_iota(jnp.int32, (1, NSA_Q), 1)
    t_row = s0 + (lane & (NSA_TQ - 1))

    s = _mxu(kc_ref[0, 0], qn)
    c_end = lax.broadcasted_iota(jnp.int32, (n_cmp, NSA_Q), 0) * CMP_BLK + (CMP_BLK - 1)
    mask_c = c_end <= t_row
    s = jnp.where(mask_c, s, -jnp.inf)
    m = jnp.max(s, axis=0, keepdims=True)
    m = jnp.where(m > -jnp.inf, m, 0.0)
    e = jnp.where(mask_c, jnp.exp2(s - m), 0.0)
    p_c = e / jnp.maximum(jnp.sum(e, axis=0, keepdims=True), 1e-30)
    o_c = _mxu(vcT_ref[0, 0], p_c.astype(BF16))

    imp_ref[...] = (p_c[:, 0:NSA_TQ] + p_c[:, NSA_TQ:2 * NSA_TQ]
                    + p_c[:, 2 * NSA_TQ:3 * NSA_TQ] + p_c[:, 3 * NSA_TQ:4 * NSA_TQ])
    imp = imp_ref[pl.ds(0, n_sel, stride=2), :] + imp_ref[pl.ds(1, n_sel, stride=2), :]
    n_iota = lax.broadcasted_iota(jnp.int32, (n_sel, NSA_TQ), 0)
    cur = (s0 + lax.broadcasted_iota(jnp.int32, (n_sel, NSA_TQ), 1)) // SEL_BLK
    valid = n_iota <= cur
    forced = (n_iota == 0) | (n_iota == cur) | (n_iota == cur - 1)
    score = jnp.where(valid, jnp.where(forced, FORCE_SCORE, imp), -jnp.inf)
    sc_ref[...] = score

    def rank_body(n2, rank):
        row = jnp.broadcast_to(sc_ref[pl.ds(n2, 1), :], (n_sel, NSA_TQ))
        beats = (row > score) | ((row == score) & (n2 < n_iota))
        return rank + jnp.where(beats, 1.0, 0.0)

    n_live = jnp.minimum((s0 + NSA_TQ - 1) // SEL_BLK + 1, n_sel)
    rank = lax.fori_loop(0, n_live, rank_body, jnp.zeros((n_sel, NSA_TQ), F32))
    sel_ref[...] = jnp.where(valid & (rank < N_SELECT), 0.0, NEG_BIG)

    key_iota = lax.broadcasted_iota(jnp.int32, (NSA_TK, NSA_TQ), 0)
    t_q = s0 + lax.broadcasted_iota(jnp.int32, (1, NSA_TQ), 1)

    def attend(k_ref, vT_ref, c_lo, c_hi, bias_fn):
        def body(c, carry):
            m_i, l_i, acc = carry
            k = k_ref[0, 0, pl.ds(pl.multiple_of(c * NSA_TK, NSA_TK), NSA_TK), :]
            sk = _mxu(k, qn)
            bias = bias_fn(c)
            ms, ls, ps = [], [], []
            for h in range(HPG):
                sl = slice(h * NSA_TQ, (h + 1) * NSA_TQ)
                s_h = sk[:, sl] + bias
                m_h = jnp.maximum(m_i[:, sl], jnp.max(s_h, axis=0, keepdims=True))
                p_h = jnp.exp2(s_h - m_h)
                ms.append(m_h)
                ls.append(jnp.sum(p_h, axis=0, keepdims=True))
                ps.append(p_h.astype(BF16))
            m_new = jnp.concatenate(ms, axis=1)
            alpha = jnp.exp2(m_i - m_new)
            l_new = alpha * l_i + jnp.concatenate(ls, axis=1)
            acc = alpha * acc + _mxu(vT_ref[0, 0, c], jnp.concatenate(ps, axis=1))
            return m_new, l_new, acc

        init = (jnp.full((1, NSA_Q), NEG_BIG, F32), jnp.zeros((1, NSA_Q), F32),
                jnp.zeros((HEAD_DIM, NSA_Q), F32))
        _, l_f, acc_f = lax.fori_loop(c_lo, c_hi, body, init)
        return acc_f / jnp.maximum(l_f, 1e-30)

    def sel_bias(c):
        pos = c * NSA_TK + key_iota
        per_blk = [jnp.broadcast_to(
            sel_ref[pl.ds(jnp.minimum(c * (NSA_TK // SEL_BLK) + i, n_sel - 1), 1), :], (SEL_BLK, NSA_TQ))
            for i in range(NSA_TK // SEL_BLK)]
        return jnp.where(pos <= t_q, jnp.concatenate(per_blk, axis=0), NEG_BIG)

    def win_bias(c):
        pos = c * NSA_TK + key_iota
        return jnp.where((pos <= t_q) & (pos > t_q - WINDOW), 0.0, NEG_BIG)

    c_hi = (s0 + NSA_TQ + NSA_TK - 1) // NSA_TK
    o_s = attend(ks_ref, vsT_ref, 0, c_hi, sel_bias)
    o_w = attend(kw_ref, vwT_ref, jnp.maximum(s0 - (WINDOW - 1), 0) // NSA_TK, c_hi, win_bias)

    g = jax.nn.sigmoid(gate_ref[0, 0, 0])
    o_ref[0, 0, 0] = g[0:1] * o_c + g[1:2] * o_s + g[2:3] * o_w


def nsa_prompt(nq, ng, kv_new, g_q, g_kc, g_ks, g_kw, pe_ck, w_ck1, w_ck2, pe_cv, w_cv1, w_cv2):
    B, T = nq.shape[:2]
    G = N_KV
    nqb = T // NSA_TQ
    n_cmp = T // CMP_BLK
    n_sel = T // SEL_BLK
    nch = T // NSA_TK
    qT = nq.reshape(B, nqb, NSA_TQ, G, HPG, HEAD_DIM).transpose(0, 3, 1, 5, 4, 2).reshape(B, G, nqb, HEAD_DIM, NSA_Q)
    gT = ng.reshape(B, nqb, NSA_TQ, G, HPG, 3).transpose(0, 3, 1, 5, 4, 2).reshape(B, G, nqb, 3, NSA_Q)
    kv_g = kv_new.transpose(2, 0, 3, 1, 4)
    xk = kv_g[0].reshape(B * G * n_cmp, CMP_BLK * HEAD_DIM)
    xv = kv_g[1].reshape(B * G * n_cmp, CMP_BLK * HEAD_DIM)
    kc = compress_rows(xk, pe_ck, w_ck1, w_ck2, g_kc, normalize=True).reshape(B, G, n_cmp, HEAD_DIM)
    vc = compress_rows(xv, pe_cv, w_cv1, w_cv2, g_kc, normalize=False).reshape(B, G, n_cmp, HEAD_DIM)
    vcT = vc.transpose(0, 1, 3, 2)
    ks = rms_rows(kv_g[2].reshape(B * G * T, HEAD_DIM), g_ks, BF16).reshape(B, G, T, HEAD_DIM)
    kw = rms_rows(kv_g[4].reshape(B * G * T, HEAD_DIM), g_kw, BF16).reshape(B, G, T, HEAD_DIM)
    vsT = kv_g[3].astype(BF16).reshape(B, G, nch, NSA_TK, HEAD_DIM).transpose(0, 1, 2, 4, 3)
    vwT = kv_g[5].astype(BF16).reshape(B, G, nch, NSA_TK, HEAD_DIM).transpose(0, 1, 2, 4, 3)

    bg = lambda b, g, j: (b, g, 0, 0)
    bg5 = lambda b, g, j: (b, g, 0, 0, 0)
    bgj = lambda b, g, j: (b, g, j, 0, 0)
    oT = pl.pallas_call(
        functools.partial(_nsa_prompt_kernel, n_cmp=n_cmp, n_sel=n_sel),
        grid=(B, G, nqb),
        in_specs=[
            pl.BlockSpec((1, 1, 1, HEAD_DIM, NSA_Q), bgj),
            pl.BlockSpec((HEAD_DIM, 1), lambda b, g, j: (0, 0)),
            pl.BlockSpec((1, 1, 1, 3, NSA_Q), bgj),
            pl.BlockSpec((1, 1, n_cmp, HEAD_DIM), bg),
            pl.BlockSpec((1, 1, HEAD_DIM, n_cmp), bg),
            pl.BlockSpec((1, 1, T, HEAD_DIM), bg),
            pl.BlockSpec((1, 1, nch, HEAD_DIM, NSA_TK), bg5),
            pl.BlockSpec((1, 1, T, HEAD_DIM), bg),
            pl.BlockSpec((1, 1, nch, HEAD_DIM, NSA_TK), bg5),
        ],
        out_specs=pl.BlockSpec((1, 1, 1, HEAD_DIM, NSA_Q), bgj),
        out_shape=jax.ShapeDtypeStruct((B, G, nqb, HEAD_DIM, NSA_Q), F32),
        scratch_shapes=[pltpu.VMEM((n_sel, NSA_TQ), F32), pltpu.VMEM((n_sel, NSA_TQ), F32),
                        pltpu.VMEM((n_cmp, NSA_TQ), F32)],
        compiler_params=pltpu.CompilerParams(
            dimension_semantics=("parallel", "parallel", "arbitrary"), vmem_limit_bytes=VMEM_LIMIT),
        name="nsa_prompt",
    )(qT, g_q.reshape(HEAD_DIM, 1), gT, kc, vcT, ks, vsT, kw, vwT)
    return oT.reshape(B, G, nqb, HEAD_DIM, HPG, NSA_TQ).transpose(0, 2, 5, 1, 4, 3).reshape(B, T, N_QW)


def rms_norm(x, g):
    xf = x.astype(F32)
    y = xf * lax.rsqrt(jnp.mean(xf * xf, axis=-1, keepdims=True) + EPS)
    return (y * g.astype(F32)).astype(x.dtype)


def masked_softmax(s, mask):
    s = jnp.where(mask, s.astype(F32), -jnp.inf)
    m = jnp.max(s, axis=-1, keepdims=True)
    m = jnp.where(jnp.isfinite(m), m, 0.0)
    e = jnp.where(mask, jnp.exp(s - m), 0.0)
    return e / jnp.maximum(jnp.sum(e, axis=-1, keepdims=True), 1e-30)


def short_conv(u, prev, w):
    T = u.shape[1]
    full = jnp.concatenate([prev.astype(u.dtype), u], axis=1)
    out = full[:, 0:T] * w[0]
    for j in range(1, M_CONV):
        out = out + full[:, j:j + T] * w[j]
    return out, full[:, T:]


def mlstm_chunkwise(q, k, v, i_pre, logf, C0, n0, m0):
    B, T = q.shape[:2]
    L = M_CHUNK if T % M_CHUNK == 0 else T
    nc = T // L

    def chunks(a):
        a = a.reshape((B, nc, L) + a.shape[2:])
        return jnp.moveaxis(jnp.moveaxis(a, 3, 2), 1, 0)

    causal = jnp.tril(jnp.ones((L, L), bool))

    def step(carry, xs):
        C, n, m = carry
        qc, kc, vc, ic, fc = xs
        b = jnp.cumsum(fc, axis=-1)
        logD = jnp.where(causal, b[..., :, None] - b[..., None, :] + ic[..., None, :], -jnp.inf)
        inter = b + m[..., None]
        m_t = jnp.maximum(inter, jnp.max(logD, axis=-1))
        a = jnp.exp(inter - m_t)
        S = jnp.einsum('bhtd,bhsd->bhts', qc, kc) * jnp.exp(logD - m_t[..., None])
        num = a[..., None] * jnp.einsum('bhtd,bhde->bhte', qc, C) + jnp.einsum('bhts,bhse->bhte', S, vc)
        den = a * jnp.einsum('bhtd,bhd->bht', qc, n) + jnp.sum(S, axis=-1)
        h = num / jnp.maximum(jnp.abs(den), jnp.exp(-m_t))[..., None]
        m_new = m_t[..., -1]
        w = jnp.exp(b[..., -1:] - b + ic - m_new[..., None])
        aL = jnp.exp(b[..., -1] + m - m_new)
        C_new = aL[..., None, None] * C + jnp.einsum('bhs,bhsd,bhse->bhde', w, kc, vc)
        n_new = aL[..., None] * n + jnp.einsum('bhs,bhsd->bhd', w, kc)
        return (C_new, n_new, m_new), h

    (C, n, m), h = lax.scan(step, (C0, n0, m0), (chunks(q), chunks(k), chunks(v), chunks(i_pre), chunks(logf)))
    h = jnp.moveaxis(jnp.moveaxis(h, 0, 1), 2, 3).reshape(B, T, M_HEADS, M_DV)
    return h, C, n, m


def compress(rows, pe, w1, w2):
    B, Lk = rows.shape[:2]
    nc = Lk // CMP_BLK
    blk = rows[:, :nc * CMP_BLK].reshape(B, nc, CMP_BLK, N_KV, HEAD_DIM) + pe[:, None, :]
    hid = jax.nn.silu(jnp.einsum('bclgd,ldf->bcgf', blk, w1))
    return jnp.einsum('bcgf,fd->bcgd', hid, w2)


def nsa_attention(q, gates, kv_all, kw_full, q0, g_kc, g_ks, g_kw, pe_ck, w_ck1, w_ck2, pe_cv, w_cv1, w_cv2):
    B, T = q.shape[:2]
    Lk = kv_all.shape[1]
    hpg = N_HEADS // N_KV
    scale = HEAD_DIM ** -0.5
    kc = rms_norm(compress(kv_all[:, :, 0], pe_ck, w_ck1, w_ck2), g_kc)
    vc = compress(kv_all[:, :, 1], pe_cv, w_cv1, w_cv2)
    nc = kc.shape[1]
    n_sel = -(-Lk // SEL_BLK)
    pad = n_sel * SEL_BLK - Lk
    ks = jnp.pad(rms_norm(kv_all[:, :, 2], g_ks), ((0, 0), (0, pad), (0, 0), (0, 0)))
    ks = ks.reshape(B, n_sel, SEL_BLK, N_KV, HEAD_DIM).transpose(0, 3, 1, 2, 4)
    vs = jnp.pad(kv_all[:, :, 3], ((0, 0), (0, pad), (0, 0), (0, 0)))
    vs = vs.reshape(B, n_sel, SEL_BLK, N_KV, HEAD_DIM).transpose(0, 3, 1, 2, 4)
    kw = rms_norm(kw_full[:, :, 0], g_kw)
    vw = kw_full[:, :, 1]
    top = min(N_SELECT, n_sel)
    ratio = SEL_BLK // CMP_BLK
    qb = NSA_QBLK if T % NSA_QBLK == 0 else T
    nqb = T // qb
    qg = q.reshape(B, nqb, qb, N_KV, hpg, HEAD_DIM).transpose(1, 0, 3, 4, 2, 5)
    gg = jnp.moveaxis(gates.reshape(B, nqb, qb, N_HEADS, 3), 1, 0)
    cmp_end = (jnp.arange(nc) + 1) * CMP_BLK - 1
    blk_ids = jnp.arange(n_sel)
    b_ix = jnp.arange(B)[:, None, None, None]
    g_ix = jnp.arange(N_KV)[None, :, None, None]

    def block(args):
        qi, gi, j = args
        s0 = j * qb
        t = q0 + s0 + jnp.arange(qb)
        sc = jnp.einsum('bghqd,bcgd->bghqc', qi, kc) * scale
        p_c = masked_softmax(sc, cmp_end[None, :] <= t[:, None])
        o_c = jnp.einsum('bghqc,bcgd->bghqd', p_c.astype(vc.dtype), vc)
        imp = jnp.sum(p_c, axis=2)
        imp = jnp.pad(imp, ((0, 0), (0, 0), (0, 0), (0, n_sel * ratio - nc)))
        imp = imp.reshape(B, N_KV, qb, n_sel, ratio).sum(-1)
        cur = t // SEL_BLK
        valid = blk_ids[None, :] <= cur[:, None]
        forced = (blk_ids[None, :] == 0) | (blk_ids[None, :] == cur[:, None]) | (blk_ids[None, :] == cur[:, None] - 1)
        imp = jnp.where(valid, jnp.where(forced, FORCE_SCORE, imp), -jnp.inf)
        top_v, top_i = lax.top_k(imp, top)
        k_sel = ks[b_ix, g_ix, top_i]
        v_sel = vs[b_ix, g_ix, top_i]
        pos = top_i[..., None] * SEL_BLK + jnp.arange(SEL_BLK)
        m_s = (pos <= t[:, None, None]) & jnp.isfinite(top_v)[..., None]
        ss = (jnp.einsum('bghqd,bgqnkd->bghqnk', qi, k_sel) * scale).reshape(B, N_KV, hpg, qb, top * SEL_BLK)
        p_s = masked_softmax(ss, m_s.reshape(B, N_KV, 1, qb, top * SEL_BLK))
        o_s = jnp.einsum('bghqnk,bgqnkd->bghqd', p_s.reshape(B, N_KV, hpg, qb, top, SEL_BLK).astype(v_sel.dtype), v_sel)
        kwi = lax.dynamic_slice_in_dim(kw, s0, qb + WINDOW, axis=1)
        vwi = lax.dynamic_slice_in_dim(vw, s0, qb + WINDOW, axis=1)
        pw = q0 - WINDOW + s0 + jnp.arange(qb + WINDOW)
        m_w = (pw[None, :] <= t[:, None]) & (pw[None, :] > t[:, None] - WINDOW) & (pw[None, :] >= 0)
        sw = jnp.einsum('bghqd,bkgd->bghqk', qi, kwi) * scale
        p_w = masked_softmax(sw, m_w)
        o_w = jnp.einsum('bghqk,bkgd->bghqd', p_w.astype(vwi.dtype), vwi)
        o = jnp.stack([o_c, o_s, o_w], axis=-1)
        gi_r = gi.reshape(B, qb, N_KV, hpg, 3).transpose(0, 2, 3, 1, 4)
        return jnp.einsum('bghqdc,bghqc->bqghd', o, gi_r.astype(o.dtype))

    out = lax.map(block, (qg, gg, jnp.arange(nqb)))
    return jnp.moveaxis(out, 0, 1).reshape(B, T, N_QW)


def nsa_sample(q, gates, cache_kv, page_table, kv_rows, win_all, g_kc, g_ks, g_kw,
               pe_ck, w_ck1, w_ck2, pe_cv, w_cv1, w_cv2):
    B, T = q.shape[:2]
    n_pages = page_table.shape[1]
    page = cache_kv.shape[1]
    P = n_pages * page
    G, hd = N_KV, HEAD_DIM
    assert T < CMP_BLK and T <= SEL_BLK and P % SEL_BLK == 0 and page % SEL_BLK == 0
    assert win_all.shape[1] == WINDOW + T
    nc = P // CMP_BLK
    cpp = page // CMP_BLK
    n_last = P // SEL_BLK
    n_sel = n_last + 1
    top = min(N_SELECT, n_sel)
    ratio = SEL_BLK // CMP_BLK
    scale = hd ** -0.5
    t = P + jnp.arange(T)
    qg = q.reshape(B, T, G, HPG, hd)

    n_kind = cache_kv.shape[2]
    cache_r = cache_kv.transpose(0, 2, 3, 4, 1).reshape(cache_kv.shape[0] * n_kind * G, hd, page)
    tile_row = lambda p, kind, g: (p * n_kind + kind) * G + g

    def cmp_tiles(kind):
        rows = tile_row(page_table[:, :, None], kind, jnp.arange(G)[None, None, :])
        return cache_r[rows.reshape(-1)]

    def cmp_rows(y):
        return y.reshape(B, n_pages, G, cpp, hd).transpose(0, 2, 1, 3, 4).reshape(B, G, nc, hd)

    kc = cmp_rows(compress_paged(cmp_tiles(0), pe_ck, w_ck1, w_ck2, g_kc, normalize=True))
    vc = cmp_rows(compress_paged(cmp_tiles(1), pe_cv, w_cv1, w_cv2, g_kc, normalize=False))
    cmp_end = (jnp.arange(nc) + 1) * CMP_BLK - 1
    sc = jnp.einsum('btghd,bgcd->bghtc', qg, kc) * scale
    p_c = masked_softmax(sc, cmp_end[None, :] <= t[:, None])
    o_c = jnp.einsum('bghtc,bgcd->bghtd', p_c, vc)

    imp = jnp.sum(p_c, axis=2)
    imp = jnp.pad(imp, ((0, 0), (0, 0), (0, 0), (0, n_sel * ratio - nc)))
    imp = imp.reshape(B, G, T, n_sel, ratio).sum(-1)
    blk_ids = jnp.arange(n_sel)
    cur = t // SEL_BLK
    valid = blk_ids[None, :] <= cur[:, None]
    forced = (blk_ids[None, :] == 0) | (blk_ids[None, :] == cur[:, None]) | (blk_ids[None, :] == cur[:, None] - 1)
    imp = jnp.where(valid, jnp.where(forced, FORCE_SCORE, imp), -jnp.inf)
    ids = jnp.arange(n_sel)
    rest, top_v, top_i = imp, [], []
    for _ in range(top):
        i_max = jnp.argmax(rest, axis=-1)
        top_i.append(i_max)
        top_v.append(jnp.max(rest, axis=-1))
        rest = jnp.where(ids == i_max[..., None], -jnp.inf, rest)
    top_v, top_i = jnp.stack(top_v, axis=-1), jnp.stack(top_i, axis=-1)

    bpp = page // SEL_BLK
    blk_c = jnp.minimum(top_i, n_last - 1)
    b_ix = jnp.arange(B)[:, None, None, None]
    phys = page_table[b_ix, blk_c // bpp]
    g_ix = jnp.arange(G)[None, :, None, None]
    tiles = jnp.stack([cache_r[tile_row(phys, 2, g_ix)], cache_r[tile_row(phys, 3, g_ix)]], axis=4)
    new_tile = jnp.pad(kv_rows[:, :, 2:4], ((0, 0), (0, page - T), (0, 0), (0, 0), (0, 0)))
    new_tile = new_tile.transpose(0, 3, 2, 4, 1)[:, :, None, None]
    is_new = top_i == n_last
    tiles = jnp.where(is_new[..., None, None, None], new_tile, tiles)
    k_t = tiles[..., 0, :, :]
    v_t = tiles[..., 1, :, :]
    k_t = k_t * lax.rsqrt(jnp.mean(k_t * k_t, axis=-2, keepdims=True) + EPS) * g_ks[:, None]
    lane = jnp.arange(page)
    tile0 = jnp.where(is_new, n_last, blk_c // bpp * bpp)
    pos = tile0[..., None] * SEL_BLK + lane
    m_s = ((pos // SEL_BLK == top_i[..., None]) & (pos <= t[:, None, None])
           & jnp.isfinite(top_v)[..., None])
    ss = (jnp.einsum('btghd,bgtndp->bghtnp', qg, k_t) * scale).reshape(B, G, HPG, T, top * page)
    p_s = masked_softmax(ss, m_s.reshape(B, G, 1, T, top * page))
    o_s = jnp.einsum('bghtnp,bgtndp->bghtd', p_s.reshape(B, G, HPG, T, top, page), v_t)

    kw = rms_norm(win_all[:, :, 0], g_kw)
    vw = win_all[:, :, 1]
    pw = P - WINDOW + jnp.arange(T + WINDOW)
    m_w = (pw[None, :] <= t[:, None]) & (pw[None, :] > t[:, None] - WINDOW) & (pw[None, :] >= 0)
    sw = jnp.einsum('btghd,bkgd->bghtk', qg, kw) * scale
    p_w = masked_softmax(sw, m_w)
    o_w = jnp.einsum('bghtk,bkgd->bghtd', p_w, vw)

    o = jnp.stack([o_c, o_s, o_w], axis=-1)
    return jnp.einsum('bghtdc,btghc->btghd', o, gates.reshape(B, T, G, HPG, 3)).reshape(B, T, N_QW)


def moe_route(group_logits, expert_logits):
    assert TOP_K_FINE == 2
    N = group_logits.shape[0]
    pg = jax.nn.softmax(group_logits.astype(F32), axis=-1)
    grp = jnp.argmax(pg, axis=-1)
    p_grp = jnp.max(pg, axis=-1)
    le = expert_logits.astype(F32).reshape(N, N_GROUPS, EXPERTS_PER_GROUP)
    pe = jax.nn.softmax(le[jnp.arange(N), grp], axis=-1)
    i1 = jnp.argmax(pe, axis=-1)
    hit1 = jnp.arange(EXPERTS_PER_GROUP)[None, :] == i1[:, None]
    rest = jnp.where(hit1, -jnp.inf, pe)
    i2 = jnp.argmax(rest, axis=-1)
    top_i = jnp.stack([i1, i2], axis=-1)
    top_p = jnp.stack([jnp.max(pe, axis=-1), jnp.max(rest, axis=-1)], axis=-1)
    wts = p_grp[:, None] * top_p / jnp.sum(top_p, axis=-1, keepdims=True)
    eid = grp[:, None] * EXPERTS_PER_GROUP + top_i
    return eid.astype(jnp.int32), wts


MOE_TM = 256


def _moe_ffn_kernel(be_ref, x_ref, w1_ref, w3_ref, w2_ref, wt_ref, o_ref, w1b_ref, w3b_ref, w2b_ref):
    i = pl.program_id(0)

    @pl.when((i == 0) | (be_ref[i] != be_ref[jnp.maximum(i - 1, 0)]))
    def _():
        w1b_ref[...] = w1_ref[0].astype(BF16)
        w3b_ref[...] = w3_ref[0].astype(BF16)
        w2b_ref[...] = w2_ref[0].astype(BF16)

    x = x_ref[...]
    a = _mxu(x, w1b_ref[...])
    b = _mxu(x, w3b_ref[...])
    h = (a * jax.nn.sigmoid(a) * b).astype(BF16)
    o_ref[...] = _mxu(h, w2b_ref[...]) * wt_ref[...]


def moe_ffn(rows, row_wt, blk_e, w1, w3, w2):
    m_pad, d = rows.shape
    n_blk = m_pad // MOE_TM
    f = w1.shape[2]
    return pl.pallas_call(
        _moe_ffn_kernel,
        grid_spec=pltpu.PrefetchScalarGridSpec(
            num_scalar_prefetch=1,
            grid=(n_blk,),
            in_specs=[
                pl.BlockSpec((MOE_TM, d), lambda i, be: (i, 0)),
                pl.BlockSpec((1, d, f), lambda i, be: (be[i], 0, 0)),
                pl.BlockSpec((1, d, f), lambda i, be: (be[i], 0, 0)),
                pl.BlockSpec((1, f, d), lambda i, be: (be[i], 0, 0)),
                pl.BlockSpec((MOE_TM, 1), lambda i, be: (i, 0)),
            ],
            out_specs=pl.BlockSpec((MOE_TM, d), lambda i, be: (i, 0)),
            scratch_shapes=[pltpu.VMEM((d, f), BF16), pltpu.VMEM((d, f), BF16), pltpu.VMEM((f, d), BF16)],
        ),
        out_shape=jax.ShapeDtypeStruct((m_pad, d), F32),
        compiler_params=pltpu.CompilerParams(
            dimension_semantics=("arbitrary",), vmem_limit_bytes=VMEM_LIMIT),
        name="moe_ffn",
    )(blk_e, rows, w1, w3, w2, row_wt)


def hier_moe_apply(x, eid, wts, w_e1, w_e3, w_e2):
    N, D = x.shape
    M = N * TOP_K_FINE
    eid_f = eid.reshape(-1)
    onehot = (eid_f[:, None] == jnp.arange(N_EXPERTS, dtype=jnp.int32)[None, :]).astype(jnp.int32)
    run = jnp.cumsum(onehot, axis=0)
    counts = run[-1]
    rank_in_e = jnp.sum(run * onehot, axis=1) - 1
    padded = (counts + MOE_TM - 1) // MOE_TM * MOE_TM
    p_end = jnp.cumsum(padded)
    p_start = p_end - padded
    pos = (p_start[eid_f] + rank_in_e).astype(jnp.int32)
    n_blk = (M + N_EXPERTS * (MOE_TM - 1) + MOE_TM - 1) // MOE_TM
    m_pad = n_blk * MOE_TM
    src_tok = jnp.zeros((m_pad,), jnp.int32).at[pos].set(jnp.arange(M, dtype=jnp.int32) // TOP_K_FINE)
    row_wt = jnp.zeros((m_pad,), F32).at[pos].set(wts.reshape(-1))
    blk_e = jnp.minimum(jnp.searchsorted(p_end, jnp.arange(n_blk, dtype=jnp.int32) * MOE_TM, side='right'),
                        N_EXPERTS - 1).astype(jnp.int32)
    rows = x[src_tok].astype(BF16)
    out = moe_ffn(rows, row_wt.reshape(m_pad, 1), blk_e, w_e1, w_e3, w_e2)
    pos = pos.reshape(N, TOP_K_FINE)
    return out[pos[:, 0]] + out[pos[:, 1]]


OUT_TM = 256
ROUTER_PAD = 128


def _out_proj_kernel(x_ref, gm_ref, gn_ref, hm_ref, hn_ref, wm_ref, wn_ref, wo_ref, gf_ref, wr_ref, br_ref,
                     x1_ref, h2_ref, lg_ref):
    pm = _mxu(hm_ref[...].astype(BF16), wm_ref[...])
    pn = _mxu(hn_ref[...].astype(BF16), wn_ref[...])
    u = jax.nn.sigmoid(gm_ref[...]) * pm + jax.nn.sigmoid(gn_ref[...]) * pn
    x1 = x_ref[...] + _mxu(u.astype(BF16), wo_ref[...])
    x1_ref[...] = x1
    h2 = x1 * lax.rsqrt(jnp.mean(x1 * x1, axis=-1, keepdims=True) + EPS) * gf_ref[...]
    h2_ref[...] = h2
    lg_ref[...] = _mxu(h2.astype(BF16), wr_ref[...]) + br_ref[...]


def out_proj(x, proj, gate_col_blocks, h_m, h_n, w_proj_m, w_proj_n, w_out, g_ffn, w_group, b_group, w_expert, b_expert):
    n, d = x.shape
    n_r = N_GROUPS + N_EXPERTS
    w_r = jnp.pad(jnp.concatenate([w_group, w_expert], axis=1), ((0, 0), (0, ROUTER_PAD - n_r))).astype(BF16)
    b_r = jnp.pad(jnp.concatenate([b_group, b_expert]), (0, ROUTER_PAD - n_r)).reshape(1, ROUTER_PAD)
    cm, cn = gate_col_blocks
    row = lambda i: (i, 0)
    const = lambda i: (0, 0)
    once = pl.Buffered(1)
    return pl.pallas_call(
        _out_proj_kernel,
        grid=(n // OUT_TM,),
        in_specs=[
            pl.BlockSpec((OUT_TM, d), row),
            pl.BlockSpec((OUT_TM, d), lambda i: (i, cm)),
            pl.BlockSpec((OUT_TM, d), lambda i: (i, cn)),
            pl.BlockSpec((OUT_TM, h_m.shape[1]), row),
            pl.BlockSpec((OUT_TM, h_n.shape[1]), row),
            pl.BlockSpec(w_proj_m.shape, const, pipeline_mode=once),
            pl.BlockSpec(w_proj_n.shape, const, pipeline_mode=once),
            pl.BlockSpec(w_out.shape, const, pipeline_mode=once),
            pl.BlockSpec((1, d), const),
            pl.BlockSpec((d, ROUTER_PAD), const),
            pl.BlockSpec((1, ROUTER_PAD), const),
        ],
        out_specs=[pl.BlockSpec((OUT_TM, d), row), pl.BlockSpec((OUT_TM, d), row),
                   pl.BlockSpec((OUT_TM, ROUTER_PAD), row)],
        out_shape=[jax.ShapeDtypeStruct((n, d), F32), jax.ShapeDtypeStruct((n, d), F32),
                   jax.ShapeDtypeStruct((n, ROUTER_PAD), F32)],
        compiler_params=pltpu.CompilerParams(dimension_semantics=("parallel",), vmem_limit_bytes=VMEM_LIMIT),
        name="out_proj",
    )(x, proj, proj, h_m, h_n, w_proj_m.astype(BF16), w_proj_n.astype(BF16), w_out.astype(BF16),
      g_ffn.reshape(1, d), w_r, b_r)


_SEG = dict(zip(("mq", "mk", "mv", "mi", "mf", "mo", "nq", "nkv", "ng", "gate_m", "gate_n"),
                zip(np.cumsum((0,) + SPLITS[:-1]).tolist(), SPLITS)))
_PROJ_ORDER = ("gate_m", "gate_n", "mv", "mo", "nq", "mq", "mk", "nkv", "mi", "mf", "ng")
_PROJ_OFF = dict(zip(_PROJ_ORDER, np.cumsum([0] + [_SEG[k][1] for k in _PROJ_ORDER[:-1]]).tolist()))
D_IN_PAD = -(-D_IN // 896) * 896


def hybrid_layer(x, conv_prev, C0, n0, m0, paged, win_past, g_mix, w_in, w_conv, b_i, b_f, g_mh, g_q, g_kc, g_ks, g_kw,
                 pe_ck, w_ck1, w_ck2, pe_cv, w_cv1, w_cv2, w_proj_m, w_proj_n, w_out, g_ffn, w_group, b_group,
                 w_expert, b_expert, w_e1, w_e3, w_e2, *, use_pallas_proj):
    B, T, D = x.shape
    WB = win_past.shape[1]
    split_at = np.cumsum(SPLITS)[:-1].tolist()
    if use_pallas_proj:
        w_pad = jnp.concatenate(
            [w_in[:, _SEG[k][0]:_SEG[k][0] + _SEG[k][1]].astype(BF16) for k in _PROJ_ORDER]
            + [jnp.zeros((D, D_IN_PAD - D_IN), BF16)], axis=1)
        proj2d = norm_proj(x.reshape(B * T, D), g_mix, w_pad)
        seg = lambda k: proj2d[:, _PROJ_OFF[k]:_PROJ_OFF[k] + _SEG[k][1]].reshape(B, T, _SEG[k][1])
        mq, mk, mv, mi, mf, mo, nq, nkv, ng = (seg(k) for k in ("mq", "mk", "mv", "mi", "mf", "mo", "nq", "nkv", "ng"))
    else:
        proj = rms_norm(x, g_mix) @ w_in
        mq, mk, mv, mi, mf, mo, nq, nkv, ng, gate_m, gate_n = jnp.split(proj, split_at, axis=-1)
    qk, conv_state = short_conv(jnp.concatenate([mq, mk], axis=-1), conv_prev, w_conv)
    qk = jax.nn.silu(qk)
    q_m = qk[..., :M_QK].reshape(B, T, M_HEADS, M_DQK).astype(F32)
    k_m = qk[..., M_QK:].reshape(B, T, M_HEADS, M_DQK).astype(F32) * (M_DQK ** -0.5)
    v_m = mv.reshape(B, T, M_HEADS, M_DV).astype(F32)
    i_pre = (mi + b_i).astype(F32)
    logf = jax.nn.log_sigmoid((mf + b_f).astype(F32))
    h_m, C1, n1, m1 = mlstm_chunkwise(q_m, k_m, v_m, i_pre, logf, C0.astype(F32), n0.astype(F32), m0.astype(F32))
    h_m = rms_norm(h_m, g_mh).reshape(B, T, M_VW).astype(x.dtype) * jax.nn.sigmoid(mo)
    qn = rms_norm(nq.reshape(B, T, N_HEADS, HEAD_DIM), g_q)
    kv_new = nkv.reshape(B, T, 6, N_KV, HEAD_DIM)
    kv_rows = kv_new[:, :, :4]
    win_all = jnp.concatenate([win_past.astype(x.dtype), kv_new[:, :, 4:]], axis=1)
    if paged is None:
        h_n = nsa_prompt(nq, ng, kv_new, g_q, g_kc, g_ks, g_kw, pe_ck, w_ck1, w_ck2, pe_cv, w_cv1, w_cv2)
    else:
        gates_n = jax.nn.sigmoid(ng).reshape(B, T, N_HEADS, 3)
        h_n = nsa_sample(qn, gates_n, paged[0], paged[1], kv_rows, win_all, g_kc, g_ks, g_kw,
                         pe_ck, w_ck1, w_ck2, pe_cv, w_cv1, w_cv2)
    win_state = win_all[:, -min(WINDOW, WB + T):]
    if use_pallas_proj:
        assert _PROJ_OFF["gate_m"] == 0 and _PROJ_OFF["gate_n"] == D
        x1, h2, logits = out_proj(x.reshape(B * T, D), proj2d, (0, 1), h_m.reshape(B * T, M_VW),
                                  h_n.reshape(B * T, N_QW), w_proj_m, w_proj_n, w_out, g_ffn,
                                  w_group, b_group, w_expert, b_expert)
        x1 = x1.reshape(B, T, D)
        lg, le = logits[:, :N_GROUPS], logits[:, N_GROUPS:N_GROUPS + N_EXPERTS]
    else:
        u = jax.nn.sigmoid(gate_m) * (h_m @ w_proj_m) + jax.nn.sigmoid(gate_n) * (h_n.astype(x.dtype) @ w_proj_n)
        x1 = x + u @ w_out
        h2 = rms_norm(x1, g_ffn).reshape(B * T, D)
        lg, le = h2 @ w_group + b_group, h2 @ w_expert + b_expert
    eid, wts = moe_route(lg, le)
    return x1, h2, eid, wts, kv_rows, win_state, C1, n1, m1, conv_state


def kernel(x_prompt, x_sample, cache_kv, page_table, state_win, state_C, state_n, state_m, state_conv, g_mix, w_in, w_conv, b_i, b_f, g_mh, g_q, g_kc, g_ks, g_kw, pe_ck, w_ck1, w_ck2, pe_cv, w_cv1, w_cv2, w_proj_m, w_proj_n, w_out, g_ffn, w_group, b_group, w_expert, b_expert, w_e1, w_e3, w_e2):
    B = x_prompt.shape[0]
    dt = x_prompt.dtype
    weights = (g_mix, w_in, w_conv, b_i, b_f, g_mh, g_q, g_kc, g_ks, g_kw, pe_ck, w_ck1, w_ck2, pe_cv, w_cv1, w_cv2,
               w_proj_m, w_proj_n, w_out, g_ffn, w_group, b_group, w_expert, b_expert, w_e1, w_e3, w_e2)
    x1_p, h2_p, eid_p, wts_p, kv_p, win_p, C_p, n_p, m_p, conv_p = hybrid_layer(
        x_prompt, jnp.zeros((B, M_CONV - 1, 2 * M_QK), dt), jnp.zeros((B, M_HEADS, M_DQK, M_DV), F32),
        jnp.zeros((B, M_HEADS, M_DQK), F32), jnp.zeros((B, M_HEADS), F32),
        None, jnp.zeros((B, 0, 2, N_KV, HEAD_DIM), dt), *weights, use_pallas_proj=True)
    with jax.default_matmul_precision("highest"):
        x1_s, h2_s, eid_s, wts_s, kv_s, win_s, C_s, n_s, m_s, conv_s = hybrid_layer(
            x_sample, state_conv, state_C, state_n, state_m, (cache_kv, page_table), state_win, *weights,
            use_pallas_proj=False)
    n_p_tok = h2_p.shape[0]
    moe = hier_moe_apply(jnp.concatenate([h2_p, h2_s], axis=0), jnp.concatenate([eid_p, eid_s], axis=0),
                         jnp.concatenate([wts_p, wts_s], axis=0), w_e1, w_e3, w_e2)
    y_p = x1_p + moe[:n_p_tok].reshape(x1_p.shape)
    y_s = x1_s + moe[n_p_tok:].reshape(x1_s.shape)
    return (y_p, y_s, kv_p, kv_s, win_p, win_s, C_p, C_s, n_p, n_s, m_p, m_s, conv_p, conv_s)
```

```python
import functools
import math

import jax
import jax.numpy as jnp
import numpy as np
from jax import lax
from jax.experimental import pallas as pl
from jax.experimental.pallas import tpu as pltpu

D_MODEL = 2048
M_HEADS = 4
M_DQK = 128
M_DV = 256
M_CONV = 4
M_CHUNK = 64
MLSTM_SCAN_CHUNK = 256
N_HEADS = 16
N_KV = 4
HEAD_DIM = 64
CMP_BLK = 32
SEL_BLK = 64
N_SELECT = 16
WINDOW = 512
NSA_QBLK = 64
FORCE_SCORE = 1e4
N_GROUPS = 4
EXPERTS_PER_GROUP = 8
N_EXPERTS = N_GROUPS * EXPERTS_PER_GROUP
TOP_K_FINE = 2
MOE_BLK = 128
EPS = 1e-6
M_QK = M_HEADS * M_DQK
M_VW = M_HEADS * M_DV
N_QW = N_HEADS * HEAD_DIM
N_KVW = 6 * N_KV * HEAD_DIM
SPLITS = (M_QK, M_QK, M_VW, M_HEADS, M_HEADS, M_VW, N_QW, N_KVW, 3 * N_HEADS, D_MODEL, D_MODEL)
D_IN = sum(SPLITS)

F32 = jnp.float32
BF16 = jnp.bfloat16
VMEM_LIMIT = 48 * 1024 * 1024


def _mxu(a, b):
    return jnp.dot(a, b, preferred_element_type=F32, precision=lax.Precision.DEFAULT)


def _norm_proj_kernel(x_ref, g_ref, w_ref, o_ref, hx_ref):
    @pl.when(pl.program_id(1) == 0)
    def _():
        x = x_ref[...]
        r = lax.rsqrt(jnp.mean(x * x, axis=-1, keepdims=True) + EPS)
        hx_ref[...] = (x * r * g_ref[...]).astype(BF16)

    o_ref[...] = _mxu(hx_ref[...], w_ref[...])


def norm_proj(x, g, w_bf16, *, tm=1024, tn=896):
    n, d = x.shape
    k = w_bf16.shape[1]
    return pl.pallas_call(
        _norm_proj_kernel,
        grid=(n // tm, k // tn),
        in_specs=[
            pl.BlockSpec((tm, d), lambda i, j: (i, 0)),
            pl.BlockSpec((1, d), lambda i, j: (0, 0)),
            pl.BlockSpec((d, tn), lambda i, j: (0, j)),
        ],
        out_specs=pl.BlockSpec((tm, tn), lambda i, j: (i, j)),
        out_shape=jax.ShapeDtypeStruct((n, k), F32),
        scratch_shapes=[pltpu.VMEM((tm, d), BF16)],
        compiler_params=pltpu.CompilerParams(
            dimension_semantics=("parallel", "arbitrary"), vmem_limit_bytes=VMEM_LIMIT),
        name="norm_proj",
    )(x, g.reshape(1, d), w_bf16)


def _rms_rows_kernel(x_ref, g_ref, o_ref):
    x = x_ref[...]
    r = lax.rsqrt(jnp.mean(x * x, axis=-1, keepdims=True) + EPS)
    o_ref[...] = (x * r * g_ref[...]).astype(o_ref.dtype)


def rms_rows(x, g, out_dtype, *, tr=2048):
    rows, c = x.shape
    tr = min(tr, rows)
    return pl.pallas_call(
        _rms_rows_kernel,
        grid=(rows // tr,),
        in_specs=[pl.BlockSpec((tr, c), lambda i: (i, 0)), pl.BlockSpec((1, c), lambda i: (0, 0))],
        out_specs=pl.BlockSpec((tr, c), lambda i: (i, 0)),
        out_shape=jax.ShapeDtypeStruct((rows, c), out_dtype),
        compiler_params=pltpu.CompilerParams(dimension_semantics=("parallel",)),
        name="rms_rows",
    )(x, g.reshape(1, c))


def _split_bf16(x):
    hi = lax.bitcast_convert_type(
        lax.bitcast_convert_type(x, jnp.int32) & jnp.int32(-65536), F32)
    return hi.astype(BF16), (x - hi).astype(BF16)


def _dot_split(a, b_hi, b_lo):
    a_hi, a_lo = _split_bf16(a)
    return _mxu(a_hi, b_hi) + (_mxu(a_lo, b_hi) + _mxu(a_hi, b_lo))


def _compress_kernel(x_ref, pe_ref, w1_ref, w1l_ref, w2_ref, w2l_ref, g_ref, o_ref, *, normalize, precise):
    x = x_ref[...] + pe_ref[...]
    if precise:
        h = _dot_split(x, w1_ref[...], w1l_ref[...])
    else:
        h = _mxu(x.astype(BF16), w1_ref[...])
    h = h * jax.nn.sigmoid(h)
    if precise:
        y = _dot_split(h, w2_ref[...], w2l_ref[...])
    else:
        y = _mxu(h.astype(BF16), w2_ref[...])
    if normalize:
        y = y * lax.rsqrt(jnp.mean(y * y, axis=-1, keepdims=True) + EPS) * g_ref[...]
    o_ref[...] = y.astype(o_ref.dtype)


def compress_rows(x, pe, w1, w2, g, *, normalize, precise=False, tr=512):
    rows, k = x.shape
    f = w1.shape[-1]
    d = w2.shape[1]
    tr = min(tr, rows)
    w1_hi, w1_lo = _split_bf16(w1.reshape(k, f))
    w2_hi, w2_lo = _split_bf16(w2)
    if not precise:
        w1_hi, w2_hi = w1.reshape(k, f).astype(BF16), w2.astype(BF16)
    return pl.pallas_call(
        functools.partial(_compress_kernel, normalize=normalize, precise=precise),
        grid=(rows // tr,),
        in_specs=[
            pl.BlockSpec((tr, k), lambda i: (i, 0)),
            pl.BlockSpec((1, k), lambda i: (0, 0)),
            pl.BlockSpec((k, f), lambda i: (0, 0)),
            pl.BlockSpec((k, f), lambda i: (0, 0)),
            pl.BlockSpec((f, d), lambda i: (0, 0)),
            pl.BlockSpec((f, d), lambda i: (0, 0)),
            pl.BlockSpec((1, d), lambda i: (0, 0)),
        ],
        out_specs=pl.BlockSpec((tr, d), lambda i: (i, 0)),
        out_shape=jax.ShapeDtypeStruct((rows, d), F32 if precise else BF16),
        compiler_params=pltpu.CompilerParams(dimension_semantics=("parallel",)),
        name="nsa_compress",
    )(x, pe.reshape(1, k), w1_hi, w1_lo, w2_hi, w2_lo, g.reshape(1, d))


CMP_TILE_ROWS = 128


def _compress_paged_kernel(x_ref, pe_ref, w_hi_ref, w_lo_ref, w2_hi_ref, w2_lo_ref, g_ref, o_ref, *,
                           normalize, cpp, hidden):
    rows = x_ref.shape[0]
    hd = x_ref.shape[1]
    acc = jnp.zeros((rows, cpp * hidden), F32)
    for dp in range(hd // 2):
        a = x_ref[:, 2 * dp, :] + pe_ref[pl.ds(2 * dp, 1), :]
        b = x_ref[:, 2 * dp + 1, :] + pe_ref[pl.ds(2 * dp + 1, 1), :]
        acc = acc + _dot_split(jnp.concatenate([a, b], axis=1), w_hi_ref[dp], w_lo_ref[dp])
    hid = acc * jax.nn.sigmoid(acc)
    for c in range(cpp):
        y = _dot_split(hid[:, c * hidden:(c + 1) * hidden], w2_hi_ref[...], w2_lo_ref[...])
        if normalize:
            y = y * lax.rsqrt(jnp.mean(y * y, axis=-1, keepdims=True) + EPS) * g_ref[...]
        o_ref[:, c * hd:(c + 1) * hd] = y


def compress_paged(x_tiles, pe, w1, w2, g, *, normalize):
    r, hd, page = x_tiles.shape
    cpp = page // CMP_BLK
    hidden = w1.shape[-1]
    eye = jnp.eye(cpp, dtype=F32)
    w_big = jnp.einsum('cC,ldf->dclCf', eye, w1).reshape(hd // 2, 2 * page, cpp * hidden)
    w_hi, w_lo = _split_bf16(w_big)
    w2_hi, w2_lo = _split_bf16(w2)
    pe_t = jnp.tile(pe.T, (1, cpp))
    const3 = lambda i: (0, 0, 0)
    const2 = lambda i: (0, 0)
    return pl.pallas_call(
        functools.partial(_compress_paged_kernel, normalize=normalize, cpp=cpp, hidden=hidden),
        grid=(r // CMP_TILE_ROWS,),
        in_specs=[
            pl.BlockSpec((CMP_TILE_ROWS, hd, page), lambda i: (i, 0, 0)),
            pl.BlockSpec((hd, page), const2),
            pl.BlockSpec((hd // 2, 2 * page, cpp * hidden), const3, pipeline_mode=pl.Buffered(1)),
            pl.BlockSpec((hd // 2, 2 * page, cpp * hidden), const3, pipeline_mode=pl.Buffered(1)),
            pl.BlockSpec((hidden, hd), const2),
            pl.BlockSpec((hidden, hd), const2),
            pl.BlockSpec((1, hd), const2),
        ],
        out_specs=pl.BlockSpec((CMP_TILE_ROWS, cpp * hd), lambda i: (i, 0)),
        out_shape=jax.ShapeDtypeStruct((r, cpp * hd), F32),
        compiler_params=pltpu.CompilerParams(dimension_semantics=("parallel",), vmem_limit_bytes=VMEM_LIMIT),
        name="nsa_compress_paged",
    )(x_tiles, pe_t, w_hi, w_lo, w2_hi, w2_lo, g.reshape(1, hd))


NSA_TQ = 128
NSA_TK = 512
HPG = N_HEADS // N_KV
NSA_Q = HPG * NSA_TQ
NEG_BIG = -1e30


def _nsa_prompt_kernel(qT_ref, gq_ref, gate_ref, kc_ref, vcT_ref, ks_ref, vsT_ref, kw_ref, vwT_ref,
                       o_ref, sc_ref, sel_ref, imp_ref, *, n_cmp, n_sel):
    j = pl.program_id(2)
    s0 = j * NSA_TQ
    q = qT_ref[0, 0, 0]
    r = lax.rsqrt(jnp.mean(q * q, axis=0, keepdims=True) + EPS)
    qn = (q * r * gq_ref[...] * (HEAD_DIM ** -0.5 * math.log2(math.e))).astype(BF16)
    lane = lax.broadcasted_iota(jnp.int32, (1, NSA_Q), 1)
    t_row = s0 + (lane & (NSA_TQ - 1))

    s = _mxu(kc_ref[0, 0], qn)
    c_end = lax.broadcasted_iota(jnp.int32, (n_cmp, NSA_Q), 0) * CMP_BLK + (CMP_BLK - 1)
    mask_c = c_end <= t_row
    s = jnp.where(mask_c, s, -jnp.inf)
    m = jnp.max(s, axis=0, keepdims=True)
    m = jnp.where(m > -jnp.inf, m, 0.0)
    e = jnp.where(mask_c, jnp.exp2(s - m), 0.0)
    p_c = e / jnp.maximum(jnp.sum(e, axis=0, keepdims=True), 1e-30)
    o_c = _mxu(vcT_ref[0, 0], p_c.astype(BF16))

    imp_ref[...] = (p_c[:, 0:NSA_TQ] + p_c[:, NSA_TQ:2 * NSA_TQ]
                    + p_c[:, 2 * NSA_TQ:3 * NSA_TQ] + p_c[:, 3 * NSA_TQ:4 * NSA_TQ])
    imp = imp_ref[pl.ds(0, n_sel, stride=2), :] + imp_ref[pl.ds(1, n_sel, stride=2), :]
    n_iota = lax.broadcasted_iota(jnp.int32, (n_sel, NSA_TQ), 0)
    cur = (s0 + lax.broadcasted_iota(jnp.int32, (n_sel, NSA_TQ), 1)) // SEL_BLK
    valid = n_iota <= cur
    forced = (n_iota == 0) | (n_iota == cur) | (n_iota == cur - 1)
    score = jnp.where(valid, jnp.where(forced, FORCE_SCORE, imp), -jnp.inf)
    sc_ref[...] = score

    def rank_body(n2, rank):
        row = jnp.broadcast_to(sc_ref[pl.ds(n2, 1), :], (n_sel, NSA_TQ))
        beats = (row > score) | ((row == score) & (n2 < n_iota))
        return rank + jnp.where(beats, 1.0, 0.0)

    n_live = jnp.minimum((s0 + NSA_TQ - 1) // SEL_BLK + 1, n_sel)
    rank = lax.fori_loop(0, n_live, rank_body, jnp.zeros((n_sel, NSA_TQ), F32))
    sel_ref[...] = jnp.where(valid & (rank < N_SELECT), 0.0, NEG_BIG)

    key_iota = lax.broadcasted_iota(jnp.int32, (NSA_TK, NSA_TQ), 0)
    t_q = s0 + lax.broadcasted_iota(jnp.int32, (1, NSA_TQ), 1)

    def attend(k_ref, vT_ref, c_lo, c_hi, bias_fn):
        def body(c, carry):
            m_i, l_i, acc = carry
            k = k_ref[0, 0, pl.ds(pl.multiple_of(c * NSA_TK, NSA_TK), NSA_TK), :]
            sk = _mxu(k, qn)
            bias = bias_fn(c)
            ms, ls, ps = [], [], []
            for h in range(HPG):
                sl = slice(h * NSA_TQ, (h + 1) * NSA_TQ)
                s_h = sk[:, sl] + bias
                m_h = jnp.maximum(m_i[:, sl], jnp.max(s_h, axis=0, keepdims=True))
                p_h = jnp.exp2(s_h - m_h)
                ms.append(m_h)
                ls.append(jnp.sum(p_h, axis=0, keepdims=True))
                ps.append(p_h.astype(BF16))
            m_new = jnp.concatenate(ms, axis=1)
            alpha = jnp.exp2(m_i - m_new)
            l_new = alpha * l_i + jnp.concatenate(ls, axis=1)
            acc = alpha * acc + _mxu(vT_ref[0, 0, c], jnp.concatenate(ps, axis=1))
            return m_new, l_new, acc

        init = (jnp.full((1, NSA_Q), NEG_BIG, F32), jnp.zeros((1, NSA_Q), F32),
                jnp.zeros((HEAD_DIM, NSA_Q), F32))
        _, l_f, acc_f = lax.fori_loop(c_lo, c_hi, body, init)
        return acc_f / jnp.maximum(l_f, 1e-30)

    def sel_bias(c):
        pos = c * NSA_TK + key_iota
        per_blk = [jnp.broadcast_to(
            sel_ref[pl.ds(jnp.minimum(c * (NSA_TK // SEL_BLK) + i, n_sel - 1), 1), :], (SEL_BLK, NSA_TQ))
            for i in range(NSA_TK // SEL_BLK)]
        return jnp.where(pos <= t_q, jnp.concatenate(per_blk, axis=0), NEG_BIG)

    def win_bias(c):
        pos = c * NSA_TK + key_iota
        return jnp.where((pos <= t_q) & (pos > t_q - WINDOW), 0.0, NEG_BIG)

    c_hi = (s0 + NSA_TQ + NSA_TK - 1) // NSA_TK
    o_s = attend(ks_ref, vsT_ref, 0, c_hi, sel_bias)
    o_w = attend(kw_ref, vwT_ref, jnp.maximum(s0 - (WINDOW - 1), 0) // NSA_TK, c_hi, win_bias)

    g = jax.nn.sigmoid(gate_ref[0, 0, 0])
    o_ref[0, 0, 0] = g[0:1] * o_c + g[1:2] * o_s + g[2:3] * o_w


def nsa_prompt(nq, ng, kv_new, g_q, g_kc, g_ks, g_kw, pe_ck, w_ck1, w_ck2, pe_cv, w_cv1, w_cv2):
    B, T = nq.shape[:2]
    G = N_KV
    nqb = T // NSA_TQ
    n_cmp = T // CMP_BLK
    n_sel = T // SEL_BLK
    nch = T // NSA_TK
    qT = nq.reshape(B, nqb, NSA_TQ, G, HPG, HEAD_DIM).transpose(0, 3, 1, 5, 4, 2).reshape(B, G, nqb, HEAD_DIM, NSA_Q)
    gT = ng.reshape(B, nqb, NSA_TQ, G, HPG, 3).transpose(0, 3, 1, 5, 4, 2).reshape(B, G, nqb, 3, NSA_Q)
    kv_g = kv_new.transpose(2, 0, 3, 1, 4)
    xk = kv_g[0].reshape(B * G * n_cmp, CMP_BLK * HEAD_DIM)
    xv = kv_g[1].reshape(B * G * n_cmp, CMP_BLK * HEAD_DIM)
    kc = compress_rows(xk, pe_ck, w_ck1, w_ck2, g_kc, normalize=True).reshape(B, G, n_cmp, HEAD_DIM)
    vc = compress_rows(xv, pe_cv, w_cv1, w_cv2, g_kc, normalize=False).reshape(B, G, n_cmp, HEAD_DIM)
    vcT = vc.transpose(0, 1, 3, 2)
    ks = rms_rows(kv_g[2].reshape(B * G * T, HEAD_DIM), g_ks, BF16).reshape(B, G, T, HEAD_DIM)
    kw = rms_rows(kv_g[4].reshape(B * G * T, HEAD_DIM), g_kw, BF16).reshape(B, G, T, HEAD_DIM)
    vsT = kv_g[3].astype(BF16).reshape(B, G, nch, NSA_TK, HEAD_DIM).transpose(0, 1, 2, 4, 3)
    vwT = kv_g[5].astype(BF16).reshape(B, G, nch, NSA_TK, HEAD_DIM).transpose(0, 1, 2, 4, 3)

    bg = lambda b, g, j: (b, g, 0, 0)
    bg5 = lambda b, g, j: (b, g, 0, 0, 0)
    bgj = lambda b, g, j: (b, g, j, 0, 0)
    oT = pl.pallas_call(
        functools.partial(_nsa_prompt_kernel, n_cmp=n_cmp, n_sel=n_sel),
        grid=(B, G, nqb),
        in_specs=[
            pl.BlockSpec((1, 1, 1, HEAD_DIM, NSA_Q), bgj),
            pl.BlockSpec((HEAD_DIM, 1), lambda b, g, j: (0, 0)),
            pl.BlockSpec((1, 1, 1, 3, NSA_Q), bgj),
            pl.BlockSpec((1, 1, n_cmp, HEAD_DIM), bg),
            pl.BlockSpec((1, 1, HEAD_DIM, n_cmp), bg),
            pl.BlockSpec((1, 1, T, HEAD_DIM), bg),
            pl.BlockSpec((1, 1, nch, HEAD_DIM, NSA_TK), bg5),
            pl.BlockSpec((1, 1, T, HEAD_DIM), bg),
            pl.BlockSpec((1, 1, nch, HEAD_DIM, NSA_TK), bg5),
        ],
        out_specs=pl.BlockSpec((1, 1, 1, HEAD_DIM, NSA_Q), bgj),
        out_shape=jax.ShapeDtypeStruct((B, G, nqb, HEAD_DIM, NSA_Q), F32),
        scratch_shapes=[pltpu.VMEM((n_sel, NSA_TQ), F32), pltpu.VMEM((n_sel, NSA_TQ), F32),
                        pltpu.VMEM((n_cmp, NSA_TQ), F32)],
        compiler_params=pltpu.CompilerParams(
            dimension_semantics=("parallel", "parallel", "arbitrary"), vmem_limit_bytes=VMEM_LIMIT),
        name="nsa_prompt",
    )(qT, g_q.reshape(HEAD_DIM, 1), gT, kc, vcT, ks, vsT, kw, vwT)
    return oT.reshape(B, G, nqb, HEAD_DIM, HPG, NSA_TQ).transpose(0, 2, 5, 1, 4, 3).reshape(B, T, N_QW)


def rms_norm(x, g):
    xf = x.astype(F32)
    y = xf * lax.rsqrt(jnp.mean(xf * xf, axis=-1, keepdims=True) + EPS)
    return (y * g.astype(F32)).astype(x.dtype)


def masked_softmax(s, mask):
    s = jnp.where(mask, s.astype(F32), -jnp.inf)
    m = jnp.max(s, axis=-1, keepdims=True)
    m = jnp.where(jnp.isfinite(m), m, 0.0)
    e = jnp.where(mask, jnp.exp(s - m), 0.0)
    return e / jnp.maximum(jnp.sum(e, axis=-1, keepdims=True), 1e-30)


def short_conv(u, prev, w):
    T = u.shape[1]
    full = jnp.concatenate([prev.astype(u.dtype), u], axis=1)
    out = full[:, 0:T] * w[0]
    for j in range(1, M_CONV):
        out = out + full[:, j:j + T] * w[j]
    return out, full[:, T:]


def mlstm_chunkwise(q, k, v, i_pre, logf, C0, n0, m0):
    B, T = q.shape[:2]
    L = MLSTM_SCAN_CHUNK if T % MLSTM_SCAN_CHUNK == 0 else (M_CHUNK if T % M_CHUNK == 0 else T)
    nc = T // L

    def chunks(a):
        a = a.reshape((B, nc, L) + a.shape[2:])
        return jnp.moveaxis(jnp.moveaxis(a, 3, 2), 1, 0)

    causal = jnp.tril(jnp.ones((L, L), bool))

    def step(carry, xs):
        C, n, m = carry
        qc, kc, vc, ic, fc = xs
        b = jnp.cumsum(fc, axis=-1)
        logD = jnp.where(causal, b[..., :, None] - b[..., None, :] + ic[..., None, :], -jnp.inf)
        inter = b + m[..., None]
        m_t = jnp.maximum(inter, jnp.max(logD, axis=-1))
        a = jnp.exp(inter - m_t)
        S = jnp.einsum('bhtd,bhsd->bhts', qc, kc) * jnp.exp(logD - m_t[..., None])
        num = a[..., None] * jnp.einsum('bhtd,bhde->bhte', qc, C) + jnp.einsum('bhts,bhse->bhte', S, vc)
        den = a * jnp.einsum('bhtd,bhd->bht', qc, n) + jnp.sum(S, axis=-1)
        h = num / jnp.maximum(jnp.abs(den), jnp.exp(-m_t))[..., None]
        m_new = m_t[..., -1]
        w = jnp.exp(b[..., -1:] - b + ic - m_new[..., None])
        aL = jnp.exp(b[..., -1] + m - m_new)
        C_new = aL[..., None, None] * C + jnp.einsum('bhs,bhsd,bhse->bhde', w, kc, vc)
        n_new = aL[..., None] * n + jnp.einsum('bhs,bhsd->bhd', w, kc)
        return (C_new, n_new, m_new), h

    (C, n, m), h = lax.scan(step, (C0, n0, m0), (chunks(q), chunks(k), chunks(v), chunks(i_pre), chunks(logf)))
    h = jnp.moveaxis(jnp.moveaxis(h, 0, 1), 2, 3).reshape(B, T, M_HEADS, M_DV)
    return h, C, n, m


def compress(rows, pe, w1, w2):
    B, Lk = rows.shape[:2]
    nc = Lk // CMP_BLK
    blk = rows[:, :nc * CMP_BLK].reshape(B, nc, CMP_BLK, N_KV, HEAD_DIM) + pe[:, None, :]
    hid = jax.nn.silu(jnp.einsum('bclgd,ldf->bcgf', blk, w1))
    return jnp.einsum('bcgf,fd->bcgd', hid, w2)


def nsa_attention(q, gates, kv_all, kw_full, q0, g_kc, g_ks, g_kw, pe_ck, w_ck1, w_ck2, pe_cv, w_cv1, w_cv2):
    B, T = q.shape[:2]
    Lk = kv_all.shape[1]
    hpg = N_HEADS // N_KV
    scale = HEAD_DIM ** -0.5
    kc = rms_norm(compress(kv_all[:, :, 0], pe_ck, w_ck1, w_ck2), g_kc)
    vc = compress(kv_all[:, :, 1], pe_cv, w_cv1, w_cv2)
    nc = kc.shape[1]
    n_sel = -(-Lk // SEL_BLK)
    pad = n_sel * SEL_BLK - Lk
    ks = jnp.pad(rms_norm(kv_all[:, :, 2], g_ks), ((0, 0), (0, pad), (0, 0), (0, 0)))
    ks = ks.reshape(B, n_sel, SEL_BLK, N_KV, HEAD_DIM).transpose(0, 3, 1, 2, 4)
    vs = jnp.pad(kv_all[:, :, 3], ((0, 0), (0, pad), (0, 0), (0, 0)))
    vs = vs.reshape(B, n_sel, SEL_BLK, N_KV, HEAD_DIM).transpose(0, 3, 1, 2, 4)
    kw = rms_norm(kw_full[:, :, 0], g_kw)
    vw = kw_full[:, :, 1]
    top = min(N_SELECT, n_sel)
    ratio = SEL_BLK // CMP_BLK
    qb = NSA_QBLK if T % NSA_QBLK == 0 else T
    nqb = T // qb
    qg = q.reshape(B, nqb, qb, N_KV, hpg, HEAD_DIM).transpose(1, 0, 3, 4, 2, 5)
    gg = jnp.moveaxis(gates.reshape(B, nqb, qb, N_HEADS, 3), 1, 0)
    cmp_end = (jnp.arange(nc) + 1) * CMP_BLK - 1
    blk_ids = jnp.arange(n_sel)
    b_ix = jnp.arange(B)[:, None, None, None]
    g_ix = jnp.arange(N_KV)[None, :, None, None]

    def block(args):
        qi, gi, j = args
        s0 = j * qb
        t = q0 + s0 + jnp.arange(qb)
        sc = jnp.einsum('bghqd,bcgd->bghqc', qi, kc) * scale
        p_c = masked_softmax(sc, cmp_end[None, :] <= t[:, None])
        o_c = jnp.einsum('bghqc,bcgd->bghqd', p_c.astype(vc.dtype), vc)
        imp = jnp.sum(p_c, axis=2)
        imp = jnp.pad(imp, ((0, 0), (0, 0), (0, 0), (0, n_sel * ratio - nc)))
        imp = imp.reshape(B, N_KV, qb, n_sel, ratio).sum(-1)
        cur = t // SEL_BLK
        valid = blk_ids[None, :] <= cur[:, None]
        forced = (blk_ids[None, :] == 0) | (blk_ids[None, :] == cur[:, None]) | (blk_ids[None, :] == cur[:, None] - 1)
        imp = jnp.where(valid, jnp.where(forced, FORCE_SCORE, imp), -jnp.inf)
        top_v, top_i = lax.top_k(imp, top)
        k_sel = ks[b_ix, g_ix, top_i]
        v_sel = vs[b_ix, g_ix, top_i]
        pos = top_i[..., None] * SEL_BLK + jnp.arange(SEL_BLK)
        m_s = (pos <= t[:, None, None]) & jnp.isfinite(top_v)[..., None]
        ss = (jnp.einsum('bghqd,bgqnkd->bghqnk', qi, k_sel) * scale).reshape(B, N_KV, hpg, qb, top * SEL_BLK)
        p_s = masked_softmax(ss, m_s.reshape(B, N_KV, 1, qb, top * SEL_BLK))
        o_s = jnp.einsum('bghqnk,bgqnkd->bghqd', p_s.reshape(B, N_KV, hpg, qb, top, SEL_BLK).astype(v_sel.dtype), v_sel)
        kwi = lax.dynamic_slice_in_dim(kw, s0, qb + WINDOW, axis=1)
        vwi = lax.dynamic_slice_in_dim(vw, s0, qb + WINDOW, axis=1)
        pw = q0 - WINDOW + s0 + jnp.arange(qb + WINDOW)
        m_w = (pw[None, :] <= t[:, None]) & (pw[None, :] > t[:, None] - WINDOW) & (pw[None, :] >= 0)
        sw = jnp.einsum('bghqd,bkgd->bghqk', qi, kwi) * scale
        p_w = masked_softmax(sw, m_w)
        o_w = jnp.einsum('bghqk,bkgd->bghqd', p_w.astype(vwi.dtype), vwi)
        o = jnp.stack([o_c, o_s, o_w], axis=-1)
        gi_r = gi.reshape(B, qb, N_KV, hpg, 3).transpose(0, 2, 3, 1, 4)
        return jnp.einsum('bghqdc,bghqc->bqghd', o, gi_r.astype(o.dtype))

    out = lax.map(block, (qg, gg, jnp.arange(nqb)))
    return jnp.moveaxis(out, 0, 1).reshape(B, T, N_QW)


def nsa_sample(q, gates, cache_kv, page_table, kv_rows, win_all, g_kc, g_ks, g_kw,
               pe_ck, w_ck1, w_ck2, pe_cv, w_cv1, w_cv2):
    B, T = q.shape[:2]
    n_pages = page_table.shape[1]
    page = cache_kv.shape[1]
    P = n_pages * page
    G, hd = N_KV, HEAD_DIM
    assert T < CMP_BLK and T <= SEL_BLK and P % SEL_BLK == 0 and page % SEL_BLK == 0
    assert win_all.shape[1] == WINDOW + T
    nc = P // CMP_BLK
    cpp = page // CMP_BLK
    n_last = P // SEL_BLK
    n_sel = n_last + 1
    top = min(N_SELECT, n_sel)
    ratio = SEL_BLK // CMP_BLK
    scale = hd ** -0.5
    t = P + jnp.arange(T)
    qg = q.reshape(B, T, G, HPG, hd)

    n_kind = cache_kv.shape[2]
    cache_r = cache_kv.transpose(0, 2, 3, 4, 1).reshape(cache_kv.shape[0] * n_kind * G, hd, page)
    tile_row = lambda p, kind, g: (p * n_kind + kind) * G + g

    def cmp_tiles(kind):
        rows = tile_row(page_table[:, :, None], kind, jnp.arange(G)[None, None, :])
        return cache_r[rows.reshape(-1)]

    def cmp_rows(y):
        return y.reshape(B, n_pages, G, cpp, hd).transpose(0, 2, 1, 3, 4).reshape(B, G, nc, hd)

    kc = cmp_rows(compress_paged(cmp_tiles(0), pe_ck, w_ck1, w_ck2, g_kc, normalize=True))
    vc = cmp_rows(compress_paged(cmp_tiles(1), pe_cv, w_cv1, w_cv2, g_kc, normalize=False))
    cmp_end = (jnp.arange(nc) + 1) * CMP_BLK - 1
    sc = jnp.einsum('btghd,bgcd->bghtc', qg, kc) * scale
    p_c = masked_softmax(sc, cmp_end[None, :] <= t[:, None])
    o_c = jnp.einsum('bghtc,bgcd->bghtd', p_c, vc)

    imp = jnp.sum(p_c, axis=2)
    imp = jnp.pad(imp, ((0, 0), (0, 0), (0, 0), (0, n_sel * ratio - nc)))
    imp = imp.reshape(B, G, T, n_sel, ratio).sum(-1)
    blk_ids = jnp.arange(n_sel)
    cur = t // SEL_BLK
    valid = blk_ids[None, :] <= cur[:, None]
    forced = (blk_ids[None, :] == 0) | (blk_ids[None, :] == cur[:, None]) | (blk_ids[None, :] == cur[:, None] - 1)
    imp = jnp.where(valid, jnp.where(forced, FORCE_SCORE, imp), -jnp.inf)
    ids = jnp.arange(n_sel)
    rest, top_v, top_i = imp, [], []
    for _ in range(top):
        i_max = jnp.argmax(rest, axis=-1)
        top_i.append(i_max)
        top_v.append(jnp.max(rest, axis=-1))
        rest = jnp.where(ids == i_max[..., None], -jnp.inf, rest)
    top_v, top_i = jnp.stack(top_v, axis=-1), jnp.stack(top_i, axis=-1)

    bpp = page // SEL_BLK
    blk_c = jnp.minimum(top_i, n_last - 1)
    b_ix = jnp.arange(B)[:, None, None, None]
    phys = page_table[b_ix, blk_c // bpp]
    g_ix = jnp.arange(G)[None, :, None, None]
    tiles = jnp.stack([cache_r[tile_row(phys, 2, g_ix)], cache_r[tile_row(phys, 3, g_ix)]], axis=4)
    rms_d = lambda k: k * lax.rsqrt(jnp.mean(k * k, axis=-2, keepdims=True) + EPS) * g_ks[:, None]
    k_t = rms_d(tiles[..., 0, :, :])
    v_t = tiles[..., 1, :, :]
    new_tile = jnp.pad(kv_rows[:, :, 2:4], ((0, 0), (0, page - T), (0, 0), (0, 0), (0, 0)))
    new_tile = new_tile.transpose(0, 3, 2, 4, 1)
    k_n, v_n = rms_d(new_tile[:, :, 0]), new_tile[:, :, 1]
    is_new = top_i == n_last
    picked = jnp.isfinite(top_v)
    lane = jnp.arange(page)
    pos = (blk_c // bpp * bpp)[..., None] * SEL_BLK + lane
    m_s = ((pos // SEL_BLK == top_i[..., None]) & (pos <= t[:, None, None])
           & (picked & ~is_new)[..., None])
    pos_n = n_last * SEL_BLK + lane
    m_n = (jnp.any(is_new & picked, axis=-1)[..., None] & (pos_n // SEL_BLK == n_last)
           & (pos_n[None, :] <= t[:, None]))
    ss = (jnp.einsum('btghd,bgtndp->bghtnp', qg, k_t) * scale).reshape(B, G, HPG, T, top * page)
    ss_n = jnp.einsum('btghd,bgdp->bghtp', qg, k_n) * scale
    p_all = masked_softmax(jnp.concatenate([ss, ss_n], axis=-1),
                           jnp.concatenate([m_s.reshape(B, G, 1, T, top * page), m_n[:, :, None]], axis=-1))
    o_s = (jnp.einsum('bghtnp,bgtndp->bghtd', p_all[..., :top * page].reshape(B, G, HPG, T, top, page), v_t)
           + jnp.einsum('bghtp,bgdp->bghtd', p_all[..., top * page:], v_n))

    kw = rms_norm(win_all[:, :, 0], g_kw)
    vw = win_all[:, :, 1]
    pw = P - WINDOW + jnp.arange(T + WINDOW)
    m_w = (pw[None, :] <= t[:, None]) & (pw[None, :] > t[:, None] - WINDOW) & (pw[None, :] >= 0)
    sw = jnp.einsum('btghd,bkgd->bghtk', qg, kw) * scale
    p_w = masked_softmax(sw, m_w)
    o_w = jnp.einsum('bghtk,bkgd->bghtd', p_w, vw)

    o = jnp.stack([o_c, o_s, o_w], axis=-1)
    return jnp.einsum('bghtdc,btghc->btghd', o, gates.reshape(B, T, G, HPG, 3)).reshape(B, T, N_QW)


def moe_route(group_logits, expert_logits):
    assert TOP_K_FINE == 2
    N = group_logits.shape[0]
    pg = jax.nn.softmax(group_logits.astype(F32), axis=-1)
    grp = jnp.argmax(pg, axis=-1)
    p_grp = jnp.max(pg, axis=-1)
    le = expert_logits.astype(F32).reshape(N, N_GROUPS, EXPERTS_PER_GROUP)
    pe = jax.nn.softmax(le[jnp.arange(N), grp], axis=-1)
    i1 = jnp.argmax(pe, axis=-1)
    hit1 = jnp.arange(EXPERTS_PER_GROUP)[None, :] == i1[:, None]
    rest = jnp.where(hit1, -jnp.inf, pe)
    i2 = jnp.argmax(rest, axis=-1)
    top_i = jnp.stack([i1, i2], axis=-1)
    top_p = jnp.stack([jnp.max(pe, axis=-1), jnp.max(rest, axis=-1)], axis=-1)
    wts = p_grp[:, None] * top_p / jnp.sum(top_p, axis=-1, keepdims=True)
    eid = grp[:, None] * EXPERTS_PER_GROUP + top_i
    return eid.astype(jnp.int32), wts


MOE_TM = 256


def _moe_ffn_kernel(be_ref, x_ref, w1_ref, w3_ref, w2_ref, wt_ref, o_ref, w1b_ref, w3b_ref, w2b_ref):
    i = pl.program_id(0)

    @pl.when((i == 0) | (be_ref[i] != be_ref[jnp.maximum(i - 1, 0)]))
    def _():
        w1b_ref[...] = w1_ref[0].astype(BF16)
        w3b_ref[...] = w3_ref[0].astype(BF16)
        w2b_ref[...] = w2_ref[0].astype(BF16)

    x = x_ref[...]
    a = _mxu(x, w1b_ref[...])
    b = _mxu(x, w3b_ref[...])
    h = (a * jax.nn.sigmoid(a) * b).astype(BF16)
    o_ref[...] = _mxu(h, w2b_ref[...]) * wt_ref[...]


def moe_ffn(rows, row_wt, blk_e, w1, w3, w2):
    m_pad, d = rows.shape
    n_blk = m_pad // MOE_TM
    f = w1.shape[2]
    return pl.pallas_call(
        _moe_ffn_kernel,
        grid_spec=pltpu.PrefetchScalarGridSpec(
            num_scalar_prefetch=1,
            grid=(n_blk,),
            in_specs=[
                pl.BlockSpec((MOE_TM, d), lambda i, be: (i, 0)),
                pl.BlockSpec((1, d, f), lambda i, be: (be[i], 0, 0)),
                pl.BlockSpec((1, d, f), lambda i, be: (be[i], 0, 0)),
                pl.BlockSpec((1, f, d), lambda i, be: (be[i], 0, 0)),
                pl.BlockSpec((MOE_TM, 1), lambda i, be: (i, 0)),
            ],
            out_specs=pl.BlockSpec((MOE_TM, d), lambda i, be: (i, 0)),
            scratch_shapes=[pltpu.VMEM((d, f), BF16), pltpu.VMEM((d, f), BF16), pltpu.VMEM((f, d), BF16)],
        ),
        out_shape=jax.ShapeDtypeStruct((m_pad, d), F32),
        compiler_params=pltpu.CompilerParams(
            dimension_semantics=("arbitrary",), vmem_limit_bytes=VMEM_LIMIT),
        name="moe_ffn",
    )(blk_e, rows, w1, w3, w2, row_wt)


def hier_moe_apply(x, eid, wts, w_e1, w_e3, w_e2):
    N, D = x.shape
    M = N * TOP_K_FINE
    eid_f = eid.reshape(-1)
    onehot = (eid_f[:, None] == jnp.arange(N_EXPERTS, dtype=jnp.int32)[None, :]).astype(jnp.int32)
    run = jnp.cumsum(onehot, axis=0)
    counts = run[-1]
    rank_in_e = jnp.sum(run * onehot, axis=1) - 1
    padded = (counts + MOE_TM - 1) // MOE_TM * MOE_TM
    p_end = jnp.cumsum(padded)
    p_start = p_end - padded
    pos = (p_start[eid_f] + rank_in_e).astype(jnp.int32)
    n_blk = (M + N_EXPERTS * (MOE_TM - 1) + MOE_TM - 1) // MOE_TM
    m_pad = n_blk * MOE_TM
    src_tok = jnp.zeros((m_pad,), jnp.int32).at[pos].set(jnp.arange(M, dtype=jnp.int32) // TOP_K_FINE)
    row_wt = jnp.zeros((m_pad,), F32).at[pos].set(wts.reshape(-1))
    blk_e = jnp.minimum(jnp.searchsorted(p_end, jnp.arange(n_blk, dtype=jnp.int32) * MOE_TM, side='right'),
                        N_EXPERTS - 1).astype(jnp.int32)
    rows = x[src_tok].astype(BF16)
    out = moe_ffn(rows, row_wt.reshape(m_pad, 1), blk_e, w_e1, w_e3, w_e2)
    pos = pos.reshape(N, TOP_K_FINE)
    return out[pos[:, 0]] + out[pos[:, 1]]


OUT_TM = 256
ROUTER_PAD = 128


def _out_proj_kernel(x_ref, gm_ref, gn_ref, hm_ref, hn_ref, wm_ref, wn_ref, wo_ref, gf_ref, wr_ref, br_ref,
                     x1_ref, h2_ref, lg_ref):
    pm = _mxu(hm_ref[...].astype(BF16), wm_ref[...])
    pn = _mxu(hn_ref[...].astype(BF16), wn_ref[...])
    u = jax.nn.sigmoid(gm_ref[...]) * pm + jax.nn.sigmoid(gn_ref[...]) * pn
    x1 = x_ref[...] + _mxu(u.astype(BF16), wo_ref[...])
    x1_ref[...] = x1
    h2 = x1 * lax.rsqrt(jnp.mean(x1 * x1, axis=-1, keepdims=True) + EPS) * gf_ref[...]
    h2_ref[...] = h2
    lg_ref[...] = _mxu(h2.astype(BF16), wr_ref[...]) + br_ref[...]


def out_proj(x, proj, gate_col_blocks, h_m, h_n, w_proj_m, w_proj_n, w_out, g_ffn, w_group, b_group, w_expert, b_expert):
    n, d = x.shape
    n_r = N_GROUPS + N_EXPERTS
    w_r = jnp.pad(jnp.concatenate([w_group, w_expert], axis=1), ((0, 0), (0, ROUTER_PAD - n_r))).astype(BF16)
    b_r = jnp.pad(jnp.concatenate([b_group, b_expert]), (0, ROUTER_PAD - n_r)).reshape(1, ROUTER_PAD)
    cm, cn = gate_col_blocks
    row = lambda i: (i, 0)
    const = lambda i: (0, 0)
    once = pl.Buffered(1)
    return pl.pallas_call(
        _out_proj_kernel,
        grid=(n // OUT_TM,),
        in_specs=[
            pl.BlockSpec((OUT_TM, d), row),
            pl.BlockSpec((OUT_TM, d), lambda i: (i, cm)),
            pl.BlockSpec((OUT_TM, d), lambda i: (i, cn)),
            pl.BlockSpec((OUT_TM, h_m.shape[1]), row),
            pl.BlockSpec((OUT_TM, h_n.shape[1]), row),
            pl.BlockSpec(w_proj_m.shape, const, pipeline_mode=once),
            pl.BlockSpec(w_proj_n.shape, const, pipeline_mode=once),
            pl.BlockSpec(w_out.shape, const, pipeline_mode=once),
            pl.BlockSpec((1, d), const),
            pl.BlockSpec((d, ROUTER_PAD), const),
            pl.BlockSpec((1, ROUTER_PAD), const),
        ],
        out_specs=[pl.BlockSpec((OUT_TM, d), row), pl.BlockSpec((OUT_TM, d), row),
                   pl.BlockSpec((OUT_TM, ROUTER_PAD), row)],
        out_shape=[jax.ShapeDtypeStruct((n, d), F32), jax.ShapeDtypeStruct((n, d), F32),
                   jax.ShapeDtypeStruct((n, ROUTER_PAD), F32)],
        compiler_params=pltpu.CompilerParams(dimension_semantics=("parallel",), vmem_limit_bytes=VMEM_LIMIT),
        name="out_proj",
    )(x, proj, proj, h_m, h_n, w_proj_m.astype(BF16), w_proj_n.astype(BF16), w_out.astype(BF16),
      g_ffn.reshape(1, d), w_r, b_r)


_SEG = dict(zip(("mq", "mk", "mv", "mi", "mf", "mo", "nq", "nkv", "ng", "gate_m", "gate_n"),
                zip(np.cumsum((0,) + SPLITS[:-1]).tolist(), SPLITS)))
_PROJ_ORDER = ("gate_m", "gate_n", "mv", "mo", "nq", "mq", "mk", "nkv", "mi", "mf", "ng")
_PROJ_OFF = dict(zip(_PROJ_ORDER, np.cumsum([0] + [_SEG[k][1] for k in _PROJ_ORDER[:-1]]).tolist()))
D_IN_PAD = -(-D_IN // 896) * 896


def hybrid_layer(x, conv_prev, C0, n0, m0, paged, win_past, g_mix, w_in, w_conv, b_i, b_f, g_mh, g_q, g_kc, g_ks, g_kw,
                 pe_ck, w_ck1, w_ck2, pe_cv, w_cv1, w_cv2, w_proj_m, w_proj_n, w_out, g_ffn, w_group, b_group,
                 w_expert, b_expert, w_e1, w_e3, w_e2, *, use_pallas_proj):
    B, T, D = x.shape
    WB = win_past.shape[1]
    split_at = np.cumsum(SPLITS)[:-1].tolist()
    if use_pallas_proj:
        w_pad = jnp.concatenate(
            [w_in[:, _SEG[k][0]:_SEG[k][0] + _SEG[k][1]].astype(BF16) for k in _PROJ_ORDER]
            + [jnp.zeros((D, D_IN_PAD - D_IN), BF16)], axis=1)
        proj2d = norm_proj(x.reshape(B * T, D), g_mix, w_pad)
        seg = lambda k: proj2d[:, _PROJ_OFF[k]:_PROJ_OFF[k] + _SEG[k][1]].reshape(B, T, _SEG[k][1])
        mq, mk, mv, mi, mf, mo, nq, nkv, ng = (seg(k) for k in ("mq", "mk", "mv", "mi", "mf", "mo", "nq", "nkv", "ng"))
    else:
        proj = rms_norm(x, g_mix) @ w_in
        mq, mk, mv, mi, mf, mo, nq, nkv, ng, gate_m, gate_n = jnp.split(proj, split_at, axis=-1)
    qk, conv_state = short_conv(jnp.concatenate([mq, mk], axis=-1), conv_prev, w_conv)
    qk = jax.nn.silu(qk)
    q_m = qk[..., :M_QK].reshape(B, T, M_HEADS, M_DQK).astype(F32)
    k_m = qk[..., M_QK:].reshape(B, T, M_HEADS, M_DQK).astype(F32) * (M_DQK ** -0.5)
    v_m = mv.reshape(B, T, M_HEADS, M_DV).astype(F32)
    i_pre = (mi + b_i).astype(F32)
    logf = jax.nn.log_sigmoid((mf + b_f).astype(F32))
    h_m, C1, n1, m1 = mlstm_chunkwise(q_m, k_m, v_m, i_pre, logf, C0.astype(F32), n0.astype(F32), m0.astype(F32))
    h_m = rms_norm(h_m, g_mh).reshape(B, T, M_VW).astype(x.dtype) * jax.nn.sigmoid(mo)
    qn = rms_norm(nq.reshape(B, T, N_HEADS, HEAD_DIM), g_q)
    kv_new = nkv.reshape(B, T, 6, N_KV, HEAD_DIM)
    kv_rows = kv_new[:, :, :4]
    win_all = jnp.concatenate([win_past.astype(x.dtype), kv_new[:, :, 4:]], axis=1)
    if paged is None:
        h_n = nsa_prompt(nq, ng, kv_new, g_q, g_kc, g_ks, g_kw, pe_ck, w_ck1, w_ck2, pe_cv, w_cv1, w_cv2)
    else:
        gates_n = jax.nn.sigmoid(ng).reshape(B, T, N_HEADS, 3)
        h_n = nsa_sample(qn, gates_n, paged[0], paged[1], kv_rows, win_all, g_kc, g_ks, g_kw,
                         pe_ck, w_ck1, w_ck2, pe_cv, w_cv1, w_cv2)
    win_state = win_all[:, -min(WINDOW, WB + T):]
    if use_pallas_proj:
        assert _PROJ_OFF["gate_m"] == 0 and _PROJ_OFF["gate_n"] == D
        x1, h2, logits = out_proj(x.reshape(B * T, D), proj2d, (0, 1), h_m.reshape(B * T, M_VW),
                                  h_n.reshape(B * T, N_QW), w_proj_m, w_proj_n, w_out, g_ffn,
                                  w_group, b_group, w_expert, b_expert)
        x1 = x1.reshape(B, T, D)
        lg, le = logits[:, :N_GROUPS], logits[:, N_GROUPS:N_GROUPS + N_EXPERTS]
    else:
        u = jax.nn.sigmoid(gate_m) * (h_m @ w_proj_m) + jax.nn.sigmoid(gate_n) * (h_n.astype(x.dtype) @ w_proj_n)
        x1 = x + u @ w_out
        h2 = rms_norm(x1, g_ffn).reshape(B * T, D)
        lg, le = h2 @ w_group + b_group, h2 @ w_expert + b_expert
    eid, wts = moe_route(lg, le)
    return x1, h2, eid, wts, kv_rows, win_state, C1, n1, m1, conv_state


def kernel(x_prompt, x_sample, cache_kv, page_table, state_win, state_C, state_n, state_m, state_conv, g_mix, w_in, w_conv, b_i, b_f, g_mh, g_q, g_kc, g_ks, g_kw, pe_ck, w_ck1, w_ck2, pe_cv, w_cv1, w_cv2, w_proj_m, w_proj_n, w_out, g_ffn, w_group, b_group, w_expert, b_expert, w_e1, w_e3, w_e2):
    B = x_prompt.shape[0]
    dt = x_prompt.dtype
    weights = (g_mix, w_in, w_conv, b_i, b_f, g_mh, g_q, g_kc, g_ks, g_kw, pe_ck, w_ck1, w_ck2, pe_cv, w_cv1, w_cv2,
               w_proj_m, w_proj_n, w_out, g_ffn, w_group, b_group, w_expert, b_expert, w_e1, w_e3, w_e2)
    x1_p, h2_p, eid_p, wts_p, kv_p, win_p, C_p, n_p, m_p, conv_p = hybrid_layer(
        x_prompt, jnp.zeros((B, M_CONV - 1, 2 * M_QK), dt), jnp.zeros((B, M_HEADS, M_DQK, M_DV), F32),
        jnp.zeros((B, M_HEADS, M_DQK), F32), jnp.zeros((B, M_HEADS), F32),
        None, jnp.zeros((B, 0, 2, N_KV, HEAD_DIM), dt), *weights, use_pallas_proj=True)
    with jax.default_matmul_precision("highest"):
        x1_s, h2_s, eid_s, wts_s, kv_s, win_s, C_s, n_s, m_s, conv_s = hybrid_layer(
            x_sample, state_conv, state_C, state_n, state_m, (cache_kv, page_table), state_win, *weights,
            use_pallas_proj=False)
    n_p_tok = h2_p.shape[0]
    moe = hier_moe_apply(jnp.concatenate([h2_p, h2_s], axis=0), jnp.concatenate([eid_p, eid_s], axis=0),
                         jnp.concatenate([wts_p, wts_s], axis=0), w_e1, w_e3, w_e2)
    y_p = x1_p + moe[:n_p_tok].reshape(x1_p.shape)
    y_s = x1_s + moe[n_p_tok:].reshape(x1_s.shape)
    return (y_p, y_s, kv_p, kv_s, win_p, win_s, C_p, C_s, n_p, n_s, m_p, m_s, conv_p, conv_s)
```

```python
import functools
import math

import jax
import jax.numpy as jnp
import numpy as np
from jax import lax
from jax.experimental import pallas as pl
from jax.experimental.pallas import tpu as pltpu

D_MODEL = 2048
M_HEADS = 4
M_DQK = 128
M_DV = 256
M_CONV = 4
M_CHUNK = 64
MLSTM_SCAN_CHUNK = 256
N_HEADS = 16
N_KV = 4
HEAD_DIM = 64
CMP_BLK = 32
SEL_BLK = 64
N_SELECT = 16
WINDOW = 512
NSA_QBLK = 64
FORCE_SCORE = 1e4
N_GROUPS = 4
EXPERTS_PER_GROUP = 8
N_EXPERTS = N_GROUPS * EXPERTS_PER_GROUP
TOP_K_FINE = 2
MOE_BLK = 128
EPS = 1e-6
M_QK = M_HEADS * M_DQK
M_VW = M_HEADS * M_DV
N_QW = N_HEADS * HEAD_DIM
N_KVW = 6 * N_KV * HEAD_DIM
SPLITS = (M_QK, M_QK, M_VW, M_HEADS, M_HEADS, M_VW, N_QW, N_KVW, 3 * N_HEADS, D_MODEL, D_MODEL)
D_IN = sum(SPLITS)

F32 = jnp.float32
BF16 = jnp.bfloat16
VMEM_LIMIT = 48 * 1024 * 1024


def _mxu(a, b):
    return jnp.dot(a, b, preferred_element_type=F32, precision=lax.Precision.DEFAULT)


def _norm_proj_kernel(x_ref, g_ref, w_ref, o_ref, hx_ref):
    @pl.when(pl.program_id(1) == 0)
    def _():
        x = x_ref[...]
        r = lax.rsqrt(jnp.mean(x * x, axis=-1, keepdims=True) + EPS)
        hx_ref[...] = (x * r * g_ref[...]).astype(BF16)

    o_ref[...] = _mxu(hx_ref[...], w_ref[...])


def norm_proj(x, g, w_bf16, *, tm=1024, tn=896):
    n, d = x.shape
    k = w_bf16.shape[1]
    return pl.pallas_call(
        _norm_proj_kernel,
        grid=(n // tm, k // tn),
        in_specs=[
            pl.BlockSpec((tm, d), lambda i, j: (i, 0)),
            pl.BlockSpec((1, d), lambda i, j: (0, 0)),
            pl.BlockSpec((d, tn), lambda i, j: (0, j)),
        ],
        out_specs=pl.BlockSpec((tm, tn), lambda i, j: (i, j)),
        out_shape=jax.ShapeDtypeStruct((n, k), F32),
        scratch_shapes=[pltpu.VMEM((tm, d), BF16)],
        compiler_params=pltpu.CompilerParams(
            dimension_semantics=("parallel", "arbitrary"), vmem_limit_bytes=VMEM_LIMIT),
        name="norm_proj",
    )(x, g.reshape(1, d), w_bf16)


def _rms_rows_kernel(x_ref, g_ref, o_ref):
    x = x_ref[...]
    r = lax.rsqrt(jnp.mean(x * x, axis=-1, keepdims=True) + EPS)
    o_ref[...] = (x * r * g_ref[...]).astype(o_ref.dtype)


def rms_rows(x, g, out_dtype, *, tr=2048):
    rows, c = x.shape
    tr = min(tr, rows)
    return pl.pallas_call(
        _rms_rows_kernel,
        grid=(rows // tr,),
        in_specs=[pl.BlockSpec((tr, c), lambda i: (i, 0)), pl.BlockSpec((1, c), lambda i: (0, 0))],
        out_specs=pl.BlockSpec((tr, c), lambda i: (i, 0)),
        out_shape=jax.ShapeDtypeStruct((rows, c), out_dtype),
        compiler_params=pltpu.CompilerParams(dimension_semantics=("parallel",)),
        name="rms_rows",
    )(x, g.reshape(1, c))


def _split_bf16(x):
    hi = lax.bitcast_convert_type(
        lax.bitcast_convert_type(x, jnp.int32) & jnp.int32(-65536), F32)
    return hi.astype(BF16), (x - hi).astype(BF16)


def _dot_split(a, b_hi, b_lo):
    a_hi, a_lo = _split_bf16(a)
    return _mxu(a_hi, b_hi) + (_mxu(a_lo, b_hi) + _mxu(a_hi, b_lo))


def _compress_kernel(x_ref, pe_ref, w1_ref, w1l_ref, w2_ref, w2l_ref, g_ref, o_ref, *, normalize, precise):
    x = x_ref[...] + pe_ref[...]
    if precise:
        h = _dot_split(x, w1_ref[...], w1l_ref[...])
    else:
        h = _mxu(x.astype(BF16), w1_ref[...])
    h = h * jax.nn.sigmoid(h)
    if precise:
        y = _dot_split(h, w2_ref[...], w2l_ref[...])
    else:
        y = _mxu(h.astype(BF16), w2_ref[...])
    if normalize:
        y = y * lax.rsqrt(jnp.mean(y * y, axis=-1, keepdims=True) + EPS) * g_ref[...]
    o_ref[...] = y.astype(o_ref.dtype)


def compress_rows(x, pe, w1, w2, g, *, normalize, precise=False, tr=512):
    rows, k = x.shape
    f = w1.shape[-1]
    d = w2.shape[1]
    tr = min(tr, rows)
    w1_hi, w1_lo = _split_bf16(w1.reshape(k, f))
    w2_hi, w2_lo = _split_bf16(w2)
    if not precise:
        w1_hi, w2_hi = w1.reshape(k, f).astype(BF16), w2.astype(BF16)
    return pl.pallas_call(
        functools.partial(_compress_kernel, normalize=normalize, precise=precise),
        grid=(rows // tr,),
        in_specs=[
            pl.BlockSpec((tr, k), lambda i: (i, 0)),
            pl.BlockSpec((1, k), lambda i: (0, 0)),
            pl.BlockSpec((k, f), lambda i: (0, 0)),
            pl.BlockSpec((k, f), lambda i: (0, 0)),
            pl.BlockSpec((f, d), lambda i: (0, 0)),
            pl.BlockSpec((f, d), lambda i: (0, 0)),
            pl.BlockSpec((1, d), lambda i: (0, 0)),
        ],
        out_specs=pl.BlockSpec((tr, d), lambda i: (i, 0)),
        out_shape=jax.ShapeDtypeStruct((rows, d), F32 if precise else BF16),
        compiler_params=pltpu.CompilerParams(dimension_semantics=("parallel",)),
        name="nsa_compress",
    )(x, pe.reshape(1, k), w1_hi, w1_lo, w2_hi, w2_lo, g.reshape(1, d))


CMP_TILE_ROWS = 128


def _compress_paged_kernel(x_ref, pe_ref, w_hi_ref, w_lo_ref, w2_hi_ref, w2_lo_ref, g_ref, o_ref, *,
                           normalize, cpp, hidden):
    rows = x_ref.shape[0]
    hd = x_ref.shape[1]
    acc = jnp.zeros((rows, cpp * hidden), F32)
    for dp in range(hd // 2):
        a = x_ref[:, 2 * dp, :] + pe_ref[pl.ds(2 * dp, 1), :]
        b = x_ref[:, 2 * dp + 1, :] + pe_ref[pl.ds(2 * dp + 1, 1), :]
        acc = acc + _dot_split(jnp.concatenate([a, b], axis=1), w_hi_ref[dp], w_lo_ref[dp])
    hid = acc * jax.nn.sigmoid(acc)
    for c in range(cpp):
        y = _dot_split(hid[:, c * hidden:(c + 1) * hidden], w2_hi_ref[...], w2_lo_ref[...])
        if normalize:
            y = y * lax.rsqrt(jnp.mean(y * y, axis=-1, keepdims=True) + EPS) * g_ref[...]
        o_ref[:, c * hd:(c + 1) * hd] = y


def compress_paged(x_tiles, pe, w1, w2, g, *, normalize):
    r, hd, page = x_tiles.shape
    cpp = page // CMP_BLK
    hidden = w1.shape[-1]
    eye = jnp.eye(cpp, dtype=F32)
    w_big = jnp.einsum('cC,ldf->dclCf', eye, w1).reshape(hd // 2, 2 * page, cpp * hidden)
    w_hi, w_lo = _split_bf16(w_big)
    w2_hi, w2_lo = _split_bf16(w2)
    pe_t = jnp.tile(pe.T, (1, cpp))
    const3 = lambda i: (0, 0, 0)
    const2 = lambda i: (0, 0)
    return pl.pallas_call(
        functools.partial(_compress_paged_kernel, normalize=normalize, cpp=cpp, hidden=hidden),
        grid=(r // CMP_TILE_ROWS,),
        in_specs=[
            pl.BlockSpec((CMP_TILE_ROWS, hd, page), lambda i: (i, 0, 0)),
            pl.BlockSpec((hd, page), const2),
            pl.BlockSpec((hd // 2, 2 * page, cpp * hidden), const3, pipeline_mode=pl.Buffered(1)),
            pl.BlockSpec((hd // 2, 2 * page, cpp * hidden), const3, pipeline_mode=pl.Buffered(1)),
            pl.BlockSpec((hidden, hd), const2),
            pl.BlockSpec((hidden, hd), const2),
            pl.BlockSpec((1, hd), const2),
        ],
        out_specs=pl.BlockSpec((CMP_TILE_ROWS, cpp * hd), lambda i: (i, 0)),
        out_shape=jax.ShapeDtypeStruct((r, cpp * hd), F32),
        compiler_params=pltpu.CompilerParams(dimension_semantics=("parallel",), vmem_limit_bytes=VMEM_LIMIT),
        name="nsa_compress_paged",
    )(x_tiles, pe_t, w_hi, w_lo, w2_hi, w2_lo, g.reshape(1, hd))


NSA_TQ = 128
NSA_TK = 512
HPG = N_HEADS // N_KV
NSA_Q = HPG * NSA_TQ
NEG_BIG = -1e30


def _nsa_prompt_kernel(qT_ref, gq_ref, gate_ref, kc_ref, vcT_ref, ks_ref, vsT_ref, kw_ref, vwT_ref,
                       o_ref, sc_ref, sel_ref, imp_ref, *, n_cmp, n_sel):
    j = pl.program_id(2)
    s0 = j * NSA_TQ
    q = qT_ref[0, 0, 0]
    r = lax.rsqrt(jnp.mean(q * q, axis=0, keepdims=True) + EPS)
    qn = (q * r * gq_ref[...] * (HEAD_DIM ** -0.5 * math.log2(math.e))).astype(BF16)
    lane = lax.broadcasted_iota(jnp.int32, (1, NSA_Q), 1)
    t_row = s0 + (lane & (NSA_TQ - 1))

    s = _mxu(kc_ref[0, 0], qn)
    c_end = lax.broadcasted_iota(jnp.int32, (n_cmp, NSA_Q), 0) * CMP_BLK + (CMP_BLK - 1)
    mask_c = c_end <= t_row
    s = jnp.where(mask_c, s, -jnp.inf)
    m = jnp.max(s, axis=0, keepdims=True)
    m = jnp.where(m > -jnp.inf, m, 0.0)
    e = jnp.where(mask_c, jnp.exp2(s - m), 0.0)
    p_c = e / jnp.maximum(jnp.sum(e, axis=0, keepdims=True), 1e-30)
    o_c = _mxu(vcT_ref[0, 0], p_c.astype(BF16))

    imp_ref[...] = (p_c[:, 0:NSA_TQ] + p_c[:, NSA_TQ:2 * NSA_TQ]
                    + p_c[:, 2 * NSA_TQ:3 * NSA_TQ] + p_c[:, 3 * NSA_TQ:4 * NSA_TQ])
    imp = imp_ref[pl.ds(0, n_sel, stride=2), :] + imp_ref[pl.ds(1, n_sel, stride=2), :]
    n_iota = lax.broadcasted_iota(jnp.int32, (n_sel, NSA_TQ), 0)
    cur = (s0 + lax.broadcasted_iota(jnp.int32, (n_sel, NSA_TQ), 1)) // SEL_BLK
    valid = n_iota <= cur
    forced = (n_iota == 0) | (n_iota == cur) | (n_iota == cur - 1)
    score = jnp.where(valid, jnp.where(forced, FORCE_SCORE, imp), -jnp.inf)
    sc_ref[...] = score

    def rank_body(n2, rank):
        row = jnp.broadcast_to(sc_ref[pl.ds(n2, 1), :], (n_sel, NSA_TQ))
        beats = (row > score) | ((row == score) & (n2 < n_iota))
        return rank + jnp.where(beats, 1.0, 0.0)

    n_live = jnp.minimum((s0 + NSA_TQ - 1) // SEL_BLK + 1, n_sel)
    rank = lax.fori_loop(0, n_live, rank_body, jnp.zeros((n_sel, NSA_TQ), F32))
    sel_ref[...] = jnp.where(valid & (rank < N_SELECT), 0.0, NEG_BIG)

    key_iota = lax.broadcasted_iota(jnp.int32, (NSA_TK, NSA_TQ), 0)
    t_q = s0 + lax.broadcasted_iota(jnp.int32, (1, NSA_TQ), 1)

    def attend(k_ref, vT_ref, c_lo, c_hi, bias_fn):
        def body(c, carry):
            m_i, l_i, acc = carry
            k = k_ref[0, 0, pl.ds(pl.multiple_of(c * NSA_TK, NSA_TK), NSA_TK), :]
            sk = _mxu(k, qn)
            bias = bias_fn(c)
            ms, ls, ps = [], [], []
            for h in range(HPG):
                sl = slice(h * NSA_TQ, (h + 1) * NSA_TQ)
                s_h = sk[:, sl] + bias
                m_h = jnp.maximum(m_i[:, sl], jnp.max(s_h, axis=0, keepdims=True))
                p_h = jnp.exp2(s_h - m_h)
                ms.append(m_h)
                ls.append(jnp.sum(p_h, axis=0, keepdims=True))
                ps.append(p_h.astype(BF16))
            m_new = jnp.concatenate(ms, axis=1)
            alpha = jnp.exp2(m_i - m_new)
            l_new = alpha * l_i + jnp.concatenate(ls, axis=1)
            acc = alpha * acc + _mxu(vT_ref[0, 0, c], jnp.concatenate(ps, axis=1))
            return m_new, l_new, acc

        init = (jnp.full((1, NSA_Q), NEG_BIG, F32), jnp.zeros((1, NSA_Q), F32),
                jnp.zeros((HEAD_DIM, NSA_Q), F32))
        _, l_f, acc_f = lax.fori_loop(c_lo, c_hi, body, init)
        return acc_f / jnp.maximum(l_f, 1e-30)

    def sel_bias(c):
        pos = c * NSA_TK + key_iota
        per_blk = [jnp.broadcast_to(
            sel_ref[pl.ds(jnp.minimum(c * (NSA_TK // SEL_BLK) + i, n_sel - 1), 1), :], (SEL_BLK, NSA_TQ))
            for i in range(NSA_TK // SEL_BLK)]
        return jnp.where(pos <= t_q, jnp.concatenate(per_blk, axis=0), NEG_BIG)

    def win_bias(c):
        pos = c * NSA_TK + key_iota
        return jnp.where((pos <= t_q) & (pos > t_q - WINDOW), 0.0, NEG_BIG)

    c_hi = (s0 + NSA_TQ + NSA_TK - 1) // NSA_TK
    o_s = attend(ks_ref, vsT_ref, 0, c_hi, sel_bias)
    o_w = attend(kw_ref, vwT_ref, jnp.maximum(s0 - (WINDOW - 1), 0) // NSA_TK, c_hi, win_bias)

    g = jax.nn.sigmoid(gate_ref[0, 0, 0])
    o_ref[0, 0, 0] = g[0:1] * o_c + g[1:2] * o_s + g[2:3] * o_w


def nsa_prompt(nq, ng, kv_new, g_q, g_kc, g_ks, g_kw, pe_ck, w_ck1, w_ck2, pe_cv, w_cv1, w_cv2):
    B, T = nq.shape[:2]
    G = N_KV
    nqb = T // NSA_TQ
    n_cmp = T // CMP_BLK
    n_sel = T // SEL_BLK
    nch = T // NSA_TK
    qT = nq.reshape(B, nqb, NSA_TQ, G, HPG, HEAD_DIM).transpose(0, 3, 1, 5, 4, 2).reshape(B, G, nqb, HEAD_DIM, NSA_Q)
    gT = ng.reshape(B, nqb, NSA_TQ, G, HPG, 3).transpose(0, 3, 1, 5, 4, 2).reshape(B, G, nqb, 3, NSA_Q)
    kv_g = kv_new.transpose(2, 0, 3, 1, 4)
    xk = kv_g[0].reshape(B * G * n_cmp, CMP_BLK * HEAD_DIM)
    xv = kv_g[1].reshape(B * G * n_cmp, CMP_BLK * HEAD_DIM)
    kc = compress_rows(xk, pe_ck, w_ck1, w_ck2, g_kc, normalize=True).reshape(B, G, n_cmp, HEAD_DIM)
    vc = compress_rows(xv, pe_cv, w_cv1, w_cv2, g_kc, normalize=False).reshape(B, G, n_cmp, HEAD_DIM)
    vcT = vc.transpose(0, 1, 3, 2)
    ks = rms_rows(kv_g[2].reshape(B * G * T, HEAD_DIM), g_ks, BF16).reshape(B, G, T, HEAD_DIM)
    kw = rms_rows(kv_g[4].reshape(B * G * T, HEAD_DIM), g_kw, BF16).reshape(B, G, T, HEAD_DIM)
    vsT = kv_g[3].astype(BF16).reshape(B, G, nch, NSA_TK, HEAD_DIM).transpose(0, 1, 2, 4, 3)
    vwT = kv_g[5].astype(BF16).reshape(B, G, nch, NSA_TK, HEAD_DIM).transpose(0, 1, 2, 4, 3)

    bg = lambda b, g, j: (b, g, 0, 0)
    bg5 = lambda b, g, j: (b, g, 0, 0, 0)
    bgj = lambda b, g, j: (b, g, j, 0, 0)
    oT = pl.pallas_call(
        functools.partial(_nsa_prompt_kernel, n_cmp=n_cmp, n_sel=n_sel),
        grid=(B, G, nqb),
        in_specs=[
            pl.BlockSpec((1, 1, 1, HEAD_DIM, NSA_Q), bgj),
            pl.BlockSpec((HEAD_DIM, 1), lambda b, g, j: (0, 0)),
            pl.BlockSpec((1, 1, 1, 3, NSA_Q), bgj),
            pl.BlockSpec((1, 1, n_cmp, HEAD_DIM), bg),
            pl.BlockSpec((1, 1, HEAD_DIM, n_cmp), bg),
            pl.BlockSpec((1, 1, T, HEAD_DIM), bg),
            pl.BlockSpec((1, 1, nch, HEAD_DIM, NSA_TK), bg5),
            pl.BlockSpec((1, 1, T, HEAD_DIM), bg),
            pl.BlockSpec((1, 1, nch, HEAD_DIM, NSA_TK), bg5),
        ],
        out_specs=pl.BlockSpec((1, 1, 1, HEAD_DIM, NSA_Q), bgj),
        out_shape=jax.ShapeDtypeStruct((B, G, nqb, HEAD_DIM, NSA_Q), F32),
        scratch_shapes=[pltpu.VMEM((n_sel, NSA_TQ), F32), pltpu.VMEM((n_sel, NSA_TQ), F32),
                        pltpu.VMEM((n_cmp, NSA_TQ), F32)],
        compiler_params=pltpu.CompilerParams(
            dimension_semantics=("parallel", "parallel", "arbitrary"), vmem_limit_bytes=VMEM_LIMIT),
        name="nsa_prompt",
    )(qT, g_q.reshape(HEAD_DIM, 1), gT, kc, vcT, ks, vsT, kw, vwT)
    return oT.reshape(B, G, nqb, HEAD_DIM, HPG, NSA_TQ).transpose(0, 2, 5, 1, 4, 3).reshape(B, T, N_QW)


def rms_norm(x, g):
    xf = x.astype(F32)
    y = xf * lax.rsqrt(jnp.mean(xf * xf, axis=-1, keepdims=True) + EPS)
    return (y * g.astype(F32)).astype(x.dtype)


def masked_softmax(s, mask):
    s = jnp.where(mask, s.astype(F32), -jnp.inf)
    m = jnp.max(s, axis=-1, keepdims=True)
    m = jnp.where(jnp.isfinite(m), m, 0.0)
    e = jnp.where(mask, jnp.exp(s - m), 0.0)
    return e / jnp.maximum(jnp.sum(e, axis=-1, keepdims=True), 1e-30)


def short_conv(u, prev, w):
    T = u.shape[1]
    full = jnp.concatenate([prev.astype(u.dtype), u], axis=1)
    out = full[:, 0:T] * w[0]
    for j in range(1, M_CONV):
        out = out + full[:, j:j + T] * w[j]
    return out, full[:, T:]


def mlstm_chunkwise(q, k, v, i_pre, logf, C0, n0, m0):
    B, T = q.shape[:2]
    L = MLSTM_SCAN_CHUNK if T % MLSTM_SCAN_CHUNK == 0 else (M_CHUNK if T % M_CHUNK == 0 else T)
    nc = T // L

    def chunks(a):
        a = a.reshape((B, nc, L) + a.shape[2:])
        return jnp.moveaxis(jnp.moveaxis(a, 3, 2), 1, 0)

    causal = jnp.tril(jnp.ones((L, L), bool))

    def step(carry, xs):
        C, n, m = carry
        qc, kc, vc, ic, fc = xs
        b = jnp.einsum('bhs,ts->bht', fc, causal.astype(F32), precision=lax.Precision.HIGHEST)
        logD = jnp.where(causal, b[..., :, None] - b[..., None, :] + ic[..., None, :], -jnp.inf)
        inter = b + m[..., None]
        m_t = jnp.maximum(inter, jnp.max(logD, axis=-1))
        a = jnp.exp(inter - m_t)
        S = jnp.einsum('bhtd,bhsd->bhts', qc, kc) * jnp.exp(logD - m_t[..., None])
        num = a[..., None] * jnp.einsum('bhtd,bhde->bhte', qc, C) + jnp.einsum('bhts,bhse->bhte', S, vc)
        den = a * jnp.einsum('bhtd,bhd->bht', qc, n) + jnp.sum(S, axis=-1)
        h = num / jnp.maximum(jnp.abs(den), jnp.exp(-m_t))[..., None]
        m_new = m_t[..., -1]
        w = jnp.exp(b[..., -1:] - b + ic - m_new[..., None])
        aL = jnp.exp(b[..., -1] + m - m_new)
        C_new = aL[..., None, None] * C + jnp.einsum('bhs,bhsd,bhse->bhde', w, kc, vc)
        n_new = aL[..., None] * n + jnp.einsum('bhs,bhsd->bhd', w, kc)
        return (C_new, n_new, m_new), h

    (C, n, m), h = lax.scan(step, (C0, n0, m0), (chunks(q), chunks(k), chunks(v), chunks(i_pre), chunks(logf)))
    h = jnp.moveaxis(jnp.moveaxis(h, 0, 1), 2, 3).reshape(B, T, M_HEADS, M_DV)
    return h, C, n, m


def compress(rows, pe, w1, w2):
    B, Lk = rows.shape[:2]
    nc = Lk // CMP_BLK
    blk = rows[:, :nc * CMP_BLK].reshape(B, nc, CMP_BLK, N_KV, HEAD_DIM) + pe[:, None, :]
    hid = jax.nn.silu(jnp.einsum('bclgd,ldf->bcgf', blk, w1))
    return jnp.einsum('bcgf,fd->bcgd', hid, w2)


def nsa_attention(q, gates, kv_all, kw_full, q0, g_kc, g_ks, g_kw, pe_ck, w_ck1, w_ck2, pe_cv, w_cv1, w_cv2):
    B, T = q.shape[:2]
    Lk = kv_all.shape[1]
    hpg = N_HEADS // N_KV
    scale = HEAD_DIM ** -0.5
    kc = rms_norm(compress(kv_all[:, :, 0], pe_ck, w_ck1, w_ck2), g_kc)
    vc = compress(kv_all[:, :, 1], pe_cv, w_cv1, w_cv2)
    nc = kc.shape[1]
    n_sel = -(-Lk // SEL_BLK)
    pad = n_sel * SEL_BLK - Lk
    ks = jnp.pad(rms_norm(kv_all[:, :, 2], g_ks), ((0, 0), (0, pad), (0, 0), (0, 0)))
    ks = ks.reshape(B, n_sel, SEL_BLK, N_KV, HEAD_DIM).transpose(0, 3, 1, 2, 4)
    vs = jnp.pad(kv_all[:, :, 3], ((0, 0), (0, pad), (0, 0), (0, 0)))
    vs = vs.reshape(B, n_sel, SEL_BLK, N_KV, HEAD_DIM).transpose(0, 3, 1, 2, 4)
    kw = rms_norm(kw_full[:, :, 0], g_kw)
    vw = kw_full[:, :, 1]
    top = min(N_SELECT, n_sel)
    ratio = SEL_BLK // CMP_BLK
    qb = NSA_QBLK if T % NSA_QBLK == 0 else T
    nqb = T // qb
    qg = q.reshape(B, nqb, qb, N_KV, hpg, HEAD_DIM).transpose(1, 0, 3, 4, 2, 5)
    gg = jnp.moveaxis(gates.reshape(B, nqb, qb, N_HEADS, 3), 1, 0)
    cmp_end = (jnp.arange(nc) + 1) * CMP_BLK - 1
    blk_ids = jnp.arange(n_sel)
    b_ix = jnp.arange(B)[:, None, None, None]
    g_ix = jnp.arange(N_KV)[None, :, None, None]

    def block(args):
        qi, gi, j = args
        s0 = j * qb
        t = q0 + s0 + jnp.arange(qb)
        sc = jnp.einsum('bghqd,bcgd->bghqc', qi, kc) * scale
        p_c = masked_softmax(sc, cmp_end[None, :] <= t[:, None])
        o_c = jnp.einsum('bghqc,bcgd->bghqd', p_c.astype(vc.dtype), vc)
        imp = jnp.sum(p_c, axis=2)
        imp = jnp.pad(imp, ((0, 0), (0, 0), (0, 0), (0, n_sel * ratio - nc)))
        imp = imp.reshape(B, N_KV, qb, n_sel, ratio).sum(-1)
        cur = t // SEL_BLK
        valid = blk_ids[None, :] <= cur[:, None]
        forced = (blk_ids[None, :] == 0) | (blk_ids[None, :] == cur[:, None]) | (blk_ids[None, :] == cur[:, None] - 1)
        imp = jnp.where(valid, jnp.where(forced, FORCE_SCORE, imp), -jnp.inf)
        top_v, top_i = lax.top_k(imp, top)
        k_sel = ks[b_ix, g_ix, top_i]
        v_sel = vs[b_ix, g_ix, top_i]
        pos = top_i[..., None] * SEL_BLK + jnp.arange(SEL_BLK)
        m_s = (pos <= t[:, None, None]) & jnp.isfinite(top_v)[..., None]
        ss = (jnp.einsum('bghqd,bgqnkd->bghqnk', qi, k_sel) * scale).reshape(B, N_KV, hpg, qb, top * SEL_BLK)
        p_s = masked_softmax(ss, m_s.reshape(B, N_KV, 1, qb, top * SEL_BLK))
        o_s = jnp.einsum('bghqnk,bgqnkd->bghqd', p_s.reshape(B, N_KV, hpg, qb, top, SEL_BLK).astype(v_sel.dtype), v_sel)
        kwi = lax.dynamic_slice_in_dim(kw, s0, qb + WINDOW, axis=1)
        vwi = lax.dynamic_slice_in_dim(vw, s0, qb + WINDOW, axis=1)
        pw = q0 - WINDOW + s0 + jnp.arange(qb + WINDOW)
        m_w = (pw[None, :] <= t[:, None]) & (pw[None, :] > t[:, None] - WINDOW) & (pw[None, :] >= 0)
        sw = jnp.einsum('bghqd,bkgd->bghqk', qi, kwi) * scale
        p_w = masked_softmax(sw, m_w)
        o_w = jnp.einsum('bghqk,bkgd->bghqd', p_w.astype(vwi.dtype), vwi)
        o = jnp.stack([o_c, o_s, o_w], axis=-1)
        gi_r = gi.reshape(B, qb, N_KV, hpg, 3).transpose(0, 2, 3, 1, 4)
        return jnp.einsum('bghqdc,bghqc->bqghd', o, gi_r.astype(o.dtype))

    out = lax.map(block, (qg, gg, jnp.arange(nqb)))
    return jnp.moveaxis(out, 0, 1).reshape(B, T, N_QW)


def nsa_sample(q, gates, cache_kv, page_table, kv_rows, win_all, g_kc, g_ks, g_kw,
               pe_ck, w_ck1, w_ck2, pe_cv, w_cv1, w_cv2):
    B, T = q.shape[:2]
    n_pages = page_table.shape[1]
    page = cache_kv.shape[1]
    P = n_pages * page
    G, hd = N_KV, HEAD_DIM
    assert T < CMP_BLK and T <= SEL_BLK and P % SEL_BLK == 0 and page % SEL_BLK == 0
    assert win_all.shape[1] == WINDOW + T
    nc = P // CMP_BLK
    cpp = page // CMP_BLK
    n_last = P // SEL_BLK
    n_sel = n_last + 1
    top = min(N_SELECT, n_sel)
    ratio = SEL_BLK // CMP_BLK
    scale = hd ** -0.5
    t = P + jnp.arange(T)
    qg = q.reshape(B, T, G, HPG, hd)

    n_kind = cache_kv.shape[2]
    cache_r = cache_kv.transpose(0, 2, 3, 4, 1).reshape(cache_kv.shape[0] * n_kind * G, hd, page)
    tile_row = lambda p, kind, g: (p * n_kind + kind) * G + g

    def cmp_tiles(kind):
        rows = tile_row(page_table[:, :, None], kind, jnp.arange(G)[None, None, :])
        return cache_r[rows.reshape(-1)]

    def cmp_rows(y):
        return y.reshape(B, n_pages, G, cpp, hd).transpose(0, 2, 1, 3, 4).reshape(B, G, nc, hd)

    kc = cmp_rows(compress_paged(cmp_tiles(0), pe_ck, w_ck1, w_ck2, g_kc, normalize=True))
    vc = cmp_rows(compress_paged(cmp_tiles(1), pe_cv, w_cv1, w_cv2, g_kc, normalize=False))
    cmp_end = (jnp.arange(nc) + 1) * CMP_BLK - 1
    sc = jnp.einsum('btghd,bgcd->bghtc', qg, kc) * scale
    p_c = masked_softmax(sc, cmp_end[None, :] <= t[:, None])
    o_c = jnp.einsum('bghtc,bgcd->bghtd', p_c, vc)

    imp = jnp.sum(p_c, axis=2)
    imp = jnp.pad(imp, ((0, 0), (0, 0), (0, 0), (0, n_sel * ratio - nc)))
    imp = imp.reshape(B, G, T, n_sel, ratio).sum(-1)
    blk_ids = jnp.arange(n_sel)
    cur = t // SEL_BLK
    valid = blk_ids[None, :] <= cur[:, None]
    forced = (blk_ids[None, :] == 0) | (blk_ids[None, :] == cur[:, None]) | (blk_ids[None, :] == cur[:, None] - 1)
    imp = jnp.where(valid, jnp.where(forced, FORCE_SCORE, imp), -jnp.inf)
    ids = jnp.arange(n_sel)
    rest, top_v, top_i = imp, [], []
    for _ in range(top):
        i_max = jnp.argmax(rest, axis=-1)
        top_i.append(i_max)
        top_v.append(jnp.max(rest, axis=-1))
        rest = jnp.where(ids == i_max[..., None], -jnp.inf, rest)
    top_v, top_i = jnp.stack(top_v, axis=-1), jnp.stack(top_i, axis=-1)

    bpp = page // SEL_BLK
    blk_c = jnp.minimum(top_i, n_last - 1)
    b_ix = jnp.arange(B)[:, None, None, None]
    phys = page_table[b_ix, blk_c // bpp]
    g_ix = jnp.arange(G)[None, :, None, None]
    tiles = jnp.stack([cache_r[tile_row(phys, 2, g_ix)], cache_r[tile_row(phys, 3, g_ix)]], axis=4)
    rms_d = lambda k: k * lax.rsqrt(jnp.mean(k * k, axis=-2, keepdims=True) + EPS) * g_ks[:, None]
    k_t = rms_d(tiles[..., 0, :, :])
    v_t = tiles[..., 1, :, :]
    new_tile = jnp.pad(kv_rows[:, :, 2:4], ((0, 0), (0, page - T), (0, 0), (0, 0), (0, 0)))
    new_tile = new_tile.transpose(0, 3, 2, 4, 1)
    k_n, v_n = rms_d(new_tile[:, :, 0]), new_tile[:, :, 1]
    is_new = top_i == n_last
    picked = jnp.isfinite(top_v)
    lane = jnp.arange(page)
    pos = (blk_c // bpp * bpp)[..., None] * SEL_BLK + lane
    m_s = ((pos // SEL_BLK == top_i[..., None]) & (pos <= t[:, None, None])
           & (picked & ~is_new)[..., None])
    pos_n = n_last * SEL_BLK + lane
    m_n = (jnp.any(is_new & picked, axis=-1)[..., None] & (pos_n // SEL_BLK == n_last)
           & (pos_n[None, :] <= t[:, None]))
    ss = (jnp.einsum('btghd,bgtndp->bghtnp', qg, k_t) * scale).reshape(B, G, HPG, T, top * page)
    ss_n = jnp.einsum('btghd,bgdp->bghtp', qg, k_n) * scale
    p_all = masked_softmax(jnp.concatenate([ss, ss_n], axis=-1),
                           jnp.concatenate([m_s.reshape(B, G, 1, T, top * page), m_n[:, :, None]], axis=-1))
    o_s = (jnp.einsum('bghtnp,bgtndp->bghtd', p_all[..., :top * page].reshape(B, G, HPG, T, top, page), v_t)
           + jnp.einsum('bghtp,bgdp->bghtd', p_all[..., top * page:], v_n))

    kw = rms_norm(win_all[:, :, 0], g_kw)
    vw = win_all[:, :, 1]
    pw = P - WINDOW + jnp.arange(T + WINDOW)
    m_w = (pw[None, :] <= t[:, None]) & (pw[None, :] > t[:, None] - WINDOW) & (pw[None, :] >= 0)
    sw = jnp.einsum('btghd,bkgd->bghtk', qg, kw) * scale
    p_w = masked_softmax(sw, m_w)
    o_w = jnp.einsum('bghtk,bkgd->bghtd', p_w, vw)

    o = jnp.stack([o_c, o_s, o_w], axis=-1)
    return jnp.einsum('bghtdc,btghc->btghd', o, gates.reshape(B, T, G, HPG, 3)).reshape(B, T, N_QW)


def moe_route(group_logits, expert_logits):
    assert TOP_K_FINE == 2
    N = group_logits.shape[0]
    pg = jax.nn.softmax(group_logits.astype(F32), axis=-1)
    grp = jnp.argmax(pg, axis=-1)
    p_grp = jnp.max(pg, axis=-1)
    le = expert_logits.astype(F32).reshape(N, N_GROUPS, EXPERTS_PER_GROUP)
    pe = jax.nn.softmax(le[jnp.arange(N), grp], axis=-1)
    i1 = jnp.argmax(pe, axis=-1)
    hit1 = jnp.arange(EXPERTS_PER_GROUP)[None, :] == i1[:, None]
    rest = jnp.where(hit1, -jnp.inf, pe)
    i2 = jnp.argmax(rest, axis=-1)
    top_i = jnp.stack([i1, i2], axis=-1)
    top_p = jnp.stack([jnp.max(pe, axis=-1), jnp.max(rest, axis=-1)], axis=-1)
    wts = p_grp[:, None] * top_p / jnp.sum(top_p, axis=-1, keepdims=True)
    eid = grp[:, None] * EXPERTS_PER_GROUP + top_i
    return eid.astype(jnp.int32), wts


MOE_TM = 256


def _moe_ffn_kernel(be_ref, x_ref, w1_ref, w3_ref, w2_ref, wt_ref, o_ref, w1b_ref, w3b_ref, w2b_ref):
    i = pl.program_id(0)

    @pl.when((i == 0) | (be_ref[i] != be_ref[jnp.maximum(i - 1, 0)]))
    def _():
        w1b_ref[...] = w1_ref[0].astype(BF16)
        w3b_ref[...] = w3_ref[0].astype(BF16)
        w2b_ref[...] = w2_ref[0].astype(BF16)

    x = x_ref[...]
    a = _mxu(x, w1b_ref[...])
    b = _mxu(x, w3b_ref[...])
    h = (a * jax.nn.sigmoid(a) * b).astype(BF16)
    o_ref[...] = _mxu(h, w2b_ref[...]) * wt_ref[...]


def moe_ffn(rows, row_wt, blk_e, w1, w3, w2):
    m_pad, d = rows.shape
    n_blk = m_pad // MOE_TM
    f = w1.shape[2]
    return pl.pallas_call(
        _moe_ffn_kernel,
        grid_spec=pltpu.PrefetchScalarGridSpec(
            num_scalar_prefetch=1,
            grid=(n_blk,),
            in_specs=[
                pl.BlockSpec((MOE_TM, d), lambda i, be: (i, 0)),
                pl.BlockSpec((1, d, f), lambda i, be: (be[i], 0, 0)),
                pl.BlockSpec((1, d, f), lambda i, be: (be[i], 0, 0)),
                pl.BlockSpec((1, f, d), lambda i, be: (be[i], 0, 0)),
                pl.BlockSpec((MOE_TM, 1), lambda i, be: (i, 0)),
            ],
            out_specs=pl.BlockSpec((MOE_TM, d), lambda i, be: (i, 0)),
            scratch_shapes=[pltpu.VMEM((d, f), BF16), pltpu.VMEM((d, f), BF16), pltpu.VMEM((f, d), BF16)],
        ),
        out_shape=jax.ShapeDtypeStruct((m_pad, d), F32),
        compiler_params=pltpu.CompilerParams(
            dimension_semantics=("arbitrary",), vmem_limit_bytes=VMEM_LIMIT),
        name="moe_ffn",
    )(blk_e, rows, w1, w3, w2, row_wt)


def hier_moe_apply(x, eid, wts, w_e1, w_e3, w_e2):
    N, D = x.shape
    M = N * TOP_K_FINE
    eid_f = eid.reshape(-1)
    onehot = (eid_f[:, None] == jnp.arange(N_EXPERTS, dtype=jnp.int32)[None, :]).astype(jnp.int32)
    run = jnp.cumsum(onehot, axis=0)
    counts = run[-1]
    rank_in_e = jnp.sum(run * onehot, axis=1) - 1
    padded = (counts + MOE_TM - 1) // MOE_TM * MOE_TM
    p_end = jnp.cumsum(padded)
    p_start = p_end - padded
    pos = (p_start[eid_f] + rank_in_e).astype(jnp.int32)
    n_blk = (M + N_EXPERTS * (MOE_TM - 1) + MOE_TM - 1) // MOE_TM
    m_pad = n_blk * MOE_TM
    src_tok = jnp.zeros((m_pad,), jnp.int32).at[pos].set(jnp.arange(M, dtype=jnp.int32) // TOP_K_FINE)
    row_wt = jnp.zeros((m_pad,), F32).at[pos].set(wts.reshape(-1))
    blk_e = jnp.minimum(jnp.searchsorted(p_end, jnp.arange(n_blk, dtype=jnp.int32) * MOE_TM, side='right'),
                        N_EXPERTS - 1).astype(jnp.int32)
    rows = x[src_tok].astype(BF16)
    out = moe_ffn(rows, row_wt.reshape(m_pad, 1), blk_e, w_e1, w_e3, w_e2)
    pos = pos.reshape(N, TOP_K_FINE)
    return out[pos[:, 0]] + out[pos[:, 1]]


OUT_TM = 256
ROUTER_PAD = 128


def _out_proj_kernel(x_ref, gm_ref, gn_ref, hm_ref, hn_ref, wm_ref, wn_ref, wo_ref, gf_ref, wr_ref, br_ref,
                     x1_ref, h2_ref, lg_ref):
    pm = _mxu(hm_ref[...].astype(BF16), wm_ref[...])
    pn = _mxu(hn_ref[...].astype(BF16), wn_ref[...])
    u = jax.nn.sigmoid(gm_ref[...]) * pm + jax.nn.sigmoid(gn_ref[...]) * pn
    x1 = x_ref[...] + _mxu(u.astype(BF16), wo_ref[...])
    x1_ref[...] = x1
    h2 = x1 * lax.rsqrt(jnp.mean(x1 * x1, axis=-1, keepdims=True) + EPS) * gf_ref[...]
    h2_ref[...] = h2
    lg_ref[...] = _mxu(h2.astype(BF16), wr_ref[...]) + br_ref[...]


def out_proj(x, proj, gate_col_blocks, h_m, h_n, w_proj_m, w_proj_n, w_out, g_ffn, w_group, b_group, w_expert, b_expert):
    n, d = x.shape
    n_r = N_GROUPS + N_EXPERTS
    w_r = jnp.pad(jnp.concatenate([w_group, w_expert], axis=1), ((0, 0), (0, ROUTER_PAD - n_r))).astype(BF16)
    b_r = jnp.pad(jnp.concatenate([b_group, b_expert]), (0, ROUTER_PAD - n_r)).reshape(1, ROUTER_PAD)
    cm, cn = gate_col_blocks
    row = lambda i: (i, 0)
    const = lambda i: (0, 0)
    once = pl.Buffered(1)
    return pl.pallas_call(
        _out_proj_kernel,
        grid=(n // OUT_TM,),
        in_specs=[
            pl.BlockSpec((OUT_TM, d), row),
            pl.BlockSpec((OUT_TM, d), lambda i: (i, cm)),
            pl.BlockSpec((OUT_TM, d), lambda i: (i, cn)),
            pl.BlockSpec((OUT_TM, h_m.shape[1]), row),
            pl.BlockSpec((OUT_TM, h_n.shape[1]), row),
            pl.BlockSpec(w_proj_m.shape, const, pipeline_mode=once),
            pl.BlockSpec(w_proj_n.shape, const, pipeline_mode=once),
            pl.BlockSpec(w_out.shape, const, pipeline_mode=once),
            pl.BlockSpec((1, d), const),
            pl.BlockSpec((d, ROUTER_PAD), const),
            pl.BlockSpec((1, ROUTER_PAD), const),
        ],
        out_specs=[pl.BlockSpec((OUT_TM, d), row), pl.BlockSpec((OUT_TM, d), row),
                   pl.BlockSpec((OUT_TM, ROUTER_PAD), row)],
        out_shape=[jax.ShapeDtypeStruct((n, d), F32), jax.ShapeDtypeStruct((n, d), F32),
                   jax.ShapeDtypeStruct((n, ROUTER_PAD), F32)],
        compiler_params=pltpu.CompilerParams(dimension_semantics=("parallel",), vmem_limit_bytes=VMEM_LIMIT),
        name="out_proj",
    )(x, proj, proj, h_m, h_n, w_proj_m.astype(BF16), w_proj_n.astype(BF16), w_out.astype(BF16),
      g_ffn.reshape(1, d), w_r, b_r)


_SEG = dict(zip(("mq", "mk", "mv", "mi", "mf", "mo", "nq", "nkv", "ng", "gate_m", "gate_n"),
                zip(np.cumsum((0,) + SPLITS[:-1]).tolist(), SPLITS)))
_PROJ_ORDER = ("gate_m", "gate_n", "mv", "mo", "nq", "mq", "mk", "nkv", "mi", "mf", "ng")
_PROJ_OFF = dict(zip(_PROJ_ORDER, np.cumsum([0] + [_SEG[k][1] for k in _PROJ_ORDER[:-1]]).tolist()))
D_IN_PAD = -(-D_IN // 896) * 896


def hybrid_layer(x, conv_prev, C0, n0, m0, paged, win_past, g_mix, w_in, w_conv, b_i, b_f, g_mh, g_q, g_kc, g_ks, g_kw,
                 pe_ck, w_ck1, w_ck2, pe_cv, w_cv1, w_cv2, w_proj_m, w_proj_n, w_out, g_ffn, w_group, b_group,
                 w_expert, b_expert, w_e1, w_e3, w_e2, *, use_pallas_proj):
    B, T, D = x.shape
    WB = win_past.shape[1]
    split_at = np.cumsum(SPLITS)[:-1].tolist()
    if use_pallas_proj:
        w_pad = jnp.concatenate(
            [w_in[:, _SEG[k][0]:_SEG[k][0] + _SEG[k][1]].astype(BF16) for k in _PROJ_ORDER]
            + [jnp.zeros((D, D_IN_PAD - D_IN), BF16)], axis=1)
        proj2d = norm_proj(x.reshape(B * T, D), g_mix, w_pad)
        seg = lambda k: proj2d[:, _PROJ_OFF[k]:_PROJ_OFF[k] + _SEG[k][1]].reshape(B, T, _SEG[k][1])
        mq, mk, mv, mi, mf, mo, nq, nkv, ng = (seg(k) for k in ("mq", "mk", "mv", "mi", "mf", "mo", "nq", "nkv", "ng"))
    else:
        proj = rms_norm(x, g_mix) @ w_in
        mq, mk, mv, mi, mf, mo, nq, nkv, ng, gate_m, gate_n = jnp.split(proj, split_at, axis=-1)
    qk, conv_state = short_conv(jnp.concatenate([mq, mk], axis=-1), conv_prev, w_conv)
    qk = jax.nn.silu(qk)
    q_m = qk[..., :M_QK].reshape(B, T, M_HEADS, M_DQK).astype(F32)
    k_m = qk[..., M_QK:].reshape(B, T, M_HEADS, M_DQK).astype(F32) * (M_DQK ** -0.5)
    v_m = mv.reshape(B, T, M_HEADS, M_DV).astype(F32)
    i_pre = (mi + b_i).astype(F32)
    logf = jax.nn.log_sigmoid((mf + b_f).astype(F32))
    h_m, C1, n1, m1 = mlstm_chunkwise(q_m, k_m, v_m, i_pre, logf, C0.astype(F32), n0.astype(F32), m0.astype(F32))
    h_m = rms_norm(h_m, g_mh).reshape(B, T, M_VW).astype(x.dtype) * jax.nn.sigmoid(mo)
    qn = rms_norm(nq.reshape(B, T, N_HEADS, HEAD_DIM), g_q)
    kv_new = nkv.reshape(B, T, 6, N_KV, HEAD_DIM)
    kv_rows = kv_new[:, :, :4]
    win_all = jnp.concatenate([win_past.astype(x.dtype), kv_new[:, :, 4:]], axis=1)
    if paged is None:
        h_n = nsa_prompt(nq, ng, kv_new, g_q, g_kc, g_ks, g_kw, pe_ck, w_ck1, w_ck2, pe_cv, w_cv1, w_cv2)
    else:
        gates_n = jax.nn.sigmoid(ng).reshape(B, T, N_HEADS, 3)
        h_n = nsa_sample(qn, gates_n, paged[0], paged[1], kv_rows, win_all, g_kc, g_ks, g_kw,
                         pe_ck, w_ck1, w_ck2, pe_cv, w_cv1, w_cv2)
    win_state = win_all[:, -min(WINDOW, WB + T):]
    if use_pallas_proj:
        assert _PROJ_OFF["gate_m"] == 0 and _PROJ_OFF["gate_n"] == D
        x1, h2, logits = out_proj(x.reshape(B * T, D), proj2d, (0, 1), h_m.reshape(B * T, M_VW),
                                  h_n.reshape(B * T, N_QW), w_proj_m, w_proj_n, w_out, g_ffn,
                                  w_group, b_group, w_expert, b_expert)
        x1 = x1.reshape(B, T, D)
        lg, le = logits[:, :N_GROUPS], logits[:, N_GROUPS:N_GROUPS + N_EXPERTS]
    else:
        u = jax.nn.sigmoid(gate_m) * (h_m @ w_proj_m) + jax.nn.sigmoid(gate_n) * (h_n.astype(x.dtype) @ w_proj_n)
        x1 = x + u @ w_out
        h2 = rms_norm(x1, g_ffn).reshape(B * T, D)
        lg, le = h2 @ w_group + b_group, h2 @ w_expert + b_expert
    eid, wts = moe_route(lg, le)
    return x1, h2, eid, wts, kv_rows, win_state, C1, n1, m1, conv_state


def kernel(x_prompt, x_sample, cache_kv, page_table, state_win, state_C, state_n, state_m, state_conv, g_mix, w_in, w_conv, b_i, b_f, g_mh, g_q, g_kc, g_ks, g_kw, pe_ck, w_ck1, w_ck2, pe_cv, w_cv1, w_cv2, w_proj_m, w_proj_n, w_out, g_ffn, w_group, b_group, w_expert, b_expert, w_e1, w_e3, w_e2):
    B = x_prompt.shape[0]
    dt = x_prompt.dtype
    weights = (g_mix, w_in, w_conv, b_i, b_f, g_mh, g_q, g_kc, g_ks, g_kw, pe_ck, w_ck1, w_ck2, pe_cv, w_cv1, w_cv2,
               w_proj_m, w_proj_n, w_out, g_ffn, w_group, b_group, w_expert, b_expert, w_e1, w_e3, w_e2)
    x1_p, h2_p, eid_p, wts_p, kv_p, win_p, C_p, n_p, m_p, conv_p = hybrid_layer(
        x_prompt, jnp.zeros((B, M_CONV - 1, 2 * M_QK), dt), jnp.zeros((B, M_HEADS, M_DQK, M_DV), F32),
        jnp.zeros((B, M_HEADS, M_DQK), F32), jnp.zeros((B, M_HEADS), F32),
        None, jnp.zeros((B, 0, 2, N_KV, HEAD_DIM), dt), *weights, use_pallas_proj=True)
    with jax.default_matmul_precision("highest"):
        x1_s, h2_s, eid_s, wts_s, kv_s, win_s, C_s, n_s, m_s, conv_s = hybrid_layer(
            x_sample, state_conv, state_C, state_n, state_m, (cache_kv, page_table), state_win, *weights,
            use_pallas_proj=False)
    n_p_tok = h2_p.shape[0]
    moe = hier_moe_apply(jnp.concatenate([h2_p, h2_s], axis=0), jnp.concatenate([eid_p, eid_s], axis=0),
                         jnp.concatenate([wts_p, wts_s], axis=0), w_e1, w_e3, w_e2)
    y_p = x1_p + moe[:n_p_tok].reshape(x1_p.shape)
    y_s = x1_s + moe[n_p_tok:].reshape(x1_s.shape)
    return (y_p, y_s, kv_p, kv_s, win_p, win_s, C_p, C_s, n_p, n_s, m_p, m_s, conv_p, conv_s)
```
